```python
import jax, jax.numpy as jnp
from jax import lax
import numpy as np

D_MODEL = 1024
BATCH = 32
SEQ = 256
DEPTH = 2
DEC_BATCH = 8
DEC_SEQ = 4096
PAST_LEN = 512

GRID_W = 64
N_RET_HEADS = 8
RET_DK = 128
RET_DV = 256
RET_QK_W = N_RET_HEADS * RET_DK
RET_V_W = N_RET_HEADS * RET_DV
RET_CHUNK = 128
N_FOURIER_GROUPS = 4
FOURIER_GROUP_W = 256
FOURIER_W = N_FOURIER_GROUPS * FOURIER_GROUP_W
IN_W = 2 * RET_QK_W + 2 * RET_V_W + FOURIER_W + 2 * D_MODEL
N_EXPERTS = 32
TOP_K = 4
D_EXPERT = 1024
SWIGLU_LIMIT = 7.0
SWIGLU_ALPHA = 1.702
MOE_BLOCK = 128
ROPE_BASE = 10000.0
NORM_EPS = 1e-6
GN_EPS = 1e-5

kernel_name = "hybrid_retention_fourier_moe_diffusion_step"

_SPLITS = list(np.cumsum([RET_QK_W, RET_QK_W, RET_V_W, RET_V_W, FOURIER_W, D_MODEL]))


def rms_norm(x, g):
    xf = x.astype(jnp.float32)
    y = xf * lax.rsqrt(jnp.mean(xf * xf, axis=-1, keepdims=True) + NORM_EPS)
    return (y * g.astype(jnp.float32)).astype(x.dtype)


def axial_rotary(x):
    n = x.shape[2]
    rows = n // GRID_W
    row = jnp.repeat(jnp.arange(rows), GRID_W).astype(jnp.float32)
    col = jnp.tile(jnp.arange(GRID_W), rows).astype(jnp.float32)
    n_freq = RET_DK // 4
    inv_freq = ROPE_BASE ** (-jnp.arange(n_freq, dtype=jnp.float32) / n_freq)
    ang = jnp.concatenate([row[:, None] * inv_freq, col[:, None] * inv_freq], axis=-1)
    cos, sin = jnp.cos(ang), jnp.sin(ang)
    x1, x2 = x[..., : RET_DK // 2], x[..., RET_DK // 2:]
    return jnp.concatenate([x1 * cos - x2 * sin, x1 * sin + x2 * cos], axis=-1)


def retention_scan(q, k, v, log_g, s0):
    b, h, n, _ = q.shape
    dv = v.shape[-1]
    nc = n // RET_CHUNK

    def chunks(t):
        return t.reshape(b, h, nc, RET_CHUNK, t.shape[-1]).transpose(2, 0, 1, 3, 4)

    i = jnp.arange(RET_CHUNK, dtype=jnp.float32)
    diff = i[:, None] - i[None, :]
    lg = log_g[:, None, None]
    intra = jnp.where(diff >= 0, jnp.exp(jnp.maximum(diff, 0.0) * lg), 0.0)
    q_dec = jnp.exp((i + 1.0)[:, None] * lg)
    k_dec = jnp.exp((RET_CHUNK - 1.0 - i)[:, None] * lg)
    c_dec = jnp.exp(RET_CHUNK * lg)

    def step(s, qkv):
        qc, kc, vc = qkv
        scores = jnp.einsum("bhid,bhjd->bhij", qc, kc) * intra
        o = (jnp.einsum("bhij,bhjv->bhiv", scores, vc)
             + jnp.einsum("bhid,bhdv->bhiv", qc * q_dec, s))
        s = s * c_dec + jnp.einsum("bhjd,bhjv->bhdv", kc * k_dec, vc)
        return s, o

    s_n, o = lax.scan(step, s0, (chunks(q), chunks(k), chunks(v)))
    o = o.transpose(1, 2, 0, 3, 4).reshape(b, h, n, dv)
    return o, s_n


def bidir_retention(q, k, v, log_g, s0_f, s0_b):
    o_f, s_f = retention_scan(q, k, v, log_g[0], s0_f)
    flip = lambda t: jnp.flip(t, axis=2)
    o_b, s_b = retention_scan(flip(q), flip(k), flip(v), log_g[1], s0_b)
    return o_f + flip(o_b), s_f, s_b


def moe(h, w_router, b_router, w1, b1, w2, b2):
    b, n, d = h.shape
    t = b * n
    xt = h.reshape(t, d)
    logits = (xt @ w_router + b_router).astype(jnp.float32)
    top_v, top_i = lax.top_k(logits, TOP_K)
    wts = jax.nn.softmax(top_v, axis=-1)
    m = t * TOP_K
    flat_e = top_i.reshape(m)
    order = jnp.argsort(flat_e)
    sorted_e = flat_e[order]
    counts = jnp.bincount(flat_e, length=N_EXPERTS)
    padded = ((counts + MOE_BLOCK - 1) // MOE_BLOCK) * MOE_BLOCK
    p_end = jnp.cumsum(padded)
    p_start = p_end - padded
    s_start = jnp.cumsum(counts) - counts
    ppos = p_start[sorted_e] + jnp.arange(m) - s_start[sorted_e]
    p_total = ((m + MOE_BLOCK - 1) // MOE_BLOCK) * MOE_BLOCK + N_EXPERTS * MOE_BLOCK
    n_blk = p_total // MOE_BLOCK
    xp = jnp.zeros((p_total, d), xt.dtype).at[ppos].set(xt[order // TOP_K])
    blk_e = jnp.minimum(jnp.searchsorted(p_end, jnp.arange(n_blk) * MOE_BLOCK, side="right"),
                        N_EXPERTS - 1)

    def expert_block(args):
        xb, e = args
        hh = xb @ w1[e] + b1[e]
        gate, lin = hh[:, :D_EXPERT], hh[:, D_EXPERT:]
        gate = jnp.minimum(gate, SWIGLU_LIMIT)
        lin = jnp.clip(lin, -SWIGLU_LIMIT, SWIGLU_LIMIT)
        act = (lin + 1.0) * (gate * jax.nn.sigmoid(SWIGLU_ALPHA * gate))
        return act @ w2[e] + b2[e]

    yb = lax.map(expert_block, (xp.reshape(n_blk, MOE_BLOCK, d), blk_e))
    y_sorted = yb.reshape(p_total, d)[ppos]
    y_flat = jnp.zeros((m, d), y_sorted.dtype).at[order].set(y_sorted)
    out = jnp.einsum("tk,tkd->td", wts.astype(y_flat.dtype), y_flat.reshape(t, TOP_K, d))
    return out.reshape(b, n, d)


def trunk_layer(x, mod, s0_f, s0_b, latent, norm1_g, norm2_g, w_in, decay_logit, w_ret_o,
                w_four, w_out, w_router, b_router, w1, b1, w2, b2):
    b, n, _ = x.shape
    shift1, scale1, gate1, shift2, scale2, gate2 = [mod[:, j][:, None, :] for j in range(6)]
    h = rms_norm(x, norm1_g) * (1.0 + scale1) + shift1
    u = h @ w_in
    q, k, v, g_ret, u_four, g_a, g_b = jnp.split(u, _SPLITS, axis=-1)

    q = q.reshape(b, n, N_RET_HEADS, RET_DK).transpose(0, 2, 1, 3).astype(jnp.float32) * (RET_DK ** -0.5)
    k = k.reshape(b, n, N_RET_HEADS, RET_DK).transpose(0, 2, 1, 3).astype(jnp.float32)
    v = v.reshape(b, n, N_RET_HEADS, RET_DV).transpose(0, 2, 1, 3).astype(jnp.float32)
    if latent:
        q, k = axial_rotary(q), axial_rotary(k)
    log_g = jax.nn.log_sigmoid(decay_logit.astype(jnp.float32))
    o, s_f, s_b = bidir_retention(q, k, v, log_g, s0_f.astype(jnp.float32), s0_b.astype(jnp.float32))
    mu = jnp.mean(o, axis=-1, keepdims=True)
    var = jnp.mean(jnp.square(o - mu), axis=-1, keepdims=True)
    o = (o - mu) * lax.rsqrt(var + GN_EPS)
    o = o.transpose(0, 2, 1, 3).reshape(b, n, RET_V_W).astype(x.dtype)
    branch_a = (jax.nn.silu(g_ret) * o) @ w_ret_o

    uf = u_four.astype(jnp.float32).reshape(b, n, N_FOURIER_GROUPS, FOURIER_GROUP_W)
    uf = jnp.fft.fft2(uf, axes=(1, 3), norm="ortho").real
    branch_b = uf.reshape(b, n, FOURIER_W).astype(x.dtype) @ w_four

    merged = jax.nn.sigmoid(g_a) * branch_a + jax.nn.sigmoid(g_b) * branch_b
    x = x + gate1 * (merged @ w_out)

    h2 = rms_norm(x, norm2_g) * (1.0 + scale2) + shift2
    x = x + gate2 * moe(h2, w_router, b_router, w1, b1, w2, b2)
    return x, s_f, s_b


def setup_inputs(seed: int = 0) -> dict:
    key = jax.random.key(seed)
    ks = jax.random.split(key, 24)
    f32 = jnp.float32
    nrm = lambda kk, shape, s: jax.random.normal(kk, shape, f32) * s
    base = np.log(2.0 ** (5.0 + np.arange(N_RET_HEADS)) - 1.0).astype(np.float32)
    return {
        "x_prompt": nrm(ks[0], (BATCH, SEQ, D_MODEL), 1.0),
        "x_sample": nrm(ks[1], (DEC_BATCH, DEC_SEQ, D_MODEL), 1.0),
        "state_ret": nrm(ks[2], (DEC_BATCH, DEPTH, 2, N_RET_HEADS, RET_DK, RET_DV), 1.0),
        "c": nrm(ks[3], (DEC_BATCH, D_MODEL), 1.0),
        "c_ctx": nrm(ks[4], (D_MODEL,), 1.0),
        "w_ada": nrm(ks[5], (DEPTH, D_MODEL, 6 * D_MODEL), 0.5 * D_MODEL ** -0.5),
        "b_ada": nrm(ks[6], (DEPTH, 6 * D_MODEL), 0.02),
        "norm1_g": 1.0 + nrm(ks[7], (DEPTH, D_MODEL), 0.02),
        "norm2_g": 1.0 + nrm(ks[8], (DEPTH, D_MODEL), 0.02),
        "w_in": nrm(ks[9], (DEPTH, D_MODEL, IN_W), D_MODEL ** -0.5),
        "ret_decay_logit": jnp.asarray(base)[None, None, :] + nrm(ks[10], (DEPTH, 2, N_RET_HEADS), 0.1),
        "w_ret_o": nrm(ks[11], (DEPTH, RET_V_W, D_MODEL), RET_V_W ** -0.5),
        "w_four": nrm(ks[12], (DEPTH, FOURIER_W, D_MODEL), FOURIER_W ** -0.5),
        "w_out": nrm(ks[13], (DEPTH, D_MODEL, D_MODEL), D_MODEL ** -0.5),
        "w_router": nrm(ks[14], (DEPTH, D_MODEL, N_EXPERTS), D_MODEL ** -0.5),
        "b_router": nrm(ks[15], (DEPTH, N_EXPERTS), 0.01),
        "w1": nrm(ks[16], (DEPTH, N_EXPERTS, D_MODEL, 2 * D_EXPERT), D_MODEL ** -0.5),
        "b1": nrm(ks[17], (DEPTH, N_EXPERTS, 2 * D_EXPERT), 0.02),
        "w2": nrm(ks[18], (DEPTH, N_EXPERTS, D_EXPERT, D_MODEL), D_EXPERT ** -0.5),
        "b2": nrm(ks[19], (DEPTH, N_EXPERTS, D_MODEL), 0.02),
        "final_g": 1.0 + nrm(ks[20], (D_MODEL,), 0.02),
    }


def reference(x_prompt, x_sample, state_ret, c, c_ctx, w_ada, b_ada, norm1_g, norm2_g, w_in,
              ret_decay_logit, w_ret_o, w_four, w_out, w_router, b_router, w1, b1, w2, b2, final_g):
    hp, hs = x_prompt, x_sample
    zero_state = jnp.zeros((x_prompt.shape[0], N_RET_HEADS, RET_DK, RET_DV), jnp.float32)
    layer_states = []
    for l in range(DEPTH):
        mod_ctx = (jax.nn.silu(c_ctx)[None, :] @ w_ada[l] + b_ada[l]).reshape(1, 6, D_MODEL)
        mod_lat = (jax.nn.silu(c) @ w_ada[l] + b_ada[l]).reshape(-1, 6, D_MODEL)
        lw = (norm1_g[l], norm2_g[l], w_in[l], ret_decay_logit[l], w_ret_o[l], w_four[l], w_out[l],
              w_router[l], b_router[l], w1[l], b1[l], w2[l], b2[l])
        hp, s_f, s_b = trunk_layer(hp, mod_ctx, zero_state, zero_state, False, *lw)
        layer_states.append(jnp.stack([s_f, s_b], axis=1))
        hs, _, _ = trunk_layer(hs, mod_lat, state_ret[:, l, 0], state_ret[:, l, 1], True, *lw)
    new_state_ret = jnp.stack(layer_states, axis=1).astype(x_prompt.dtype)
    y_prompt = rms_norm(hp, final_g)
    y_sample = rms_norm(hs, final_g)
    return (y_prompt, y_sample, new_state_ret)
```

```python
import functools

import jax
import jax.numpy as jnp
from jax import lax
from jax.experimental import pallas as pl
from jax.experimental.pallas import tpu as pltpu

F32 = jnp.float32
BF16 = jnp.bfloat16

N_RET_HEADS = 8
RET_DK = 128
RET_DV = 256
RET_QK_W = N_RET_HEADS * RET_DK
RET_V_W = N_RET_HEADS * RET_DV
N_FOURIER_GROUPS = 4
FOURIER_GROUP_W = 256
FOURIER_W = N_FOURIER_GROUPS * FOURIER_GROUP_W
GRID_W = 64
N_EXPERTS = 32
TOP_K = 4
SWIGLU_LIMIT = 7.0
SWIGLU_ALPHA = 1.702
ROPE_BASE = 10000.0
NORM_EPS = 1e-6
GN_EPS = 1e-5

COL_Q = 0
COL_K = COL_Q + RET_QK_W
COL_V = COL_K + RET_QK_W
COL_GRET = COL_V + RET_V_W
COL_FOUR = COL_GRET + RET_V_W
COL_GA = COL_FOUR + FOURIER_W

LANES = 128
VMEM_LIMIT_BYTES = 56 * 1024 * 1024

MOD_ROWS = 8
RET_CHUNK = 256
MOE_BLOCK = 256
E_PAD = LANES


def _cparams(sem, vmem=VMEM_LIMIT_BYTES):
    return pltpu.CompilerParams(dimension_semantics=sem, vmem_limit_bytes=vmem)


def _mod_kernel(c_ref, w_ref, b_ref, o_ref):
    c = c_ref[...]
    s = c * jax.nn.sigmoid(c)
    o_ref[0] = jnp.dot(s, w_ref[0], precision=lax.Precision.HIGHEST,
                       preferred_element_type=F32) + b_ref[0]


def _adaln_mod(cond, w_ada, b_ada):
    depth, d, w6 = w_ada.shape
    r = cond.shape[0]
    tn = 1536
    assert w6 % tn == 0
    return pl.pallas_call(
        _mod_kernel,
        out_shape=jax.ShapeDtypeStruct((depth, r, w6), F32),
        grid=(depth, w6 // tn),
        in_specs=[pl.BlockSpec((r, d), lambda l, j: (0, 0)),
                  pl.BlockSpec((1, d, tn), lambda l, j: (l, 0, j)),
                  pl.BlockSpec((1, 1, tn), lambda l, j: (l, 0, j))],
        out_specs=pl.BlockSpec((1, r, tn), lambda l, j: (l, 0, j)),
        compiler_params=_cparams(("parallel", "parallel")),
        name="adaln_mod",
    )(cond, w_ada, b_ada.reshape(depth, 1, w6))


def _rms_mod(x, g, shift, scale):
    y = x * lax.rsqrt(jnp.mean(x * x, axis=-1, keepdims=True) + NORM_EPS) * g
    return y * (1.0 + scale) + shift


def _in_proj_kernel(grp_ref, x_ref, mod_ref, g_ref, w_ref, o_ref, h_ref):
    del grp_ref

    @pl.when(pl.program_id(1) == 0)
    def _():
        m = mod_ref[0]
        h_ref[...] = _rms_mod(x_ref[...], g_ref[...], m[0:1], m[1:2]).astype(BF16)

    o_ref[...] = jnp.dot(h_ref[...], w_ref[...], preferred_element_type=F32).astype(BF16)


def _in_proj(x, tile_grp, mod, g, w, tm, tn):
    t, d = x.shape
    n = w.shape[1]
    assert t % tm == 0 and n % tn == 0
    grid_spec = pltpu.PrefetchScalarGridSpec(
        num_scalar_prefetch=1,
        grid=(t // tm, n // tn),
        in_specs=[pl.BlockSpec((tm, d), lambda i, j, grp: (i, 0)),
                  pl.BlockSpec((1, MOD_ROWS, d), lambda i, j, grp: (grp[i], 0, 0)),
                  pl.BlockSpec((1, d), lambda i, j, grp: (0, 0)),
                  pl.BlockSpec((d, tn), lambda i, j, grp: (0, j))],
        out_specs=pl.BlockSpec((tm, tn), lambda i, j, grp: (i, j)),
        scratch_shapes=[pltpu.VMEM((tm, d), BF16)],
    )
    return pl.pallas_call(
        _in_proj_kernel,
        out_shape=jax.ShapeDtypeStruct((t, n), BF16),
        grid_spec=grid_spec,
        compiler_params=_cparams(("parallel", "arbitrary")),
        name="in_proj",
    )(tile_grp, x, mod, g.reshape(1, d), w)


def _retention_kernel(cdec_ref, *refs, n, chunk, rotary, has_s0, emit_state):
    refs = list(refs)
    q_ref, k_ref, v_ref, g_ref, dmask_ref, dvec_ref = refs[:6]
    pos = 6
    if rotary:
        cos_ref, sin_ref = refs[pos:pos + 2]
        pos += 2
    if has_s0:
        s0_ref = refs[pos]
        pos += 1
    o_ref = refs[pos]
    pos += 1
    if emit_state:
        sn_ref = refs[pos]
        pos += 1
    qs_ref, ks_ref, sf_ref, sb_ref = refs[pos:pos + 4]

    nc = n // chunk
    cross = has_s0 or nc > 1
    h = pl.program_id(0)
    cdec_f = cdec_ref[2 * h]
    cdec_b = cdec_ref[2 * h + 1]
    tn_dims = (((0,), (0,)), ((), ()))
    nt_dims = (((1,), (1,)), ((), ()))

    def rows(c):
        return pl.ds(pl.multiple_of(c * chunk, chunk), chunk)

    def prep(c):
        r = rows(c)
        q = q_ref[r, :].astype(F32) * (RET_DK ** -0.5)
        k = k_ref[r, :].astype(F32)
        if rotary:
            cs, sn = cos_ref[r, :], sin_ref[r, :]
            q = q * cs + pltpu.roll(q, RET_DK // 2, 1) * sn
            k = k * cs + pltpu.roll(k, RET_DK // 2, 1) * sn
        qs_ref[r, :] = q.astype(BF16)
        ks_ref[r, :] = k.astype(BF16)

    def kv(c, dec):
        r = rows(c)
        kd = (ks_ref[r, :].astype(F32) * dec).astype(BF16)
        return lax.dot_general(kd, v_ref[r, :], tn_dims, preferred_element_type=F32)

    s_init_f = s0_ref[0, 0, 0, 0] if has_s0 else jnp.zeros((RET_DK, RET_DV), F32)
    s_init_b = s0_ref[0, 0, 1, 0] if has_s0 else jnp.zeros((RET_DK, RET_DV), F32)

    def fwd_step(c, s):
        prep(c)
        sf_ref[c] = s.astype(BF16)
        return s * cdec_f + kv(c, dvec_ref[0, 1])

    def bwd_step(i, s):
        c = nc - 1 - i
        sb_ref[c] = s.astype(BF16)
        return s * cdec_b + kv(c, dvec_ref[0, 3])

    if nc == 1:
        s_f = fwd_step(0, s_init_f)
        s_b = bwd_step(0, s_init_b)
    else:
        s_f = lax.fori_loop(0, nc, fwd_step, s_init_f)
        s_b = lax.fori_loop(0, nc, bwd_step, s_init_b)
    if emit_state:
        sn_ref[0, 0, 0] = s_f
        sn_ref[0, 1, 0] = s_b

    def out_step(c, carry):
        r = rows(c)
        qc = qs_ref[r, :]
        sc = lax.dot_general(qc, ks_ref[r, :], nt_dims, preferred_element_type=F32)
        o = jnp.dot((sc * dmask_ref[0]).astype(BF16), v_ref[r, :], preferred_element_type=F32)
        if cross:
            qf = qc.astype(F32)
            o = o + jnp.dot((qf * dvec_ref[0, 0]).astype(BF16), sf_ref[c], preferred_element_type=F32)
            o = o + jnp.dot((qf * dvec_ref[0, 2]).astype(BF16), sb_ref[c], preferred_element_type=F32)
        mu = jnp.mean(o, axis=-1, keepdims=True)
        oc = o - mu
        var = jnp.mean(oc * oc, axis=-1, keepdims=True)
        gt = g_ref[r, :].astype(F32)
        o_ref[r, :] = (gt * jax.nn.sigmoid(gt) * (oc * lax.rsqrt(var + GN_EPS))).astype(BF16)
        return carry

    if nc == 1:
        out_step(0, 0)
    else:
        lax.fori_loop(0, nc, out_step, 0)


def _decay_tables(log_g, chunk):
    i = jnp.arange(chunk, dtype=F32)
    diff = i[:, None] - i[None, :]
    lf = log_g[0][:, None, None]
    lb = log_g[1][:, None, None]
    dmask = (jnp.where(diff >= 0, jnp.exp(jnp.maximum(diff, 0.0) * lf), 0.0)
             + jnp.where(diff <= 0, jnp.exp(jnp.maximum(-diff, 0.0) * lb), 0.0))
    lf2, lb2 = log_g[0][:, None], log_g[1][:, None]
    vecs = jnp.stack([jnp.exp((i + 1.0)[None, :] * lf2),
                      jnp.exp((chunk - 1.0 - i)[None, :] * lf2),
                      jnp.exp((chunk - i)[None, :] * lb2),
                      jnp.exp(i[None, :] * lb2)], axis=1)
    dvec = jnp.broadcast_to(vecs[..., None], vecs.shape + (LANES,))
    cdec = jnp.exp(chunk * log_g).T.reshape(-1)
    return dmask, dvec, cdec


def _retention(u, og_prev, *, n, nseq, row0, tables, rot, s0, layer, emit_state):
    t = u.shape[0]
    chunk = min(RET_CHUNK, n)
    assert n % chunk == 0 and row0 % n == 0
    rb = row0 // n
    dmask, dvec, cdec = tables
    nc = n // chunk
    qb, vb = COL_K // RET_DK, COL_V // RET_DV
    gb = COL_GRET // RET_DV
    in_specs = [pl.BlockSpec((n, RET_DK), lambda h, b, cd: (rb + b, h)),
                pl.BlockSpec((n, RET_DK), lambda h, b, cd: (rb + b, qb + h)),
                pl.BlockSpec((n, RET_DV), lambda h, b, cd: (rb + b, vb + h)),
                pl.BlockSpec((n, RET_DV), lambda h, b, cd: (rb + b, gb + h)),
                pl.BlockSpec((1, chunk, chunk), lambda h, b, cd: (h, 0, 0)),
                pl.BlockSpec((1, 4, chunk, LANES), lambda h, b, cd: (h, 0, 0, 0))]
    args = [u, u, u, u, dmask, dvec]
    if rot is not None:
        in_specs += [pl.BlockSpec((n, RET_DK), lambda h, b, cd: (0, 0))] * 2
        args += list(rot)
    if s0 is not None:
        in_specs.append(pl.BlockSpec((1, 1, 2, 1, RET_DK, RET_DV),
                                     lambda h, b, cd: (b, layer, 0, h, 0, 0)))
        args.append(s0)
    out_shape = [jax.ShapeDtypeStruct((t, RET_V_W), BF16)]
    out_specs = [pl.BlockSpec((n, RET_DV), lambda h, b, cd: (rb + b, h))]
    if emit_state:
        out_shape.append(jax.ShapeDtypeStruct((nseq, 2, N_RET_HEADS, RET_DK, RET_DV), F32))
        out_specs.append(pl.BlockSpec((1, 2, 1, RET_DK, RET_DV), lambda h, b, cd: (b, 0, h, 0, 0)))
    aliases = {}
    if og_prev is not None:
        in_specs.append(pl.BlockSpec(memory_space=pl.ANY))
        args.append(og_prev)
        aliases = {len(args): 0}
    kern = functools.partial(_retention_kernel, n=n, chunk=chunk, rotary=rot is not None,
                             has_s0=s0 is not None, emit_state=emit_state)
    if og_prev is not None:
        kern = _drop_ref(kern, 1 + len(args) - 1)
    grid_spec = pltpu.PrefetchScalarGridSpec(
        num_scalar_prefetch=1, grid=(N_RET_HEADS, nseq), in_specs=in_specs, out_specs=out_specs,
        scratch_shapes=[pltpu.VMEM((n, RET_DK), BF16), pltpu.VMEM((n, RET_DK), BF16),
                        pltpu.VMEM((nc, RET_DK, RET_DV), BF16), pltpu.VMEM((nc, RET_DK, RET_DV), BF16)])
    outs = pl.pallas_call(
        kern, out_shape=out_shape, grid_spec=grid_spec, input_output_aliases=aliases,
        compiler_params=_cparams(("parallel", "parallel")),
        name="retention_lat" if rot is not None else "retention_ctx",
    )(cdec, *args)
    return outs if emit_state else outs[0]


def _drop_ref(kern, idx):
    def wrapped(*refs):
        return kern(*refs[:idx], *refs[idx + 1:])
    return wrapped


def _rotary_tables(n):
    rows = n // GRID_W
    row = jnp.repeat(jnp.arange(rows), GRID_W).astype(F32)
    col = jnp.tile(jnp.arange(GRID_W), rows).astype(F32)
    n_freq = RET_DK // 4
    inv_freq = ROPE_BASE ** (-jnp.arange(n_freq, dtype=F32) / n_freq)
    ang = jnp.concatenate([row[:, None] * inv_freq, col[:, None] * inv_freq], axis=-1)
    cos, sin = jnp.cos(ang), jnp.sin(ang)
    return jnp.concatenate([cos, cos], axis=-1), jnp.concatenate([-sin, sin], axis=-1)


def _chan_dft_kernel(u_ref, cs_ref, zc_ref, zs_ref):
    gw = FOURIER_GROUP_W
    for g in range(N_FOURIER_GROUPS):
        r = jnp.dot(u_ref[:, g * gw:(g + 1) * gw], cs_ref[...], preferred_element_type=F32)
        zc_ref[:, g * gw:(g + 1) * gw] = r[:, :gw].astype(BF16)
        zs_ref[:, g * gw:(g + 1) * gw] = r[:, gw:].astype(BF16)


def _chan_dft(u, cs, tm):
    t = u.shape[0]
    fb = COL_FOUR // FOURIER_W
    out = jax.ShapeDtypeStruct((t, FOURIER_W), BF16)
    return pl.pallas_call(
        _chan_dft_kernel,
        out_shape=[out, out],
        grid=(t // tm,),
        in_specs=[pl.BlockSpec((tm, FOURIER_W), lambda i: (i, fb)),
                  pl.BlockSpec((FOURIER_GROUP_W, 2 * FOURIER_GROUP_W), lambda i: (0, 0))],
        out_specs=[pl.BlockSpec((tm, FOURIER_W), lambda i: (i, 0))] * 2,
        compiler_params=_cparams(("parallel",)),
        name="chan_dft",
    )(u, cs)


def _pos_dft_kernel(dc_ref, ds_ref, zc_ref, zs_ref, o_ref):
    o = jnp.dot(dc_ref[...], zc_ref[...], preferred_element_type=F32)
    o = o + jnp.dot(ds_ref[...], zs_ref[...], preferred_element_type=F32)
    o_ref[...] = o.astype(BF16)


def _pos_dft(zc, zs, uf_prev, *, n, nseq, row0, dmats, tr, tc):
    t = zc.shape[0]
    dc, ds = dmats
    assert n % tr == 0 and FOURIER_W % tc == 0 and row0 % n == 0
    rb = row0 // n
    nr = n // tr
    in_specs = [pl.BlockSpec((tr, n), lambda b, j, i: (i, 0)),
                pl.BlockSpec((tr, n), lambda b, j, i: (i, 0)),
                pl.BlockSpec((n, tc), lambda b, j, i: (rb + b, j)),
                pl.BlockSpec((n, tc), lambda b, j, i: (rb + b, j))]
    args = [dc, ds, zc, zs]
    aliases = {}
    kern = _pos_dft_kernel
    if uf_prev is not None:
        in_specs.append(pl.BlockSpec(memory_space=pl.ANY))
        args.append(uf_prev)
        aliases = {4: 0}
        kern = _drop_ref(kern, 4)
    return pl.pallas_call(
        kern,
        out_shape=jax.ShapeDtypeStruct((t, FOURIER_W), BF16),
        grid=(nseq, FOURIER_W // tc, nr),
        in_specs=in_specs,
        out_specs=pl.BlockSpec((tr, tc), lambda b, j, i: ((rb + b) * nr + i, j)),
        input_output_aliases=aliases,
        compiler_params=_cparams(("parallel", "parallel", "parallel")),
        name="pos_dft",
    )(*args)


def _dft_mats(n, scale):
    j = jnp.arange(n, dtype=jnp.int32)
    m = (j[:, None] * j[None, :]) % n
    ang = m.astype(F32) * (2.0 * jnp.pi / n)
    return (jnp.cos(ang) * scale).astype(BF16), (-jnp.sin(ang) * scale).astype(BF16)


def _merge_kernel(grp_ref, x_ref, og_ref, uf_ref, ga_ref, gb_ref, mod_ref, g2_ref,
                  wro_ref, wf_ref, wo_ref, wr_ref, br_ref,
                  xo_ref, h2_ref, ti_ref, tw_ref, rk_ref, cnt_ref, run_ref):
    del grp_ref
    tm = x_ref.shape[0]

    @pl.when(pl.program_id(0) == 0)
    def _():
        run_ref[...] = jnp.zeros_like(run_ref)

    a = jnp.dot(og_ref[...], wro_ref[...], preferred_element_type=F32)
    b = jnp.dot(uf_ref[...], wf_ref[...], preferred_element_type=F32)
    merged = (jax.nn.sigmoid(ga_ref[...].astype(F32)) * a
              + jax.nn.sigmoid(gb_ref[...].astype(F32)) * b)
    y = jnp.dot(merged.astype(BF16), wo_ref[...], preferred_element_type=F32)
    m = mod_ref[0]
    x = x_ref[...] + m[2:3] * y
    xo_ref[...] = x
    h2 = _rms_mod(x, g2_ref[...], m[3:4], m[4:5])
    h2_ref[...] = h2
    logits = jnp.dot(h2, wr_ref[...], precision=lax.Precision.HIGHEST,
                     preferred_element_type=F32) + br_ref[...]

    lane = lax.broadcasted_iota(jnp.int32, (tm, E_PAD), 1)
    rest = logits
    vals, idxs, sels = [], [], []
    for _ in range(TOP_K):
        mx = jnp.max(rest, axis=-1, keepdims=True)
        idx = jnp.min(jnp.where(rest == mx, lane, E_PAD), axis=-1, keepdims=True)
        sel = lane == idx
        vals.append(mx)
        idxs.append(idx)
        sels.append(sel)
        rest = jnp.where(sel, -jnp.inf, rest)
    exps = [jnp.exp(v - vals[0]) for v in vals]
    den = exps[0] + exps[1] + exps[2] + exps[3]

    picked = jnp.zeros((tm, E_PAD), F32)
    for sel in sels:
        picked = picked + sel.astype(F32)
    ri = lax.broadcasted_iota(jnp.int32, (tm, tm), 0)
    ci = lax.broadcasted_iota(jnp.int32, (tm, tm), 1)
    lower = (ri > ci).astype(BF16)
    before = jnp.dot(lower, picked.astype(BF16), preferred_element_type=F32) + run_ref[...]

    ti = jnp.zeros((tm, E_PAD), jnp.int32)
    tw = jnp.zeros((tm, E_PAD), F32)
    rk = jnp.zeros((tm, E_PAD), jnp.int32)
    for k in range(TOP_K):
        rank = jnp.sum(jnp.where(sels[k], before, 0.0), axis=-1, keepdims=True).astype(jnp.int32)
        ti = jnp.where(lane == k, idxs[k], ti)
        tw = jnp.where(lane == k, exps[k] / den, tw)
        rk = jnp.where(lane == k, rank, rk)
    ti_ref[...] = ti
    tw_ref[...] = tw
    rk_ref[...] = rk
    run_ref[...] = run_ref[...] + jnp.sum(picked, axis=0, keepdims=True)
    cnt_ref[...] = run_ref[...]


def _merge(x, og, uf, u, tile_grp, mod, g2, wro, wf, wo, wr, br, tm):
    t, d = x.shape
    ga_b, gb_b = COL_GA // d, COL_GA // d + 1
    full = lambda shape: pl.BlockSpec(shape, lambda i, grp: (0,) * len(shape))
    row = lambda w: pl.BlockSpec((tm, w), lambda i, grp: (i, 0))
    grid_spec = pltpu.PrefetchScalarGridSpec(
        num_scalar_prefetch=1,
        grid=(t // tm,),
        in_specs=[row(d), row(RET_V_W), row(FOURIER_W),
                  pl.BlockSpec((tm, d), lambda i, grp: (i, ga_b)),
                  pl.BlockSpec((tm, d), lambda i, grp: (i, gb_b)),
                  pl.BlockSpec((1, MOD_ROWS, d), lambda i, grp: (grp[i], 0, 0)),
                  full((1, d)), full(wro.shape), full(wf.shape), full(wo.shape),
                  full(wr.shape), full(br.shape)],
        out_specs=[row(d), row(d), row(E_PAD), row(E_PAD), row(E_PAD), full((1, E_PAD))],
        scratch_shapes=[pltpu.VMEM((1, E_PAD), F32)],
    )
    return pl.pallas_call(
        _merge_kernel,
        out_shape=[jax.ShapeDtypeStruct((t, d), F32), jax.ShapeDtypeStruct((t, d), F32),
                   jax.ShapeDtypeStruct((t, E_PAD), jnp.int32), jax.ShapeDtypeStruct((t, E_PAD), F32),
                   jax.ShapeDtypeStruct((t, E_PAD), jnp.int32), jax.ShapeDtypeStruct((1, E_PAD), F32)],
        grid_spec=grid_spec,
        compiler_params=_cparams(("arbitrary",)),
        name="merge_route",
    )(tile_grp, x, og, uf, u, u, mod, g2.reshape(1, d), wro, wf, wo, wr, br)


def _dispatch_kernel(pp_ref, h_ref, xz_ref, xp_ref, sem):
    del xz_ref
    tm = h_ref.shape[0]

    def issue(t, carry):
        for k in range(TOP_K):
            p = pp_ref[t * TOP_K + k]
            pltpu.make_async_copy(h_ref.at[pl.ds(t, 1)], xp_ref.at[pl.ds(p, 1)], sem).start()
        return carry

    lax.fori_loop(0, tm, issue, 0)
    for _ in range(TOP_K):
        pltpu.make_async_copy(h_ref, xp_ref.at[pl.ds(0, tm)], sem).wait()


def _dispatch(h2, ppos, p_total, tm):
    t, d = h2.shape
    xz = jnp.zeros((p_total, d), F32)
    return pl.pallas_call(
        _dispatch_kernel,
        out_shape=jax.ShapeDtypeStruct((p_total, d), F32),
        grid=(t // tm,),
        in_specs=[pl.BlockSpec((tm * TOP_K,), lambda i: (i,), memory_space=pltpu.SMEM),
                  pl.BlockSpec((tm, d), lambda i: (i, 0)),
                  pl.BlockSpec(memory_space=pl.ANY)],
        out_specs=pl.BlockSpec(memory_space=pl.ANY),
        scratch_shapes=[pltpu.SemaphoreType.DMA],
        input_output_aliases={2: 0},
        compiler_params=_cparams(("arbitrary",)),
        name="moe_dispatch",
    )(ppos, h2, xz)


def _expert_kernel(be_ref, nu_ref, x_ref, w1_ref, b1_ref, w2_ref, b2_ref, y_ref):
    del be_ref
    de = w2_ref.shape[1]

    @pl.when(pl.program_id(0) < nu_ref[0])
    def _():
        hh = jnp.dot(x_ref[...].astype(BF16), w1_ref[0], preferred_element_type=F32) + b1_ref[0]
        gate = jnp.minimum(hh[:, :de], SWIGLU_LIMIT)
        lin = jnp.clip(hh[:, de:], -SWIGLU_LIMIT, SWIGLU_LIMIT)
        act = (lin + 1.0) * (gate * jax.nn.sigmoid(SWIGLU_ALPHA * gate))
        y_ref[...] = jnp.dot(act.astype(BF16), w2_ref[0], preferred_element_type=F32) + b2_ref[0]

    @pl.when(pl.program_id(0) >= nu_ref[0])
    def _():
        y_ref[...] = jnp.zeros_like(y_ref)


def _experts(xp, blk_e, n_used, w1, b1, w2, b2):
    p_total, d = xp.shape
    ne, _, de2 = w1.shape
    de = w2.shape[1]
    bm = MOE_BLOCK
    grid_spec = pltpu.PrefetchScalarGridSpec(
        num_scalar_prefetch=2,
        grid=(p_total // bm,),
        in_specs=[pl.BlockSpec((bm, d), lambda i, be, nu: (i, 0)),
                  pl.BlockSpec((1, d, de2), lambda i, be, nu: (be[i], 0, 0)),
                  pl.BlockSpec((1, 1, de2), lambda i, be, nu: (be[i], 0, 0)),
                  pl.BlockSpec((1, de, d), lambda i, be, nu: (be[i], 0, 0)),
                  pl.BlockSpec((1, 1, d), lambda i, be, nu: (be[i], 0, 0))],
        out_specs=pl.BlockSpec((bm, d), lambda i, be, nu: (i, 0)),
    )
    return pl.pallas_call(
        _expert_kernel,
        out_shape=jax.ShapeDtypeStruct((p_total, d), F32),
        grid_spec=grid_spec,
        compiler_params=_cparams(("arbitrary",)),
        name="moe_experts",
    )(blk_e, n_used, xp, w1, b1.reshape(ne, 1, de2), w2, b2.reshape(ne, 1, d))


def _combine_kernel(grp_ref, pp_ref, x_ref, tw_ref, mod_ref, fg_ref, yb_ref, o_ref, buf_ref, sem,
                    *, final_norm):
    del grp_ref
    tm = x_ref.shape[0]

    def issue(t, carry):
        for k in range(TOP_K):
            p = pp_ref[t * TOP_K + k]
            pltpu.make_async_copy(yb_ref.at[pl.ds(p, 1)], buf_ref.at[k, pl.ds(t, 1)], sem).start()
        return carry

    lax.fori_loop(0, tm, issue, 0)
    for k in range(TOP_K):
        pltpu.make_async_copy(yb_ref.at[pl.ds(0, tm)], buf_ref.at[k], sem).wait()
    tw = tw_ref[...]
    acc = tw[:, 0:1] * buf_ref[0]
    for k in range(1, TOP_K):
        acc = acc + tw[:, k:k + 1] * buf_ref[k]
    x = x_ref[...] + mod_ref[0][5:6] * acc
    if final_norm:
        x = x * lax.rsqrt(jnp.mean(x * x, axis=-1, keepdims=True) + NORM_EPS) * fg_ref[...]
    o_ref[...] = x


def _combine(x, yb, ppos, tw, tile_grp, mod, final_g, tm, final_norm):
    t, d = x.shape
    grid_spec = pltpu.PrefetchScalarGridSpec(
        num_scalar_prefetch=1,
        grid=(t // tm,),
        in_specs=[pl.BlockSpec((tm * TOP_K,), lambda i, grp: (i,), memory_space=pltpu.SMEM),
                  pl.BlockSpec((tm, d), lambda i, grp: (i, 0)),
                  pl.BlockSpec((tm, E_PAD), lambda i, grp: (i, 0)),
                  pl.BlockSpec((1, MOD_ROWS, d), lambda i, grp: (grp[i], 0, 0)),
                  pl.BlockSpec((1, d), lambda i, grp: (0, 0)),
                  pl.BlockSpec(memory_space=pl.ANY)],
        out_specs=pl.BlockSpec((tm, d), lambda i, grp: (i, 0)),
        scratch_shapes=[pltpu.VMEM((TOP_K, tm, d), F32), pltpu.SemaphoreType.DMA],
    )
    return pl.pallas_call(
        functools.partial(_combine_kernel, final_norm=final_norm),
        out_shape=jax.ShapeDtypeStruct((t, d), F32),
        grid_spec=grid_spec,
        compiler_params=_cparams(("arbitrary",)),
        name="moe_combine",
    )(tile_grp, ppos, x, tw, mod, final_g.reshape(1, d), yb)


def _route_layout(ti, rk, cnt, t):
    bm = MOE_BLOCK
    counts = cnt[0, :N_EXPERTS].astype(jnp.int32)
    padded = ((counts + bm - 1) // bm) * bm
    p_end = jnp.cumsum(padded)
    p_start = p_end - padded
    ti4 = ti[:, :TOP_K]
    onehot = ti4[:, :, None] == jnp.arange(N_EXPERTS, dtype=jnp.int32)
    ppos = jnp.sum(jnp.where(onehot, p_start, 0), axis=-1) + rk[:, :TOP_K]
    assert (t * TOP_K) % bm == 0
    n_blk = (t * TOP_K) // bm + N_EXPERTS
    blk_e = jnp.minimum(jnp.searchsorted(p_end, jnp.arange(n_blk, dtype=jnp.int32) * bm, side="right"),
                        N_EXPERTS - 1).astype(jnp.int32)
    n_used = (p_end[-1:] // bm).astype(jnp.int32)
    return ppos.reshape(-1).astype(jnp.int32), blk_e, n_used, n_blk * bm


def _tile_groups(t, t_ctx, dec_seq, tm):
    starts = jnp.arange(t // tm, dtype=jnp.int32) * tm
    return jnp.where(starts < t_ctx, 0, 1 + (starts - t_ctx) // dec_seq).astype(jnp.int32)


def kernel(x_prompt, x_sample, state_ret, c, c_ctx, w_ada, b_ada, norm1_g, norm2_g, w_in, ret_decay_logit, w_ret_o, w_four, w_out, w_router, b_router, w1, b1, w2, b2, final_g):
    batch, seq, d = x_prompt.shape
    dec_batch, dec_seq, _ = x_sample.shape
    depth = w_in.shape[0]
    t_ctx, t_lat = batch * seq, dec_batch * dec_seq
    t = t_ctx + t_lat
    tm_in = min(1024, seq * batch, dec_seq)
    tm = min(512, seq * batch, dec_seq)
    assert t_ctx % tm_in == 0 and dec_seq % tm_in == 0 and t_ctx % dec_seq == 0

    x = jnp.concatenate([x_prompt.reshape(t_ctx, d), x_sample.reshape(t_lat, d)], axis=0)

    n_cond = 16
    cond = jnp.zeros((n_cond, d), F32).at[0].set(c_ctx).at[1:1 + dec_batch].set(c)
    mod = _adaln_mod(cond, w_ada, b_ada).reshape(depth, n_cond, 6, d)
    mod = jnp.pad(mod, ((0, 0), (0, 0), (0, MOD_ROWS - 6), (0, 0)))
    grp_in = _tile_groups(t, t_ctx, dec_seq, tm_in)
    grp = _tile_groups(t, t_ctx, dec_seq, tm)

    rot = _rotary_tables(dec_seq)
    gw = FOURIER_GROUP_W
    jc = jnp.arange(gw, dtype=jnp.int32)
    ang_c = ((jc[:, None] * jc[None, :]) % gw).astype(F32) * (2.0 * jnp.pi / gw)
    cs = jnp.concatenate([jnp.cos(ang_c), jnp.sin(ang_c)], axis=1).astype(BF16)
    dm_ctx = _dft_mats(seq, (seq * gw) ** -0.5)
    dm_lat = _dft_mats(dec_seq, (dec_seq * gw) ** -0.5)
    log_g = jax.nn.log_sigmoid(ret_decay_logit.astype(F32))
    wr_pad = jnp.pad(w_router, ((0, 0), (0, 0), (0, E_PAD - N_EXPERTS)))
    br_pad = jnp.pad(b_router, ((0, 0), (0, E_PAD - N_EXPERTS)), constant_values=-1e30)

    states = []
    for l in range(depth):
        u = _in_proj(x, grp_in, mod[l], norm1_g[l], w_in[l].astype(BF16), tm_in, 1536)
        tables = _decay_tables(log_g[l], min(RET_CHUNK, seq, dec_seq))
        og, st = _retention(u, None, n=seq, nseq=batch, row0=0, tables=tables, rot=None, s0=None,
                            layer=l, emit_state=True)
        og = _retention(u, og, n=dec_seq, nseq=dec_batch, row0=t_ctx, tables=tables, rot=rot,
                        s0=state_ret, layer=l, emit_state=False)
        states.append(st)
        zc, zs = _chan_dft(u, cs, tm)
        uf = _pos_dft(zc, zs, None, n=seq, nseq=batch, row0=0, dmats=dm_ctx, tr=seq, tc=FOURIER_W)
        uf = _pos_dft(zc, zs, uf, n=dec_seq, nseq=dec_batch, row0=t_ctx, dmats=dm_lat,
                      tr=min(512, dec_seq), tc=512)
        x, h2, ti, tw, rk, cnt = _merge(
            x, og, uf, u, grp, mod[l], norm2_g[l], w_ret_o[l].astype(BF16), w_four[l].astype(BF16),
            w_out[l].astype(BF16), wr_pad[l], br_pad[l].reshape(1, E_PAD), tm)
        ppos, blk_e, n_used, p_total = _route_layout(ti, rk, cnt, t)
        xp = _dispatch(h2, ppos, p_total, tm)
        yb = _experts(xp, blk_e, n_used, w1[l].astype(BF16), b1[l], w2[l].astype(BF16), b2[l])
        x = _combine(x, yb, ppos, tw, grp, mod[l], final_g, tm, final_norm=(l == depth - 1))

    new_state = jnp.stack(states, axis=1).astype(x_prompt.dtype)
    return (x[:t_ctx].reshape(batch, seq, d), x[t_ctx:].reshape(dec_batch, dec_seq, d), new_state)
```

```python
import functools

import jax
import jax.numpy as jnp
from jax import lax
from jax.experimental import pallas as pl
from jax.experimental.pallas import tpu as pltpu

F32 = jnp.float32
BF16 = jnp.bfloat16

N_RET_HEADS = 8
RET_DK = 128
RET_DV = 256
RET_QK_W = N_RET_HEADS * RET_DK
RET_V_W = N_RET_HEADS * RET_DV
N_FOURIER_GROUPS = 4
FOURIER_GROUP_W = 256
FOURIER_W = N_FOURIER_GROUPS * FOURIER_GROUP_W
GRID_W = 64
N_EXPERTS = 32
TOP_K = 4
SWIGLU_LIMIT = 7.0
SWIGLU_ALPHA = 1.702
ROPE_BASE = 10000.0
NORM_EPS = 1e-6
GN_EPS = 1e-5

COL_Q = 0
COL_K = COL_Q + RET_QK_W
COL_V = COL_K + RET_QK_W
COL_GRET = COL_V + RET_V_W
COL_FOUR = COL_GRET + RET_V_W
COL_GA = COL_FOUR + FOURIER_W

LANES = 128
VMEM_LIMIT_BYTES = 56 * 1024 * 1024

MOD_ROWS = 8
RET_CHUNK = 256
MOE_BLOCK = 256
E_PAD = LANES


def _cparams(sem, vmem=VMEM_LIMIT_BYTES):
    return pltpu.CompilerParams(dimension_semantics=sem, vmem_limit_bytes=vmem)


def _drop_ref(kern, idx):
    def wrapped(*refs):
        return kern(*refs[:idx], *refs[idx + 1:])
    return wrapped


def _mod_kernel(c_ref, w_ref, b_ref, o_ref):
    c = c_ref[...]
    s = c * jax.nn.sigmoid(c)
    o_ref[0] = jnp.dot(s, w_ref[0], precision=lax.Precision.HIGHEST,
                       preferred_element_type=F32) + b_ref[0]


def _adaln_mod(cond, w_ada, b_ada):
    depth, d, w6 = w_ada.shape
    r = cond.shape[0]
    tn = 1536
    assert w6 % tn == 0
    return pl.pallas_call(
        _mod_kernel,
        out_shape=jax.ShapeDtypeStruct((depth, r, w6), F32),
        grid=(depth, w6 // tn),
        in_specs=[pl.BlockSpec((r, d), lambda l, j: (0, 0)),
                  pl.BlockSpec((1, d, tn), lambda l, j: (l, 0, j)),
                  pl.BlockSpec((1, 1, tn), lambda l, j: (l, 0, j))],
        out_specs=pl.BlockSpec((1, r, tn), lambda l, j: (l, 0, j)),
        compiler_params=_cparams(("parallel", "parallel")),
        name="adaln_mod",
    )(cond, w_ada, b_ada.reshape(depth, 1, w6))


def _stream_specs(xs, tm, imap_args):
    d = xs[0].shape[1]
    if len(xs) == 1:
        return [pl.BlockSpec((tm, d), lambda i, *a: (i, 0))]
    nct = xs[0].shape[0] // tm
    return [pl.BlockSpec((tm, d), lambda i, *a: (jnp.minimum(i, nct - 1), 0)),
            pl.BlockSpec((tm, d), lambda i, *a: (jnp.maximum(i - nct, 0), 0))]


def _stream_tile(x_refs, i, nct):
    if len(x_refs) == 1:
        return x_refs[0][...]
    return jnp.where(i < nct, x_refs[0][...], x_refs[1][...])


def _rms_mod(x, g, shift, scale):
    y = x * lax.rsqrt(jnp.mean(x * x, axis=-1, keepdims=True) + NORM_EPS) * g
    return y * (1.0 + scale) + shift


def _in_proj_kernel(grp_ref, *refs, n_x, nct):
    del grp_ref
    x_refs = refs[:n_x]
    mod_ref, g_ref, w_ref, o_ref, h_ref = refs[n_x:]

    @pl.when(pl.program_id(1) == 0)
    def _():
        m = mod_ref[0]
        x = _stream_tile(x_refs, pl.program_id(0), nct)
        h_ref[...] = _rms_mod(x, g_ref[...], m[0:1], m[1:2]).astype(BF16)

    o_ref[...] = jnp.dot(h_ref[...], w_ref[...], preferred_element_type=F32).astype(BF16)


def _in_proj(xs, tile_grp, mod, g, w, tm, tn):
    t = sum(x.shape[0] for x in xs)
    d, n = w.shape
    assert t % tm == 0 and n % tn == 0 and all(x.shape[0] % tm == 0 for x in xs)
    x_specs = [pl.BlockSpec(s.block_shape, (lambda f: lambda i, j, grp: f(i))(s.index_map))
               for s in _stream_specs(xs, tm, 1)]
    grid_spec = pltpu.PrefetchScalarGridSpec(
        num_scalar_prefetch=1,
        grid=(t // tm, n // tn),
        in_specs=x_specs + [pl.BlockSpec((1, MOD_ROWS, d), lambda i, j, grp: (grp[i], 0, 0)),
                            pl.BlockSpec((1, d), lambda i, j, grp: (0, 0)),
                            pl.BlockSpec((d, tn), lambda i, j, grp: (0, j))],
        out_specs=pl.BlockSpec((tm, tn), lambda i, j, grp: (i, j)),
        scratch_shapes=[pltpu.VMEM((tm, d), BF16)],
    )
    return pl.pallas_call(
        functools.partial(_in_proj_kernel, n_x=len(xs), nct=xs[0].shape[0] // tm),
        out_shape=jax.ShapeDtypeStruct((t, n), BF16),
        grid_spec=grid_spec,
        compiler_params=_cparams(("parallel", "arbitrary")),
        name="in_proj",
    )(tile_grp, *xs, mod, g.reshape(1, d), w)


def _retention_kernel(cdec_ref, *refs, n, chunk, rotary, has_s0, emit_state):
    refs = list(refs)
    q_ref, k_ref, v_ref, g_ref, dmask_ref, dvec_ref = refs[:6]
    pos = 6
    if rotary:
        cos_ref, sin_ref = refs[pos:pos + 2]
        pos += 2
    if has_s0:
        s0_ref = refs[pos]
        pos += 1
    o_ref = refs[pos]
    pos += 1
    if emit_state:
        sn_ref = refs[pos]
        pos += 1
    qs_ref, ks_ref, kv_ref, st_ref, s_ref = refs[pos:pos + 5]

    nc = n // chunk
    cross = has_s0 or nc > 1
    unroll = 2 if nc % 2 == 0 else 1
    h = pl.program_id(0)
    cdec_f = cdec_ref[2 * h]
    cdec_b = cdec_ref[2 * h + 1]
    tn_dims = (((0,), (0,)), ((), ()))
    nt_dims = (((1,), (1,)), ((), ()))
    fwd = pl.ds(0, RET_DK)
    bwd = pl.ds(RET_DK, RET_DK)

    def rows(c):
        return pl.ds(pl.multiple_of(c * chunk, chunk), chunk)

    def prep(c, carry):
        r = rows(c)
        q = q_ref[r, :].astype(F32) * (RET_DK ** -0.5)
        k = k_ref[r, :].astype(F32)
        if rotary:
            cs, sn = cos_ref[r, :], sin_ref[r, :]
            q = q * cs + pltpu.roll(q, RET_DK // 2, 1) * sn
            k = k * cs + pltpu.roll(k, RET_DK // 2, 1) * sn
        qs_ref[r, :] = q.astype(BF16)
        kb = k.astype(BF16)
        ks_ref[r, :] = kb
        kf = kb.astype(F32)
        kd = jnp.concatenate([(kf * dvec_ref[0, 1]).astype(BF16),
                              (kf * dvec_ref[0, 3]).astype(BF16)], axis=1)
        kv_ref[c] = lax.dot_general(kd, v_ref[r, :], tn_dims, preferred_element_type=F32)
        return carry

    def scan(i, carry):
        cb = nc - 1 - i
        st_ref[i, fwd, :] = s_ref[fwd, :].astype(BF16)
        s_ref[fwd, :] = s_ref[fwd, :] * cdec_f + kv_ref[i, fwd, :]
        st_ref[cb, bwd, :] = s_ref[bwd, :].astype(BF16)
        s_ref[bwd, :] = s_ref[bwd, :] * cdec_b + kv_ref[cb, bwd, :]
        return carry

    def out_step(c, carry):
        r = rows(c)
        qc = qs_ref[r, :]
        sc = lax.dot_general(qc, ks_ref[r, :], nt_dims, preferred_element_type=F32)
        o = jnp.dot((sc * dmask_ref[0]).astype(BF16), v_ref[r, :], preferred_element_type=F32)
        if cross:
            qf = qc.astype(F32)
            ql = jnp.concatenate([(qf * dvec_ref[0, 0]).astype(BF16),
                                  (qf * dvec_ref[0, 2]).astype(BF16)], axis=1)
            o = o + jnp.dot(ql, st_ref[c], preferred_element_type=F32)
        mu = jnp.mean(o, axis=-1, keepdims=True)
        oc = o - mu
        var = jnp.mean(oc * oc, axis=-1, keepdims=True)
        gt = g_ref[r, :].astype(F32)
        o_ref[r, :] = (gt * jax.nn.sigmoid(gt) * (oc * lax.rsqrt(var + GN_EPS))).astype(BF16)
        return carry

    def loop(body):
        if nc == 1:
            body(0, 0)
        else:
            lax.fori_loop(0, nc, body, 0, unroll=unroll)

    if has_s0:
        s_ref[fwd, :] = s0_ref[0, 0, 0, 0]
        s_ref[bwd, :] = s0_ref[0, 0, 1, 0]
    else:
        s_ref[...] = jnp.zeros_like(s_ref)
    loop(prep)
    if nc == 1:
        scan(0, 0)
    else:
        lax.fori_loop(0, nc, scan, 0)
    if emit_state:
        sn_ref[0, 0, 0] = s_ref[fwd, :]
        sn_ref[0, 1, 0] = s_ref[bwd, :]
    loop(out_step)


def _decay_tables(log_g, chunk):
    i = jnp.arange(chunk, dtype=F32)
    diff = i[:, None] - i[None, :]
    lf = log_g[0][:, None, None]
    lb = log_g[1][:, None, None]
    dmask = (jnp.where(diff >= 0, jnp.exp(jnp.maximum(diff, 0.0) * lf), 0.0)
             + jnp.where(diff <= 0, jnp.exp(jnp.maximum(-diff, 0.0) * lb), 0.0))
    lf2, lb2 = log_g[0][:, None], log_g[1][:, None]
    vecs = jnp.stack([jnp.exp((i + 1.0)[None, :] * lf2),
                      jnp.exp((chunk - 1.0 - i)[None, :] * lf2),
                      jnp.exp((chunk - i)[None, :] * lb2),
                      jnp.exp(i[None, :] * lb2)], axis=1)
    dvec = jnp.broadcast_to(vecs[..., None], vecs.shape + (LANES,))
    cdec = jnp.exp(chunk * log_g).T.reshape(-1)
    return dmask, dvec, cdec


def _retention(u, og_prev, *, n, nseq, row0, tables, rot, s0, layer, emit_state):
    t = u.shape[0]
    chunk = min(RET_CHUNK, n)
    assert n % chunk == 0 and row0 % n == 0
    rb = row0 // n
    dmask, dvec, cdec = tables
    nc = n // chunk
    qb, vb = COL_K // RET_DK, COL_V // RET_DV
    gb = COL_GRET // RET_DV
    in_specs = [pl.BlockSpec((n, RET_DK), lambda h, b, cd: (rb + b, h)),
                pl.BlockSpec((n, RET_DK), lambda h, b, cd: (rb + b, qb + h)),
                pl.BlockSpec((n, RET_DV), lambda h, b, cd: (rb + b, vb + h)),
                pl.BlockSpec((n, RET_DV), lambda h, b, cd: (rb + b, gb + h)),
                pl.BlockSpec((1, chunk, chunk), lambda h, b, cd: (h, 0, 0)),
                pl.BlockSpec((1, 4, chunk, LANES), lambda h, b, cd: (h, 0, 0, 0))]
    args = [u, u, u, u, dmask, dvec]
    if rot is not None:
        in_specs += [pl.BlockSpec((n, RET_DK), lambda h, b, cd: (0, 0))] * 2
        args += list(rot)
    if s0 is not None:
        in_specs.append(pl.BlockSpec((1, 1, 2, 1, RET_DK, RET_DV),
                                     lambda h, b, cd: (b, layer, 0, h, 0, 0)))
        args.append(s0)
    out_shape = [jax.ShapeDtypeStruct((t, RET_V_W), BF16)]
    out_specs = [pl.BlockSpec((n, RET_DV), lambda h, b, cd: (rb + b, h))]
    if emit_state:
        out_shape.append(jax.ShapeDtypeStruct((nseq, 2, N_RET_HEADS, RET_DK, RET_DV), F32))
        out_specs.append(pl.BlockSpec((1, 2, 1, RET_DK, RET_DV), lambda h, b, cd: (b, 0, h, 0, 0)))
    kern = functools.partial(_retention_kernel, n=n, chunk=chunk, rotary=rot is not None,
                             has_s0=s0 is not None, emit_state=emit_state)
    aliases = {}
    if og_prev is not None:
        in_specs.append(pl.BlockSpec(memory_space=pl.ANY))
        args.append(og_prev)
        aliases = {len(args): 0}
        kern = _drop_ref(kern, len(args))
    grid_spec = pltpu.PrefetchScalarGridSpec(
        num_scalar_prefetch=1, grid=(N_RET_HEADS, nseq), in_specs=in_specs, out_specs=out_specs,
        scratch_shapes=[pltpu.VMEM((n, RET_DK), BF16), pltpu.VMEM((n, RET_DK), BF16),
                        pltpu.VMEM((nc, 2 * RET_DK, RET_DV), F32),
                        pltpu.VMEM((nc, 2 * RET_DK, RET_DV), BF16),
                        pltpu.VMEM((2 * RET_DK, RET_DV), F32)])
    outs = pl.pallas_call(
        kern, out_shape=out_shape, grid_spec=grid_spec, input_output_aliases=aliases,
        compiler_params=_cparams(("parallel", "parallel")),
        name="retention_lat" if rot is not None else "retention_ctx",
    )(cdec, *args)
    return outs if emit_state else outs[0]


def _rotary_tables(n):
    rows = n // GRID_W
    row = jnp.repeat(jnp.arange(rows), GRID_W).astype(F32)
    col = jnp.tile(jnp.arange(GRID_W), rows).astype(F32)
    n_freq = RET_DK // 4
    inv_freq = ROPE_BASE ** (-jnp.arange(n_freq, dtype=F32) / n_freq)
    ang = jnp.concatenate([row[:, None] * inv_freq, col[:, None] * inv_freq], axis=-1)
    cos, sin = jnp.cos(ang), jnp.sin(ang)
    return jnp.concatenate([cos, cos], axis=-1), jnp.concatenate([-sin, sin], axis=-1)


def _chan_dft_kernel(u_ref, cs_ref, zc_ref, zs_ref):
    gw = FOURIER_GROUP_W
    for g in range(N_FOURIER_GROUPS):
        r = jnp.dot(u_ref[:, g * gw:(g + 1) * gw], cs_ref[...], preferred_element_type=F32)
        zc_ref[:, g * gw:(g + 1) * gw] = r[:, :gw].astype(BF16)
        zs_ref[:, g * gw:(g + 1) * gw] = r[:, gw:].astype(BF16)


def _chan_dft(u, cs, tm):
    t = u.shape[0]
    fb = COL_FOUR // FOURIER_W
    out = jax.ShapeDtypeStruct((t, FOURIER_W), BF16)
    return pl.pallas_call(
        _chan_dft_kernel,
        out_shape=[out, out],
        grid=(t // tm,),
        in_specs=[pl.BlockSpec((tm, FOURIER_W), lambda i: (i, fb)),
                  pl.BlockSpec((FOURIER_GROUP_W, 2 * FOURIER_GROUP_W), lambda i: (0, 0))],
        out_specs=[pl.BlockSpec((tm, FOURIER_W), lambda i: (i, 0))] * 2,
        compiler_params=_cparams(("parallel",)),
        name="chan_dft",
    )(u, cs)


def _pos_dft_kernel(dc_ref, ds_ref, zc_ref, zs_ref, o_ref):
    o = jnp.dot(dc_ref[...], zc_ref[...], preferred_element_type=F32)
    o = o + jnp.dot(ds_ref[...], zs_ref[...], preferred_element_type=F32)
    o_ref[...] = o.astype(BF16)


def _pos_dft(zc, zs, uf_prev, *, n, nseq, row0, dmats, tr, tc):
    t = zc.shape[0]
    dc, ds = dmats
    assert n % tr == 0 and FOURIER_W % tc == 0 and row0 % n == 0
    rb = row0 // n
    nr = n // tr
    in_specs = [pl.BlockSpec((tr, n), lambda b, j, i: (i, 0)),
                pl.BlockSpec((tr, n), lambda b, j, i: (i, 0)),
                pl.BlockSpec((n, tc), lambda b, j, i: (rb + b, j)),
                pl.BlockSpec((n, tc), lambda b, j, i: (rb + b, j))]
    args = [dc, ds, zc, zs]
    aliases = {}
    kern = _pos_dft_kernel
    if uf_prev is not None:
        in_specs.append(pl.BlockSpec(memory_space=pl.ANY))
        args.append(uf_prev)
        aliases = {4: 0}
        kern = _drop_ref(kern, 4)
    return pl.pallas_call(
        kern,
        out_shape=jax.ShapeDtypeStruct((t, FOURIER_W), BF16),
        grid=(nseq, FOURIER_W // tc, nr),
        in_specs=in_specs,
        out_specs=pl.BlockSpec((tr, tc), lambda b, j, i: ((rb + b) * nr + i, j)),
        input_output_aliases=aliases,
        compiler_params=_cparams(("parallel", "parallel", "parallel")),
        name="pos_dft",
    )(*args)


def _dft_mats(n, scale):
    j = jnp.arange(n, dtype=jnp.int32)
    m = (j[:, None] * j[None, :]) % n
    ang = m.astype(F32) * (2.0 * jnp.pi / n)
    return (jnp.cos(ang) * scale).astype(BF16), (-jnp.sin(ang) * scale).astype(BF16)


def _merge_kernel(grp_ref, *refs, n_x, nct):
    del grp_ref
    x_refs = refs[:n_x]
    (og_ref, uf_ref, ga_ref, gb_ref, mod_ref, g2_ref, wro_ref, wf_ref, wo_ref, wr_ref, br_ref,
     xo_ref, h2_ref, ti_ref, tw_ref, rk_ref, cnt_ref, run_ref) = refs[n_x:]
    tm = og_ref.shape[0]

    @pl.when(pl.program_id(0) == 0)
    def _():
        run_ref[...] = jnp.zeros_like(run_ref)

    a = jnp.dot(og_ref[...], wro_ref[...], preferred_element_type=F32)
    b = jnp.dot(uf_ref[...], wf_ref[...], preferred_element_type=F32)
    merged = (jax.nn.sigmoid(ga_ref[...].astype(F32)) * a
              + jax.nn.sigmoid(gb_ref[...].astype(F32)) * b)
    y = jnp.dot(merged.astype(BF16), wo_ref[...], preferred_element_type=F32)
    m = mod_ref[0]
    x = _stream_tile(x_refs, pl.program_id(0), nct) + m[2:3] * y
    xo_ref[...] = x
    h2 = _rms_mod(x, g2_ref[...], m[3:4], m[4:5])
    h2_ref[...] = h2
    h_hi = h2.astype(BF16)
    h_lo = (h2 - h_hi.astype(F32)).astype(BF16)
    p_hi = jnp.dot(h_hi, wr_ref[...], preferred_element_type=F32)
    p_lo = jnp.dot(h_lo, wr_ref[...], preferred_element_type=F32)
    logits = p_hi + pltpu.roll(p_hi, E_PAD - N_EXPERTS, 1) + p_lo + br_ref[...]

    lane = lax.broadcasted_iota(jnp.int32, (tm, E_PAD), 1)
    rest = logits
    vals, idxs, sels = [], [], []
    for _ in range(TOP_K):
        mx = jnp.max(rest, axis=-1, keepdims=True)
        idx = jnp.min(jnp.where(rest == mx, lane, E_PAD), axis=-1, keepdims=True)
        sel = lane == idx
        vals.append(mx)
        idxs.append(idx)
        sels.append(sel)
        rest = jnp.where(sel, -jnp.inf, rest)
    exps = [jnp.exp(v - vals[0]) for v in vals]
    den = exps[0] + exps[1] + exps[2] + exps[3]

    picked = jnp.zeros((tm, E_PAD), F32)
    for sel in sels:
        picked = picked + sel.astype(F32)
    ri = lax.broadcasted_iota(jnp.int32, (tm, tm), 0)
    ci = lax.broadcasted_iota(jnp.int32, (tm, tm), 1)
    lower = (ri > ci).astype(BF16)
    before = jnp.dot(lower, picked.astype(BF16), preferred_element_type=F32) + run_ref[...]

    ti = jnp.zeros((tm, E_PAD), jnp.int32)
    tw = jnp.zeros((tm, E_PAD), F32)
    rk = jnp.zeros((tm, E_PAD), jnp.int32)
    for k in range(TOP_K):
        rank = jnp.sum(jnp.where(sels[k], before, 0.0), axis=-1, keepdims=True).astype(jnp.int32)
        ti = jnp.where(lane == k, idxs[k], ti)
        tw = jnp.where(lane == k, exps[k] / den, tw)
        rk = jnp.where(lane == k, rank, rk)
    ti_ref[...] = ti
    tw_ref[...] = tw
    rk_ref[...] = rk
    run_ref[...] = run_ref[...] + jnp.sum(picked, axis=0, keepdims=True)
    cnt_ref[...] = run_ref[...]


def _merge(xs, og, uf, u, tile_grp, mod, g2, wro, wf, wo, wr, br, tm):
    t, d = og.shape[0], wo.shape[1]
    ga_b, gb_b = COL_GA // d, COL_GA // d + 1
    full = lambda shape: pl.BlockSpec(shape, lambda i, grp: (0,) * len(shape))
    row = lambda w: pl.BlockSpec((tm, w), lambda i, grp: (i, 0))
    grid_spec = pltpu.PrefetchScalarGridSpec(
        num_scalar_prefetch=1,
        grid=(t // tm,),
        in_specs=_stream_specs(xs, tm, 1) + [
            row(RET_V_W), row(FOURIER_W),
            pl.BlockSpec((tm, d), lambda i, grp: (i, ga_b)),
            pl.BlockSpec((tm, d), lambda i, grp: (i, gb_b)),
            pl.BlockSpec((1, MOD_ROWS, d), lambda i, grp: (grp[i], 0, 0)),
            full((1, d)), full(wro.shape), full(wf.shape), full(wo.shape),
            full(wr.shape), full(br.shape)],
        out_specs=[row(d), row(d), row(E_PAD), row(E_PAD), row(E_PAD), full((1, E_PAD))],
        scratch_shapes=[pltpu.VMEM((1, E_PAD), F32)],
    )
    return pl.pallas_call(
        functools.partial(_merge_kernel, n_x=len(xs), nct=xs[0].shape[0] // tm),
        out_shape=[jax.ShapeDtypeStruct((t, d), F32), jax.ShapeDtypeStruct((t, d), F32),
                   jax.ShapeDtypeStruct((t, E_PAD), jnp.int32), jax.ShapeDtypeStruct((t, E_PAD), F32),
                   jax.ShapeDtypeStruct((t, E_PAD), jnp.int32), jax.ShapeDtypeStruct((1, E_PAD), F32)],
        grid_spec=grid_spec,
        compiler_params=_cparams(("arbitrary",)),
        name="merge_route",
    )(tile_grp, *xs, og, uf, u, u, mod, g2.reshape(1, d), wro, wf, wo, wr, br)


def _router_operands(w_router, b_router):
    hi = w_router.astype(BF16)
    lo = (w_router - hi.astype(F32)).astype(BF16)
    wr = jnp.concatenate([hi, lo], axis=-1)
    wr = jnp.pad(wr, ((0, 0), (0, 0), (0, E_PAD - 2 * N_EXPERTS)))
    br = jnp.pad(b_router, ((0, 0), (0, E_PAD - N_EXPERTS)), constant_values=-1e30)
    return wr, br[:, None, :]


def _dispatch_kernel(pp_ref, h_ref, xz_ref, xp_ref, sem):
    del xz_ref
    tm = h_ref.shape[0]

    def issue(t, carry):
        for k in range(TOP_K):
            p = pp_ref[t * TOP_K + k]
            pltpu.make_async_copy(h_ref.at[pl.ds(t, 1)], xp_ref.at[pl.ds(p, 1)], sem).start()
        return carry

    lax.fori_loop(0, tm, issue, 0)
    for _ in range(TOP_K):
        pltpu.make_async_copy(h_ref, xp_ref.at[pl.ds(0, tm)], sem).wait()


def _dispatch(h2, ppos, p_total, tm):
    t, d = h2.shape
    xz = jnp.zeros((p_total, d), F32)
    return pl.pallas_call(
        _dispatch_kernel,
        out_shape=jax.ShapeDtypeStruct((p_total, d), F32),
        grid=(t // tm,),
        in_specs=[pl.BlockSpec((tm * TOP_K,), lambda i: (i,), memory_space=pltpu.SMEM),
                  pl.BlockSpec((tm, d), lambda i: (i, 0)),
                  pl.BlockSpec(memory_space=pl.ANY)],
        out_specs=pl.BlockSpec(memory_space=pl.ANY),
        scratch_shapes=[pltpu.SemaphoreType.DMA],
        input_output_aliases={2: 0},
        compiler_params=_cparams(("arbitrary",)),
        name="moe_dispatch",
    )(ppos, h2, xz)


def _expert_kernel(be_ref, nu_ref, x_ref, w1_ref, b1_ref, w2_ref, b2_ref, y_ref, w1b_ref, w2b_ref):
    i = pl.program_id(0)
    de = w2_ref.shape[1]
    used = i < nu_ref[0]

    @pl.when(used & ((i == 0) | (be_ref[i] != be_ref[jnp.maximum(i - 1, 0)])))
    def _():
        w1b_ref[...] = w1_ref[0].astype(BF16)
        w2b_ref[...] = w2_ref[0].astype(BF16)

    @pl.when(used)
    def _():
        hh = jnp.dot(x_ref[...].astype(BF16), w1b_ref[...], preferred_element_type=F32) + b1_ref[0]
        gate = jnp.minimum(hh[:, :de], SWIGLU_LIMIT)
        lin = jnp.clip(hh[:, de:], -SWIGLU_LIMIT, SWIGLU_LIMIT)
        act = (lin + 1.0) * (gate * jax.nn.sigmoid(SWIGLU_ALPHA * gate))
        y_ref[...] = jnp.dot(act.astype(BF16), w2b_ref[...], preferred_element_type=F32) + b2_ref[0]

    @pl.when(jnp.logical_not(used))
    def _():
        y_ref[...] = jnp.zeros_like(y_ref)


def _experts(xp, blk_e, n_used, w1, b1, w2, b2):
    p_total, d = xp.shape
    ne, _, de2 = w1.shape
    de = w2.shape[1]
    bm = MOE_BLOCK
    grid_spec = pltpu.PrefetchScalarGridSpec(
        num_scalar_prefetch=2,
        grid=(p_total // bm,),
        in_specs=[pl.BlockSpec((bm, d), lambda i, be, nu: (i, 0)),
                  pl.BlockSpec((1, d, de2), lambda i, be, nu: (be[i], 0, 0)),
                  pl.BlockSpec((1, 1, de2), lambda i, be, nu: (be[i], 0, 0)),
                  pl.BlockSpec((1, de, d), lambda i, be, nu: (be[i], 0, 0)),
                  pl.BlockSpec((1, 1, d), lambda i, be, nu: (be[i], 0, 0))],
        out_specs=pl.BlockSpec((bm, d), lambda i, be, nu: (i, 0)),
        scratch_shapes=[pltpu.VMEM((d, de2), BF16), pltpu.VMEM((de, d), BF16)],
    )
    return pl.pallas_call(
        _expert_kernel,
        out_shape=jax.ShapeDtypeStruct((p_total, d), F32),
        grid_spec=grid_spec,
        compiler_params=_cparams(("arbitrary",)),
        name="moe_experts",
    )(blk_e, n_used, xp, w1, b1.reshape(ne, 1, de2), w2, b2.reshape(ne, 1, d))


def _combine_kernel(grp_ref, pp_ref, x_ref, tw_ref, mod_ref, fg_ref, yb_ref, *refs, split_out, nct):
    del grp_ref
    o_refs = refs[:2] if split_out else refs[:1]
    buf_ref, sem = refs[len(o_refs):]
    tm = x_ref.shape[0]
    i = pl.program_id(0)

    def issue(t, carry):
        for k in range(TOP_K):
            p = pp_ref[t * TOP_K + k]
            pltpu.make_async_copy(yb_ref.at[pl.ds(p, 1)], buf_ref.at[k, pl.ds(t, 1)], sem).start()
        return carry

    lax.fori_loop(0, tm, issue, 0)
    for k in range(TOP_K):
        pltpu.make_async_copy(yb_ref.at[pl.ds(0, tm)], buf_ref.at[k], sem).wait()
    tw = tw_ref[...]
    acc = tw[:, 0:1] * buf_ref[0]
    for k in range(1, TOP_K):
        acc = acc + tw[:, k:k + 1] * buf_ref[k]
    x = x_ref[...] + mod_ref[0][5:6] * acc
    if not split_out:
        o_refs[0][...] = x
    else:
        y = x * lax.rsqrt(jnp.mean(x * x, axis=-1, keepdims=True) + NORM_EPS) * fg_ref[...]

        @pl.when(i < nct)
        def _():
            o_refs[0][...] = y

        @pl.when(i >= nct)
        def _():
            o_refs[1][...] = y


def _combine(x, yb, ppos, tw, tile_grp, mod, final_g, tm, t_ctx, split_out):
    t, d = x.shape
    nct = t_ctx // tm
    if split_out:
        out_shape = [jax.ShapeDtypeStruct((t_ctx, d), F32), jax.ShapeDtypeStruct((t - t_ctx, d), F32)]
        out_specs = [pl.BlockSpec((tm, d), lambda i, grp: (jnp.minimum(i, nct - 1), 0)),
                     pl.BlockSpec((tm, d), lambda i, grp: (jnp.maximum(i - nct, 0), 0))]
    else:
        out_shape = [jax.ShapeDtypeStruct((t, d), F32)]
        out_specs = [pl.BlockSpec((tm, d), lambda i, grp: (i, 0))]
    grid_spec = pltpu.PrefetchScalarGridSpec(
        num_scalar_prefetch=1,
        grid=(t // tm,),
        in_specs=[pl.BlockSpec((tm * TOP_K,), lambda i, grp: (i,), memory_space=pltpu.SMEM),
                  pl.BlockSpec((tm, d), lambda i, grp: (i, 0)),
                  pl.BlockSpec((tm, E_PAD), lambda i, grp: (i, 0)),
                  pl.BlockSpec((1, MOD_ROWS, d), lambda i, grp: (grp[i], 0, 0)),
                  pl.BlockSpec((1, d), lambda i, grp: (0, 0)),
                  pl.BlockSpec(memory_space=pl.ANY)],
        out_specs=out_specs,
        scratch_shapes=[pltpu.VMEM((TOP_K, tm, d), F32), pltpu.SemaphoreType.DMA],
    )
    return pl.pallas_call(
        functools.partial(_combine_kernel, split_out=split_out, nct=nct),
        out_shape=out_shape,
        grid_spec=grid_spec,
        compiler_params=_cparams(("arbitrary",)),
        name="moe_combine",
    )(tile_grp, ppos, x, tw, mod, final_g.reshape(1, d), yb)


def _route_layout(ti, rk, cnt, t):
    bm = MOE_BLOCK
    counts = cnt[0, :N_EXPERTS].astype(jnp.int32)
    padded = ((counts + bm - 1) // bm) * bm
    p_end = jnp.cumsum(padded)
    p_start = p_end - padded
    ti4 = ti[:, :TOP_K]
    onehot = ti4[:, :, None] == jnp.arange(N_EXPERTS, dtype=jnp.int32)
    ppos = jnp.sum(jnp.where(onehot, p_start, 0), axis=-1) + rk[:, :TOP_K]
    assert (t * TOP_K) % bm == 0
    n_blk = (t * TOP_K) // bm + N_EXPERTS
    blk_start = jnp.arange(n_blk, dtype=jnp.int32) * bm
    blk_e = jnp.minimum(jnp.sum(p_end[None, :] <= blk_start[:, None], axis=1), N_EXPERTS - 1)
    n_used = (p_end[-1:] // bm).astype(jnp.int32)
    return ppos.reshape(-1).astype(jnp.int32), blk_e.astype(jnp.int32), n_used, n_blk * bm


def _tile_groups(t, t_ctx, dec_seq, tm):
    starts = jnp.arange(t // tm, dtype=jnp.int32) * tm
    return jnp.where(starts < t_ctx, 0, 1 + (starts - t_ctx) // dec_seq).astype(jnp.int32)


def kernel(x_prompt, x_sample, state_ret, c, c_ctx, w_ada, b_ada, norm1_g, norm2_g, w_in, ret_decay_logit, w_ret_o, w_four, w_out, w_router, b_router, w1, b1, w2, b2, final_g):
    batch, seq, d = x_prompt.shape
    dec_batch, dec_seq, _ = x_sample.shape
    depth = w_in.shape[0]
    t_ctx, t_lat = batch * seq, dec_batch * dec_seq
    t = t_ctx + t_lat
    tm_in = min(1024, t_ctx, dec_seq)
    tm = min(512, t_ctx, dec_seq)
    assert t_ctx % tm_in == 0 and dec_seq % tm_in == 0 and t_ctx % dec_seq == 0

    n_cond = 16
    cond = jnp.zeros((n_cond, d), F32).at[0].set(c_ctx).at[1:1 + dec_batch].set(c)
    mod = _adaln_mod(cond, w_ada, b_ada).reshape(depth, n_cond, 6, d)
    mod = jnp.pad(mod, ((0, 0), (0, 0), (0, MOD_ROWS - 6), (0, 0)))
    grp_in = _tile_groups(t, t_ctx, dec_seq, tm_in)
    grp = _tile_groups(t, t_ctx, dec_seq, tm)

    rot = _rotary_tables(dec_seq)
    gw = FOURIER_GROUP_W
    jc = jnp.arange(gw, dtype=jnp.int32)
    ang_c = ((jc[:, None] * jc[None, :]) % gw).astype(F32) * (2.0 * jnp.pi / gw)
    cs = jnp.concatenate([jnp.cos(ang_c), jnp.sin(ang_c)], axis=1).astype(BF16)
    dm_ctx = _dft_mats(seq, (seq * gw) ** -0.5)
    dm_lat = _dft_mats(dec_seq, (dec_seq * gw) ** -0.5)
    log_g = jax.nn.log_sigmoid(ret_decay_logit.astype(F32))
    wr, br = _router_operands(w_router, b_router)

    xs = [x_prompt.reshape(t_ctx, d), x_sample.reshape(t_lat, d)]
    states = []
    for l in range(depth):
        last = l == depth - 1
        u = _in_proj(xs, grp_in, mod[l], norm1_g[l], w_in[l].astype(BF16), tm_in, 1536)
        tables = _decay_tables(log_g[l], min(RET_CHUNK, seq, dec_seq))
        og, st = _retention(u, None, n=seq, nseq=batch, row0=0, tables=tables, rot=None, s0=None,
                            layer=l, emit_state=True)
        og = _retention(u, og, n=dec_seq, nseq=dec_batch, row0=t_ctx, tables=tables, rot=rot,
                        s0=state_ret, layer=l, emit_state=False)
        states.append(st)
        zc, zs = _chan_dft(u, cs, tm)
        uf = _pos_dft(zc, zs, None, n=seq, nseq=batch, row0=0, dmats=dm_ctx, tr=seq, tc=FOURIER_W)
        uf = _pos_dft(zc, zs, uf, n=dec_seq, nseq=dec_batch, row0=t_ctx, dmats=dm_lat,
                      tr=min(512, dec_seq), tc=512)
        x, h2, ti, tw, rk, cnt = _merge(
            xs, og, uf, u, grp, mod[l], norm2_g[l], w_ret_o[l].astype(BF16), w_four[l].astype(BF16),
            w_out[l].astype(BF16), wr[l], br[l], tm)
        ppos, blk_e, n_used, p_total = _route_layout(ti, rk, cnt, t)
        xp = _dispatch(h2, ppos, p_total, tm)
        yb = _experts(xp, blk_e, n_used, w1[l], b1[l], w2[l], b2[l])
        xs = _combine(x, yb, ppos, tw, grp, mod[l], final_g, tm, t_ctx, split_out=last)

    new_state = jnp.stack(states, axis=1).astype(x_prompt.dtype)
    return (xs[0].reshape(batch, seq, d), xs[1].reshape(dec_batch, dec_seq, d), new_state)
```

```python
import functools

import jax
import jax.numpy as jnp
from jax import lax
from jax.experimental import pallas as pl
from jax.experimental.pallas import tpu as pltpu

F32 = jnp.float32
BF16 = jnp.bfloat16

N_RET_HEADS = 8
RET_DK = 128
RET_DV = 256
RET_QK_W = N_RET_HEADS * RET_DK
RET_V_W = N_RET_HEADS * RET_DV
N_FOURIER_GROUPS = 4
FOURIER_GROUP_W = 256
FOURIER_W = N_FOURIER_GROUPS * FOURIER_GROUP_W
GRID_W = 64
N_EXPERTS = 32
TOP_K = 4
SWIGLU_LIMIT = 7.0
SWIGLU_ALPHA = 1.702
ROPE_BASE = 10000.0
NORM_EPS = 1e-6
GN_EPS = 1e-5

COL_Q = 0
COL_K = COL_Q + RET_QK_W
COL_V = COL_K + RET_QK_W
COL_GRET = COL_V + RET_V_W
COL_FOUR = COL_GRET + RET_V_W
COL_GA = COL_FOUR + FOURIER_W

LANES = 128
VMEM_LIMIT_BYTES = 56 * 1024 * 1024

MOD_ROWS = 8
RET_CHUNK = 256
MOE_BLOCK = 512
MOE_TILE = 256
SEG_ALIGN = 8
SEG_ROWS = MOE_TILE * TOP_K + N_EXPERTS * SEG_ALIGN
E_PAD = LANES


def _cparams(sem, vmem=VMEM_LIMIT_BYTES):
    return pltpu.CompilerParams(dimension_semantics=sem, vmem_limit_bytes=vmem)


def _drop_ref(kern, idx):
    def wrapped(*refs):
        return kern(*refs[:idx], *refs[idx + 1:])
    return wrapped


def _mod_kernel(c_ref, w_ref, b_ref, o_ref):
    c = c_ref[...]
    s = c * jax.nn.sigmoid(c)
    o_ref[0] = jnp.dot(s, w_ref[0], precision=lax.Precision.HIGHEST,
                       preferred_element_type=F32) + b_ref[0]


def _adaln_mod(cond, w_ada, b_ada):
    depth, d, w6 = w_ada.shape
    r = cond.shape[0]
    tn = 1536
    assert w6 % tn == 0
    return pl.pallas_call(
        _mod_kernel,
        out_shape=jax.ShapeDtypeStruct((depth, r, w6), F32),
        grid=(depth, w6 // tn),
        in_specs=[pl.BlockSpec((r, d), lambda l, j: (0, 0)),
                  pl.BlockSpec((1, d, tn), lambda l, j: (l, 0, j)),
                  pl.BlockSpec((1, 1, tn), lambda l, j: (l, 0, j))],
        out_specs=pl.BlockSpec((1, r, tn), lambda l, j: (l, 0, j)),
        compiler_params=_cparams(("parallel", "parallel")),
        name="adaln_mod",
    )(cond, w_ada, b_ada.reshape(depth, 1, w6))


def _stream_specs(xs, tm, imap_args):
    d = xs[0].shape[1]
    if len(xs) == 1:
        return [pl.BlockSpec((tm, d), lambda i, *a: (i, 0))]
    nct = xs[0].shape[0] // tm
    return [pl.BlockSpec((tm, d), lambda i, *a: (jnp.minimum(i, nct - 1), 0)),
            pl.BlockSpec((tm, d), lambda i, *a: (jnp.maximum(i - nct, 0), 0))]


def _stream_tile(x_refs, i, nct):
    if len(x_refs) == 1:
        return x_refs[0][...]
    return jnp.where(i < nct, x_refs[0][...], x_refs[1][...])


def _rms_mod(x, g, shift, scale):
    y = x * lax.rsqrt(jnp.mean(x * x, axis=-1, keepdims=True) + NORM_EPS) * g
    return y * (1.0 + scale) + shift


def _in_proj_kernel(grp_ref, *refs, n_x, nct):
    del grp_ref
    x_refs = refs[:n_x]
    mod_ref, g_ref, w_ref, o_ref, h_ref = refs[n_x:]

    @pl.when(pl.program_id(1) == 0)
    def _():
        m = mod_ref[0]
        x = _stream_tile(x_refs, pl.program_id(0), nct)
        h_ref[...] = _rms_mod(x, g_ref[...], m[0:1], m[1:2]).astype(BF16)

    o_ref[...] = jnp.dot(h_ref[...], w_ref[...], preferred_element_type=F32).astype(BF16)


def _in_proj(xs, tile_grp, mod, g, w, tm, tn):
    t = sum(x.shape[0] for x in xs)
    d, n = w.shape
    assert t % tm == 0 and n % tn == 0 and all(x.shape[0] % tm == 0 for x in xs)
    x_specs = [pl.BlockSpec(s.block_shape, (lambda f: lambda i, j, grp: f(i))(s.index_map))
               for s in _stream_specs(xs, tm, 1)]
    grid_spec = pltpu.PrefetchScalarGridSpec(
        num_scalar_prefetch=1,
        grid=(t // tm, n // tn),
        in_specs=x_specs + [pl.BlockSpec((1, MOD_ROWS, d), lambda i, j, grp: (grp[i], 0, 0)),
                            pl.BlockSpec((1, d), lambda i, j, grp: (0, 0)),
                            pl.BlockSpec((d, tn), lambda i, j, grp: (0, j))],
        out_specs=pl.BlockSpec((tm, tn), lambda i, j, grp: (i, j)),
        scratch_shapes=[pltpu.VMEM((tm, d), BF16)],
    )
    return pl.pallas_call(
        functools.partial(_in_proj_kernel, n_x=len(xs), nct=xs[0].shape[0] // tm),
        out_shape=jax.ShapeDtypeStruct((t, n), BF16),
        grid_spec=grid_spec,
        compiler_params=_cparams(("parallel", "arbitrary")),
        name="in_proj",
    )(tile_grp, *xs, mod, g.reshape(1, d), w)


def _retention_kernel(cdec_ref, *refs, n, chunk, rotary, has_s0, emit_state):
    refs = list(refs)
    q_ref, k_ref, v_ref, g_ref, dmask_ref, dvec_ref = refs[:6]
    pos = 6
    if rotary:
        cos_ref, sin_ref = refs[pos:pos + 2]
        pos += 2
    if has_s0:
        s0_ref = refs[pos]
        pos += 1
    o_ref = refs[pos]
    pos += 1
    if emit_state:
        sn_ref = refs[pos]
        pos += 1
    qs_ref, ks_ref, kv_ref, st_ref, s_ref = refs[pos:pos + 5]

    nc = n // chunk
    cross = has_s0 or nc > 1
    unroll = 2 if nc % 2 == 0 else 1
    h = pl.program_id(0)
    cdec_f = cdec_ref[2 * h]
    cdec_b = cdec_ref[2 * h + 1]
    tn_dims = (((0,), (0,)), ((), ()))
    nt_dims = (((1,), (1,)), ((), ()))
    fwd = pl.ds(0, RET_DK)
    bwd = pl.ds(RET_DK, RET_DK)

    def rows(c):
        return pl.ds(pl.multiple_of(c * chunk, chunk), chunk)

    def prep(c, carry):
        r = rows(c)
        q = q_ref[r, :].astype(F32) * (RET_DK ** -0.5)
        k = k_ref[r, :].astype(F32)
        if rotary:
            cs, sn = cos_ref[r, :], sin_ref[r, :]
            q = q * cs + pltpu.roll(q, RET_DK // 2, 1) * sn
            k = k * cs + pltpu.roll(k, RET_DK // 2, 1) * sn
        qs_ref[r, :] = q.astype(BF16)
        kb = k.astype(BF16)
        ks_ref[r, :] = kb
        kf = kb.astype(F32)
        kd = jnp.concatenate([(kf * dvec_ref[0, 1]).astype(BF16),
                              (kf * dvec_ref[0, 3]).astype(BF16)], axis=1)
        kv_ref[c] = lax.dot_general(kd, v_ref[r, :], tn_dims, preferred_element_type=F32)
        return carry

    def scan(i, carry):
        cb = nc - 1 - i
        st_ref[i, fwd, :] = s_ref[fwd, :].astype(BF16)
        s_ref[fwd, :] = s_ref[fwd, :] * cdec_f + kv_ref[i, fwd, :]
        st_ref[cb, bwd, :] = s_ref[bwd, :].astype(BF16)
        s_ref[bwd, :] = s_ref[bwd, :] * cdec_b + kv_ref[cb, bwd, :]
        return carry

    def out_step(c, carry):
        r = rows(c)
        qc = qs_ref[r, :]
        sc = lax.dot_general(qc, ks_ref[r, :], nt_dims, preferred_element_type=F32)
        o = jnp.dot((sc * dmask_ref[0]).astype(BF16), v_ref[r, :], preferred_element_type=F32)
        if cross:
            qf = qc.astype(F32)
            ql = jnp.concatenate([(qf * dvec_ref[0, 0]).astype(BF16),
                                  (qf * dvec_ref[0, 2]).astype(BF16)], axis=1)
            o = o + jnp.dot(ql, st_ref[c], preferred_element_type=F32)
        mu = jnp.mean(o, axis=-1, keepdims=True)
        oc = o - mu
        var = jnp.mean(oc * oc, axis=-1, keepdims=True)
        gt = g_ref[r, :].astype(F32)
        o_ref[r, :] = (gt * jax.nn.sigmoid(gt) * (oc * lax.rsqrt(var + GN_EPS))).astype(BF16)
        return carry

    def loop(body):
        if nc == 1:
            body(0, 0)
        else:
            lax.fori_loop(0, nc, body, 0, unroll=unroll)

    if has_s0:
        s_ref[fwd, :] = s0_ref[0, 0, 0, 0]
        s_ref[bwd, :] = s0_ref[0, 0, 1, 0]
    else:
        s_ref[...] = jnp.zeros_like(s_ref)
    loop(prep)
    if nc == 1:
        scan(0, 0)
    else:
        lax.fori_loop(0, nc, scan, 0)
    if emit_state:
        sn_ref[0, 0, 0] = s_ref[fwd, :]
        sn_ref[0, 1, 0] = s_ref[bwd, :]
    loop(out_step)


def _decay_tables(log_g, chunk):
    i = jnp.arange(chunk, dtype=F32)
    diff = i[:, None] - i[None, :]
    lf = log_g[0][:, None, None]
    lb = log_g[1][:, None, None]
    dmask = (jnp.where(diff >= 0, jnp.exp(jnp.maximum(diff, 0.0) * lf), 0.0)
             + jnp.where(diff <= 0, jnp.exp(jnp.maximum(-diff, 0.0) * lb), 0.0))
    lf2, lb2 = log_g[0][:, None], log_g[1][:, None]
    vecs = jnp.stack([jnp.exp((i + 1.0)[None, :] * lf2),
                      jnp.exp((chunk - 1.0 - i)[None, :] * lf2),
                      jnp.exp((chunk - i)[None, :] * lb2),
                      jnp.exp(i[None, :] * lb2)], axis=1)
    dvec = jnp.broadcast_to(vecs[..., None], vecs.shape + (LANES,))
    cdec = jnp.exp(chunk * log_g).T.reshape(-1)
    return dmask, dvec, cdec


def _retention(u, og_prev, *, n, nseq, row0, tables, rot, s0, layer, emit_state):
    t = u.shape[0]
    chunk = min(RET_CHUNK, n)
    assert n % chunk == 0 and row0 % n == 0
    rb = row0 // n
    dmask, dvec, cdec = tables
    nc = n // chunk
    qb, vb = COL_K // RET_DK, COL_V // RET_DV
    gb = COL_GRET // RET_DV
    in_specs = [pl.BlockSpec((n, RET_DK), lambda h, b, cd: (rb + b, h)),
                pl.BlockSpec((n, RET_DK), lambda h, b, cd: (rb + b, qb + h)),
                pl.BlockSpec((n, RET_DV), lambda h, b, cd: (rb + b, vb + h)),
                pl.BlockSpec((n, RET_DV), lambda h, b, cd: (rb + b, gb + h)),
                pl.BlockSpec((1, chunk, chunk), lambda h, b, cd: (h, 0, 0)),
                pl.BlockSpec((1, 4, chunk, LANES), lambda h, b, cd: (h, 0, 0, 0))]
    args = [u, u, u, u, dmask, dvec]
    if rot is not None:
        in_specs += [pl.BlockSpec((n, RET_DK), lambda h, b, cd: (0, 0))] * 2
        args += list(rot)
    if s0 is not None:
        in_specs.append(pl.BlockSpec((1, 1, 2, 1, RET_DK, RET_DV),
                                     lambda h, b, cd: (b, layer, 0, h, 0, 0)))
        args.append(s0)
    out_shape = [jax.ShapeDtypeStruct((t, RET_V_W), BF16)]
    out_specs = [pl.BlockSpec((n, RET_DV), lambda h, b, cd: (rb + b, h))]
    if emit_state:
        out_shape.append(jax.ShapeDtypeStruct((nseq, 2, N_RET_HEADS, RET_DK, RET_DV), F32))
        out_specs.append(pl.BlockSpec((1, 2, 1, RET_DK, RET_DV), lambda h, b, cd: (b, 0, h, 0, 0)))
    kern = functools.partial(_retention_kernel, n=n, chunk=chunk, rotary=rot is not None,
                             has_s0=s0 is not None, emit_state=emit_state)
    aliases = {}
    if og_prev is not None:
        in_specs.append(pl.BlockSpec(memory_space=pl.ANY))
        args.append(og_prev)
        aliases = {len(args): 0}
        kern = _drop_ref(kern, len(args))
    grid_spec = pltpu.PrefetchScalarGridSpec(
        num_scalar_prefetch=1, grid=(N_RET_HEADS, nseq), in_specs=in_specs, out_specs=out_specs,
        scratch_shapes=[pltpu.VMEM((n, RET_DK), BF16), pltpu.VMEM((n, RET_DK), BF16),
                        pltpu.VMEM((nc, 2 * RET_DK, RET_DV), F32),
                        pltpu.VMEM((nc, 2 * RET_DK, RET_DV), BF16),
                        pltpu.VMEM((2 * RET_DK, RET_DV), F32)])
    outs = pl.pallas_call(
        kern, out_shape=out_shape, grid_spec=grid_spec, input_output_aliases=aliases,
        compiler_params=_cparams(("parallel", "parallel")),
        name="retention_lat" if rot is not None else "retention_ctx",
    )(cdec, *args)
    return outs if emit_state else outs[0]


def _rotary_tables(n):
    rows = n // GRID_W
    row = jnp.repeat(jnp.arange(rows), GRID_W).astype(F32)
    col = jnp.tile(jnp.arange(GRID_W), rows).astype(F32)
    n_freq = RET_DK // 4
    inv_freq = ROPE_BASE ** (-jnp.arange(n_freq, dtype=F32) / n_freq)
    ang = jnp.concatenate([row[:, None] * inv_freq, col[:, None] * inv_freq], axis=-1)
    cos, sin = jnp.cos(ang), jnp.sin(ang)
    return jnp.concatenate([cos, cos], axis=-1), jnp.concatenate([-sin, sin], axis=-1)


def _chan_dft_kernel(u_ref, cs_ref, zc_ref, zs_ref):
    gw = FOURIER_GROUP_W
    for g in range(N_FOURIER_GROUPS):
        r = jnp.dot(u_ref[:, g * gw:(g + 1) * gw], cs_ref[...], preferred_element_type=F32)
        zc_ref[:, g * gw:(g + 1) * gw] = r[:, :gw].astype(BF16)
        zs_ref[:, g * gw:(g + 1) * gw] = r[:, gw:].astype(BF16)


def _chan_dft(u, cs, tm):
    t = u.shape[0]
    fb = COL_FOUR // FOURIER_W
    out = jax.ShapeDtypeStruct((t, FOURIER_W), BF16)
    return pl.pallas_call(
        _chan_dft_kernel,
        out_shape=[out, out],
        grid=(t // tm,),
        in_specs=[pl.BlockSpec((tm, FOURIER_W), lambda i: (i, fb)),
                  pl.BlockSpec((FOURIER_GROUP_W, 2 * FOURIER_GROUP_W), lambda i: (0, 0))],
        out_specs=[pl.BlockSpec((tm, FOURIER_W), lambda i: (i, 0))] * 2,
        compiler_params=_cparams(("parallel",)),
        name="chan_dft",
    )(u, cs)


def _pos_dft_kernel(dc_ref, ds_ref, zc_ref, zs_ref, o_ref):
    o = jnp.dot(dc_ref[...], zc_ref[...], preferred_element_type=F32)
    o = o + jnp.dot(ds_ref[...], zs_ref[...], preferred_element_type=F32)
    o_ref[...] = o.astype(BF16)


def _pos_dft(zc, zs, uf_prev, *, n, nseq, row0, dmats, tr, tc):
    t = zc.shape[0]
    dc, ds = dmats
    assert n % tr == 0 and FOURIER_W % tc == 0 and row0 % n == 0
    rb = row0 // n
    nr = n // tr
    in_specs = [pl.BlockSpec((tr, n), lambda b, j, i: (i, 0)),
                pl.BlockSpec((tr, n), lambda b, j, i: (i, 0)),
                pl.BlockSpec((n, tc), lambda b, j, i: (rb + b, j)),
                pl.BlockSpec((n, tc), lambda b, j, i: (rb + b, j))]
    args = [dc, ds, zc, zs]
    aliases = {}
    kern = _pos_dft_kernel
    if uf_prev is not None:
        in_specs.append(pl.BlockSpec(memory_space=pl.ANY))
        args.append(uf_prev)
        aliases = {4: 0}
        kern = _drop_ref(kern, 4)
    return pl.pallas_call(
        kern,
        out_shape=jax.ShapeDtypeStruct((t, FOURIER_W), BF16),
        grid=(nseq, FOURIER_W // tc, nr),
        in_specs=in_specs,
        out_specs=pl.BlockSpec((tr, tc), lambda b, j, i: ((rb + b) * nr + i, j)),
        input_output_aliases=aliases,
        compiler_params=_cparams(("parallel", "parallel", "parallel")),
        name="pos_dft",
    )(*args)


def _dft_mats(n, scale):
    h = n // 2
    assert n % 4 == 0
    j = jnp.arange(h, dtype=jnp.int32)
    ang = ((j[:, None] * j[None, :]) % n).astype(F32) * (2.0 * jnp.pi / n)
    sgn = jnp.where(j % 2 == 0, 1.0, -1.0).astype(F32)

    def full(corner):
        top = jnp.concatenate([corner, corner * sgn[:, None]], axis=1)
        return jnp.concatenate([top, top * jnp.concatenate([sgn, sgn])[None, :]], axis=0).astype(BF16)

    return full(jnp.cos(ang) * scale), full(-jnp.sin(ang) * scale)


def _merge_kernel(grp_ref, *refs, n_x, nct):
    del grp_ref
    x_refs = refs[:n_x]
    (og_ref, uf_ref, ga_ref, gb_ref, mod_ref, g2_ref, wro_ref, wf_ref, wo_ref, wr_ref, br_ref,
     xo_ref, h2_ref, lp_ref, tw_ref, cnt_ref) = refs[n_x:]
    tm = og_ref.shape[0]

    a = jnp.dot(og_ref[...], wro_ref[...], preferred_element_type=F32)
    b = jnp.dot(uf_ref[...], wf_ref[...], preferred_element_type=F32)
    merged = (jax.nn.sigmoid(ga_ref[...].astype(F32)) * a
              + jax.nn.sigmoid(gb_ref[...].astype(F32)) * b)
    y = jnp.dot(merged.astype(BF16), wo_ref[...], preferred_element_type=F32)
    m = mod_ref[0]
    x = _stream_tile(x_refs, pl.program_id(0), nct) + m[2:3] * y
    xo_ref[...] = x
    h2 = _rms_mod(x, g2_ref[...], m[3:4], m[4:5])
    h_hi = h2.astype(BF16)
    h2_ref[...] = h_hi
    h_lo = (h2 - h_hi.astype(F32)).astype(BF16)
    p_hi = jnp.dot(h_hi, wr_ref[...], preferred_element_type=F32)
    p_lo = jnp.dot(h_lo, wr_ref[...], preferred_element_type=F32)
    logits = p_hi + pltpu.roll(p_hi, E_PAD - N_EXPERTS, 1) + p_lo + br_ref[...]
    for s in range(tm // MOE_TILE):
        lp, tw, n = _route_tile(logits[s * MOE_TILE:(s + 1) * MOE_TILE])
        lp_ref[pl.ds(s * MOE_TILE, MOE_TILE), :] = lp
        tw_ref[pl.ds(s * MOE_TILE, MOE_TILE), :] = tw
        cnt_ref[s] = n


def _route_tile(logits):
    tt = logits.shape[0]
    lane = lax.broadcasted_iota(jnp.int32, (tt, E_PAD), 1)
    rest = logits
    vals, sels = [], []
    for _ in range(TOP_K):
        mx = jnp.max(rest, axis=-1, keepdims=True)
        idx = jnp.min(jnp.where(rest == mx, lane, E_PAD), axis=-1, keepdims=True)
        sel = lane == idx
        vals.append(mx)
        sels.append(sel)
        rest = jnp.where(sel, -jnp.inf, rest)
    exps = [jnp.exp(v - vals[0]) for v in vals]
    den = exps[0] + exps[1] + exps[2] + exps[3]

    picked = jnp.zeros((tt, E_PAD), F32)
    for sel in sels:
        picked = picked + sel.astype(F32)
    n = jnp.sum(picked, axis=0, keepdims=True).astype(jnp.int32)
    seg = ((n + (SEG_ALIGN - 1)) // SEG_ALIGN) * SEG_ALIGN
    er = lax.broadcasted_iota(jnp.int32, (E_PAD, E_PAD), 0)
    ec = lax.broadcasted_iota(jnp.int32, (E_PAD, E_PAD), 1)
    seg_rows = jnp.broadcast_to(seg.astype(F32), (SEG_ALIGN, E_PAD)).astype(BF16)
    off = jnp.dot(seg_rows, (er < ec).astype(BF16), preferred_element_type=F32)[0:1]
    ri = lax.broadcasted_iota(jnp.int32, (tt, tt), 0)
    ci = lax.broadcasted_iota(jnp.int32, (tt, tt), 1)
    before = jnp.dot((ri > ci).astype(BF16), picked.astype(BF16), preferred_element_type=F32) + off

    lp = jnp.zeros((tt, E_PAD), jnp.int32)
    tw = jnp.zeros((tt, E_PAD), F32)
    for k in range(TOP_K):
        row = jnp.sum(jnp.where(sels[k], before, 0.0), axis=-1, keepdims=True).astype(jnp.int32)
        lp = jnp.where(lane == k, row, lp)
        tw = jnp.where(lane == k, exps[k] / den, tw)
    return lp, tw, n


def _merge(xs, og, uf, u, tile_grp, mod, g2, wro, wf, wo, wr, br, tm):
    t, d = og.shape[0], wo.shape[1]
    assert tm % MOE_TILE == 0
    sub = tm // MOE_TILE
    ga_b, gb_b = COL_GA // d, COL_GA // d + 1
    full = lambda shape: pl.BlockSpec(shape, lambda i, grp: (0,) * len(shape))
    row = lambda w: pl.BlockSpec((tm, w), lambda i, grp: (i, 0))
    grid_spec = pltpu.PrefetchScalarGridSpec(
        num_scalar_prefetch=1,
        grid=(t // tm,),
        in_specs=_stream_specs(xs, tm, 1) + [
            row(RET_V_W), row(FOURIER_W),
            pl.BlockSpec((tm, d), lambda i, grp: (i, ga_b)),
            pl.BlockSpec((tm, d), lambda i, grp: (i, gb_b)),
            pl.BlockSpec((1, MOD_ROWS, d), lambda i, grp: (grp[i], 0, 0)),
            full((1, d)), full(wro.shape), full(wf.shape), full(wo.shape),
            full(wr.shape), full(br.shape)],
        out_specs=[row(d), row(d), row(E_PAD), row(E_PAD),
                   pl.BlockSpec((sub, 1, E_PAD), lambda i, grp: (i, 0, 0))],
    )
    return pl.pallas_call(
        functools.partial(_merge_kernel, n_x=len(xs), nct=xs[0].shape[0] // tm),
        out_shape=[jax.ShapeDtypeStruct((t, d), F32), jax.ShapeDtypeStruct((t, d), BF16),
                   jax.ShapeDtypeStruct((t, E_PAD), jnp.int32), jax.ShapeDtypeStruct((t, E_PAD), F32),
                   jax.ShapeDtypeStruct((t // MOE_TILE, 1, E_PAD), jnp.int32)],
        grid_spec=grid_spec,
        compiler_params=_cparams(("parallel",)),
        name="merge_route",
    )(tile_grp, *xs, og, uf, u, u, mod, g2.reshape(1, d), wro, wf, wo, wr, br)


def _router_operands(w_router, b_router):
    hi = w_router.astype(BF16)
    lo = (w_router - hi.astype(F32)).astype(BF16)
    wr = jnp.concatenate([hi, lo], axis=-1)
    wr = jnp.pad(wr, ((0, 0), (0, 0), (0, E_PAD - 2 * N_EXPERTS)))
    br = jnp.pad(b_router, ((0, 0), (0, E_PAD - N_EXPERTS)), constant_values=-1e30)
    return wr, br[:, None, :]


def _segment_copies(i, off_ref, seg_ref, gs_ref, local, remote, sem, to_remote):
    for e in range(N_EXPERTS):
        j = i * N_EXPERTS + e
        n = pl.multiple_of(seg_ref[j], SEG_ALIGN)

        @pl.when(n > 0)
        def _():
            lo = local.at[pl.ds(pl.multiple_of(off_ref[j], SEG_ALIGN), n)]
            re = remote.at[pl.ds(pl.multiple_of(gs_ref[j], SEG_ALIGN), n)]
            src, dst = (lo, re) if to_remote else (re, lo)
            pltpu.make_async_copy(src, dst, sem).start()


def _segment_wait(i, off_ref, seg_ref, local, remote, sem):
    j = i * N_EXPERTS + N_EXPERTS - 1
    tot = pl.multiple_of(off_ref[j] + seg_ref[j], SEG_ALIGN)
    pltpu.make_async_copy(local.at[pl.ds(0, tot)], remote.at[pl.ds(0, tot)], sem).wait()


def _local_row_iota(tt):
    return lax.broadcasted_iota(jnp.int32, (tt, SEG_ROWS), 1)


def _dispatch_kernel(off_ref, seg_ref, gs_ref, pend_ref, h_ref, lp_ref, xp_ref, xbuf_ref, zbuf_ref,
                     sem, zsem):
    i = pl.program_id(0)
    nt = pl.num_programs(0)
    slot = i % 2
    bm = zbuf_ref.shape[0]

    @pl.when(i == 0)
    def _():
        zbuf_ref[...] = jnp.zeros_like(zbuf_ref)

        def tail(e):
            start = jnp.where(e == 0, 0, pend_ref[jnp.maximum(e - 1, 0)])
            end = pend_ref[e]
            return end > start, xp_ref.at[pl.ds(pl.multiple_of(jnp.maximum(end - bm, 0), SEG_ALIGN), bm)]

        for e in range(N_EXPERTS):
            has, dst = tail(e)

            @pl.when(has)
            def _():
                pltpu.make_async_copy(zbuf_ref, dst, zsem).start()
        for e in range(N_EXPERTS):
            has, dst = tail(e)

            @pl.when(has)
            def _():
                pltpu.make_async_copy(zbuf_ref, dst, zsem).wait()

    lp = lp_ref[...]
    rid = _local_row_iota(lp.shape[0])
    hit = rid == lp[:, 0:1]
    for k in range(1, TOP_K):
        hit = hit | (rid == lp[:, k:k + 1])
    xbuf_ref[slot] = lax.dot_general(hit.astype(BF16), h_ref[...], (((0,), (0,)), ((), ())),
                                     preferred_element_type=F32)
    _segment_copies(i, off_ref, seg_ref, gs_ref, xbuf_ref.at[slot], xp_ref, sem.at[slot], True)

    @pl.when(i > 0)
    def _():
        _segment_wait(i - 1, off_ref, seg_ref, xbuf_ref.at[1 - slot], xp_ref, sem.at[1 - slot])

    @pl.when(i == nt - 1)
    def _():
        _segment_wait(i, off_ref, seg_ref, xbuf_ref.at[slot], xp_ref, sem.at[slot])


def _dispatch(h2, lp, lay):
    t, d = h2.shape
    tt = MOE_TILE
    grid_spec = pltpu.PrefetchScalarGridSpec(
        num_scalar_prefetch=4,
        grid=(t // tt,),
        in_specs=[pl.BlockSpec((tt, d), lambda i, *s: (i, 0)),
                  pl.BlockSpec((tt, E_PAD), lambda i, *s: (i, 0))],
        out_specs=pl.BlockSpec(memory_space=pl.ANY),
        scratch_shapes=[pltpu.VMEM((2, SEG_ROWS, d), F32), pltpu.VMEM((MOE_BLOCK, d), F32),
                        pltpu.SemaphoreType.DMA((2,)), pltpu.SemaphoreType.DMA],
    )
    return pl.pallas_call(
        _dispatch_kernel,
        out_shape=jax.ShapeDtypeStruct((lay["p_total"], d), F32),
        grid_spec=grid_spec,
        compiler_params=_cparams(("arbitrary",)),
        name="moe_dispatch",
    )(lay["off"], lay["seg"], lay["gseg"], lay["p_end"], h2, lp)


def _expert_kernel(be_ref, nu_ref, x_ref, w1_ref, b1_ref, w2_ref, b2_ref, y_ref, w1b_ref, w2b_ref):
    i = pl.program_id(0)
    de = w2b_ref.shape[0]
    used = i < nu_ref[0]

    @pl.when(used & ((i == 0) | (be_ref[i] != be_ref[jnp.maximum(i - 1, 0)])))
    def _():
        w1b_ref[...] = w1_ref[0, 0].astype(BF16)
        w2b_ref[...] = w2_ref[0, 0].astype(BF16)

    @pl.when(used)
    def _():
        hh = jnp.dot(x_ref[...].astype(BF16), w1b_ref[...], preferred_element_type=F32) + b1_ref[0, 0]
        gate = jnp.minimum(hh[:, :de], SWIGLU_LIMIT)
        lin = jnp.clip(hh[:, de:], -SWIGLU_LIMIT, SWIGLU_LIMIT)
        act = (lin + 1.0) * (gate * jax.nn.sigmoid(SWIGLU_ALPHA * gate))
        y_ref[...] = jnp.dot(act.astype(BF16), w2b_ref[...], preferred_element_type=F32) + b2_ref[0, 0]

    @pl.when(jnp.logical_not(used))
    def _():
        y_ref[...] = jnp.zeros_like(y_ref)


def _experts(xp, lay, layer, w1, b1, w2, b2):
    p_total, d = xp.shape
    depth, ne, _, de2 = w1.shape
    de = w2.shape[2]
    bm = MOE_BLOCK
    wmap = lambda i, be, nu: (layer, be[i], 0, 0)
    grid_spec = pltpu.PrefetchScalarGridSpec(
        num_scalar_prefetch=2,
        grid=(p_total // bm,),
        in_specs=[pl.BlockSpec((bm, d), lambda i, be, nu: (jnp.minimum(i, nu[0] - 1), 0)),
                  pl.BlockSpec((1, 1, d, de2), wmap),
                  pl.BlockSpec((1, 1, 1, de2), wmap),
                  pl.BlockSpec((1, 1, de, d), wmap),
                  pl.BlockSpec((1, 1, 1, d), wmap)],
        out_specs=pl.BlockSpec((bm, d), lambda i, be, nu: (i, 0)),
        scratch_shapes=[pltpu.VMEM((d, de2), BF16), pltpu.VMEM((de, d), BF16)],
    )
    return pl.pallas_call(
        _expert_kernel,
        out_shape=jax.ShapeDtypeStruct((p_total, d), F32),
        grid_spec=grid_spec,
        compiler_params=_cparams(("arbitrary",)),
        name="moe_experts",
    )(lay["blk_e"], lay["n_used"], xp, w1, b1.reshape(depth, ne, 1, de2), w2, b2.reshape(depth, ne, 1, d))


def _combine_kernel(grp_ref, off_ref, seg_ref, gs_ref, lp_ref, x_ref, tw_ref, mod_ref, fg_ref, yb_ref,
                    *refs, split_out, nct):
    del grp_ref
    o_refs = refs[:2] if split_out else refs[:1]
    ybuf_ref, sem = refs[len(o_refs):]
    i = pl.program_id(0)
    nt = pl.num_programs(0)
    slot = i % 2

    def gather(tile, s):
        _segment_copies(tile, off_ref, seg_ref, gs_ref, ybuf_ref.at[s], yb_ref, sem.at[s], False)

    @pl.when(i == 0)
    def _():
        ybuf_ref[...] = jnp.zeros_like(ybuf_ref)
        gather(0, 0)

    @pl.when(i + 1 < nt)
    def _():
        gather(i + 1, 1 - slot)

    _segment_wait(i, off_ref, seg_ref, ybuf_ref.at[slot], yb_ref, sem.at[slot])
    lp = lp_ref[...]
    tw = tw_ref[...]
    rid = _local_row_iota(lp.shape[0])
    wsel = jnp.where(rid == lp[:, 0:1], tw[:, 0:1], 0.0)
    for k in range(1, TOP_K):
        wsel = jnp.where(rid == lp[:, k:k + 1], tw[:, k:k + 1], wsel)
    acc = jnp.dot(wsel.astype(BF16), ybuf_ref[slot].astype(BF16), preferred_element_type=F32)
    x = x_ref[...] + mod_ref[0][5:6] * acc
    if not split_out:
        o_refs[0][...] = x
    else:
        y = x * lax.rsqrt(jnp.mean(x * x, axis=-1, keepdims=True) + NORM_EPS) * fg_ref[...]

        @pl.when(i < nct)
        def _():
            o_refs[0][...] = y

        @pl.when(i >= nct)
        def _():
            o_refs[1][...] = y


def _combine(x, yb, lp, tw, lay, tile_grp, mod, final_g, t_ctx, split_out):
    t, d = x.shape
    tt = MOE_TILE
    nct = t_ctx // tt
    if split_out:
        out_shape = [jax.ShapeDtypeStruct((t_ctx, d), F32), jax.ShapeDtypeStruct((t - t_ctx, d), F32)]
        out_specs = [pl.BlockSpec((tt, d), lambda i, *s: (jnp.minimum(i, nct - 1), 0)),
                     pl.BlockSpec((tt, d), lambda i, *s: (jnp.maximum(i - nct, 0), 0))]
    else:
        out_shape = [jax.ShapeDtypeStruct((t, d), F32)]
        out_specs = [pl.BlockSpec((tt, d), lambda i, *s: (i, 0))]
    grid_spec = pltpu.PrefetchScalarGridSpec(
        num_scalar_prefetch=4,
        grid=(t // tt,),
        in_specs=[pl.BlockSpec((tt, E_PAD), lambda i, *s: (i, 0)),
                  pl.BlockSpec((tt, d), lambda i, *s: (i, 0)),
                  pl.BlockSpec((tt, E_PAD), lambda i, *s: (i, 0)),
                  pl.BlockSpec((1, MOD_ROWS, d), lambda i, grp, *s: (grp[i], 0, 0)),
                  pl.BlockSpec((1, d), lambda i, *s: (0, 0)),
                  pl.BlockSpec(memory_space=pl.ANY)],
        out_specs=out_specs,
        scratch_shapes=[pltpu.VMEM((2, SEG_ROWS, d), F32), pltpu.SemaphoreType.DMA((2,))],
    )
    return pl.pallas_call(
        functools.partial(_combine_kernel, split_out=split_out, nct=nct),
        out_shape=out_shape,
        grid_spec=grid_spec,
        compiler_params=_cparams(("arbitrary",)),
        name="moe_combine",
    )(tile_grp, lay["off"], lay["seg"], lay["gseg"], lp, x, tw, mod, final_g.reshape(1, d), yb)


def _segment_layout(cnt, t):
    bm = MOE_BLOCK
    nt = t // MOE_TILE
    n = cnt[:, 0, :N_EXPERTS]
    seg = ((n + (SEG_ALIGN - 1)) // SEG_ALIGN) * SEG_ALIGN
    off = jnp.cumsum(seg, axis=1) - seg
    tot = jnp.sum(seg, axis=0)
    padded = ((tot + bm - 1) // bm) * bm
    p_end = jnp.cumsum(padded)
    gseg = (p_end - padded)[None, :] + jnp.cumsum(seg, axis=0) - seg
    max_rows = t * TOP_K + nt * N_EXPERTS * (SEG_ALIGN - 1)
    n_blk = -(-max_rows // bm) + N_EXPERTS
    blk_start = jnp.arange(n_blk, dtype=jnp.int32) * bm
    blk_e = jnp.minimum(jnp.sum(p_end[None, :] <= blk_start[:, None], axis=1), N_EXPERTS - 1)
    i32 = lambda a: a.reshape(-1).astype(jnp.int32)
    return dict(seg=i32(seg), off=i32(off), gseg=i32(gseg), p_end=i32(p_end), blk_e=i32(blk_e),
                n_used=i32(p_end[-1:] // bm), p_total=n_blk * bm)


def _tile_groups(t, t_ctx, dec_seq, tm):
    starts = jnp.arange(t // tm, dtype=jnp.int32) * tm
    return jnp.where(starts < t_ctx, 0, 1 + (starts - t_ctx) // dec_seq).astype(jnp.int32)


def kernel(x_prompt, x_sample, state_ret, c, c_ctx, w_ada, b_ada, norm1_g, norm2_g, w_in, ret_decay_logit, w_ret_o, w_four, w_out, w_router, b_router, w1, b1, w2, b2, final_g):
    batch, seq, d = x_prompt.shape
    dec_batch, dec_seq, _ = x_sample.shape
    depth = w_in.shape[0]
    t_ctx, t_lat = batch * seq, dec_batch * dec_seq
    t = t_ctx + t_lat
    tm_in = min(1024, t_ctx, dec_seq)
    tm = min(512, t_ctx, dec_seq)
    assert t_ctx % tm_in == 0 and dec_seq % tm_in == 0 and t_ctx % dec_seq == 0

    n_cond = 16
    cond = jnp.zeros((n_cond, d), F32).at[0].set(c_ctx).at[1:1 + dec_batch].set(c)
    mod = _adaln_mod(cond, w_ada, b_ada).reshape(depth, n_cond, 6, d)
    mod = jnp.pad(mod, ((0, 0), (0, 0), (0, MOD_ROWS - 6), (0, 0)))
    grp_in = _tile_groups(t, t_ctx, dec_seq, tm_in)
    grp = _tile_groups(t, t_ctx, dec_seq, tm)
    grp_moe = _tile_groups(t, t_ctx, dec_seq, MOE_TILE)

    rot = _rotary_tables(dec_seq)
    gw = FOURIER_GROUP_W
    jc = jnp.arange(gw, dtype=jnp.int32)
    ang_c = ((jc[:, None] * jc[None, :]) % gw).astype(F32) * (2.0 * jnp.pi / gw)
    cs = jnp.concatenate([jnp.cos(ang_c), jnp.sin(ang_c)], axis=1).astype(BF16)
    dm_ctx = _dft_mats(seq, (seq * gw) ** -0.5)
    dm_lat = _dft_mats(dec_seq, (dec_seq * gw) ** -0.5)
    log_g = jax.nn.log_sigmoid(ret_decay_logit.astype(F32))
    wr, br = _router_operands(w_router, b_router)

    xs = [x_prompt.reshape(t_ctx, d), x_sample.reshape(t_lat, d)]
    states = []
    for l in range(depth):
        last = l == depth - 1
        u = _in_proj(xs, grp_in, mod[l], norm1_g[l], w_in[l].astype(BF16), tm_in, 1536)
        tables = _decay_tables(log_g[l], min(RET_CHUNK, seq, dec_seq))
        og, st = _retention(u, None, n=seq, nseq=batch, row0=0, tables=tables, rot=None, s0=None,
                            layer=l, emit_state=True)
        og = _retention(u, og, n=dec_seq, nseq=dec_batch, row0=t_ctx, tables=tables, rot=rot,
                        s0=state_ret, layer=l, emit_state=False)
        states.append(st)
        zc, zs = _chan_dft(u, cs, tm)
        uf = _pos_dft(zc, zs, None, n=seq, nseq=batch, row0=0, dmats=dm_ctx, tr=seq, tc=FOURIER_W)
        uf = _pos_dft(zc, zs, uf, n=dec_seq, nseq=dec_batch, row0=t_ctx, dmats=dm_lat,
                      tr=min(512, dec_seq), tc=512)
        x, h2, lp, tw, cnt = _merge(
            xs, og, uf, u, grp, mod[l], norm2_g[l], w_ret_o[l].astype(BF16), w_four[l].astype(BF16),
            w_out[l].astype(BF16), wr[l], br[l], tm)
        lay = _segment_layout(cnt, t)
        xp = _dispatch(h2, lp, lay)
        yb = _experts(xp, lay, l, w1, b1, w2, b2)
        xs = _combine(x, yb, lp, tw, lay, grp_moe, mod[l], final_g, t_ctx, split_out=last)

    new_state = jnp.stack(states, axis=1).astype(x_prompt.dtype)
    return (xs[0].reshape(batch, seq, d), xs[1].reshape(dec_batch, dec_seq, d), new_state)
```

```python
import functools

import jax
import jax.numpy as jnp
from jax import lax
from jax.experimental import pallas as pl
from jax.experimental.pallas import tpu as pltpu

F32 = jnp.float32
BF16 = jnp.bfloat16

N_RET_HEADS = 8
RET_DK = 128
RET_DV = 256
RET_QK_W = N_RET_HEADS * RET_DK
RET_V_W = N_RET_HEADS * RET_DV
N_FOURIER_GROUPS = 4
FOURIER_GROUP_W = 256
FOURIER_W = N_FOURIER_GROUPS * FOURIER_GROUP_W
GRID_W = 64
N_EXPERTS = 32
TOP_K = 4
SWIGLU_LIMIT = 7.0
SWIGLU_ALPHA = 1.702
ROPE_BASE = 10000.0
NORM_EPS = 1e-6
GN_EPS = 1e-5

COL_Q = 0
COL_K = COL_Q + RET_QK_W
COL_V = COL_K + RET_QK_W
COL_GRET = COL_V + RET_V_W
COL_FOUR = COL_GRET + RET_V_W
COL_GA = COL_FOUR + FOURIER_W

LANES = 128
VMEM_LIMIT_BYTES = 56 * 1024 * 1024

MOD_ROWS = 8
RET_CHUNK = 256
MOE_BLOCK = 512
MOE_TILE = 256
SEG_ALIGN = 8
SEG_ROWS = MOE_TILE * TOP_K + N_EXPERTS * SEG_ALIGN
E_PAD = LANES


def _cparams(sem, vmem=VMEM_LIMIT_BYTES):
    return pltpu.CompilerParams(dimension_semantics=sem, vmem_limit_bytes=vmem)


def _drop_ref(kern, idx):
    def wrapped(*refs):
        return kern(*refs[:idx], *refs[idx + 1:])
    return wrapped


def _mod_kernel(c_ref, w_ref, b_ref, o_ref):
    c = c_ref[...]
    s = c * jax.nn.sigmoid(c)
    o_ref[0] = jnp.dot(s, w_ref[0], precision=lax.Precision.HIGHEST,
                       preferred_element_type=F32) + b_ref[0]


def _adaln_mod(cond, w_ada, b_ada):
    depth, d, w6 = w_ada.shape
    r = cond.shape[0]
    tn = 1536
    assert w6 % tn == 0
    return pl.pallas_call(
        _mod_kernel,
        out_shape=jax.ShapeDtypeStruct((depth, r, w6), F32),
        grid=(depth, w6 // tn),
        in_specs=[pl.BlockSpec((r, d), lambda l, j: (0, 0)),
                  pl.BlockSpec((1, d, tn), lambda l, j: (l, 0, j)),
                  pl.BlockSpec((1, 1, tn), lambda l, j: (l, 0, j))],
        out_specs=pl.BlockSpec((1, r, tn), lambda l, j: (l, 0, j)),
        compiler_params=_cparams(("parallel", "parallel")),
        name="adaln_mod",
    )(cond, w_ada, b_ada.reshape(depth, 1, w6))


def _stream_specs(xs, tm, imap_args):
    d = xs[0].shape[1]
    if len(xs) == 1:
        return [pl.BlockSpec((tm, d), lambda i, *a: (i, 0))]
    nct = xs[0].shape[0] // tm
    return [pl.BlockSpec((tm, d), lambda i, *a: (jnp.minimum(i, nct - 1), 0)),
            pl.BlockSpec((tm, d), lambda i, *a: (jnp.maximum(i - nct, 0), 0))]


def _stream_tile(x_refs, i, nct):
    if len(x_refs) == 1:
        return x_refs[0][...]
    return jnp.where(i < nct, x_refs[0][...], x_refs[1][...])


def _rms_mod(x, g, shift, scale):
    y = x * lax.rsqrt(jnp.mean(x * x, axis=-1, keepdims=True) + NORM_EPS) * g
    return y * (1.0 + scale) + shift


def _in_proj_kernel(grp_ref, *refs, n_x, nct):
    del grp_ref
    x_refs = refs[:n_x]
    mod_ref, g_ref, w_ref, o_ref, h_ref = refs[n_x:]

    @pl.when(pl.program_id(1) == 0)
    def _():
        m = mod_ref[0]
        x = _stream_tile(x_refs, pl.program_id(0), nct)
        h_ref[...] = _rms_mod(x, g_ref[...], m[0:1], m[1:2]).astype(BF16)

    o_ref[...] = jnp.dot(h_ref[...], w_ref[...], preferred_element_type=F32).astype(BF16)


def _in_proj(xs, tile_grp, mod, g, w, tm, tn):
    t = sum(x.shape[0] for x in xs)
    d, n = w.shape
    assert t % tm == 0 and n % tn == 0 and all(x.shape[0] % tm == 0 for x in xs)
    x_specs = [pl.BlockSpec(s.block_shape, (lambda f: lambda i, j, grp: f(i))(s.index_map))
               for s in _stream_specs(xs, tm, 1)]
    grid_spec = pltpu.PrefetchScalarGridSpec(
        num_scalar_prefetch=1,
        grid=(t // tm, n // tn),
        in_specs=x_specs + [pl.BlockSpec((1, MOD_ROWS, d), lambda i, j, grp: (grp[i], 0, 0)),
                            pl.BlockSpec((1, d), lambda i, j, grp: (0, 0)),
                            pl.BlockSpec((d, tn), lambda i, j, grp: (0, j))],
        out_specs=pl.BlockSpec((tm, tn), lambda i, j, grp: (i, j)),
        scratch_shapes=[pltpu.VMEM((tm, d), BF16)],
    )
    return pl.pallas_call(
        functools.partial(_in_proj_kernel, n_x=len(xs), nct=xs[0].shape[0] // tm),
        out_shape=jax.ShapeDtypeStruct((t, n), BF16),
        grid_spec=grid_spec,
        compiler_params=_cparams(("parallel", "arbitrary")),
        name="in_proj",
    )(tile_grp, *xs, mod, g.reshape(1, d), w)


def _retention_kernel(cdec_ref, *refs, n, chunk, rotary, has_s0, emit_state, heads):
    for hh in range(heads):
        _retention_head(cdec_ref, refs, hh, n=n, chunk=chunk, rotary=rotary, has_s0=has_s0,
                        emit_state=emit_state, heads=heads)


def _retention_head(cdec_ref, refs, hh, *, n, chunk, rotary, has_s0, emit_state, heads):
    refs = list(refs)
    q_ref, k_ref, v_ref, g_ref, dmask_ref, dvec_ref = refs[:6]
    pos = 6
    if rotary:
        cos_ref, sin_ref = refs[pos:pos + 2]
        pos += 2
    if has_s0:
        s0_ref = refs[pos]
        pos += 1
    o_ref = refs[pos]
    pos += 1
    if emit_state:
        sn_ref = refs[pos]
        pos += 1
    qs_ref, ks_ref, kv_ref, st_ref, s_ref = [ref.at[hh] for ref in refs[pos:pos + 5]]
    qcols = pl.ds(hh * RET_DK, RET_DK)
    vcols = pl.ds(hh * RET_DV, RET_DV)

    nc = n // chunk
    cross = has_s0 or nc > 1
    unroll = 2 if nc % 2 == 0 else 1
    h = pl.program_id(0) * heads + hh
    cdec_f = cdec_ref[2 * h]
    cdec_b = cdec_ref[2 * h + 1]
    tn_dims = (((0,), (0,)), ((), ()))
    nt_dims = (((1,), (1,)), ((), ()))
    fwd = pl.ds(0, RET_DK)
    bwd = pl.ds(RET_DK, RET_DK)

    def rows(c):
        return pl.ds(pl.multiple_of(c * chunk, chunk), chunk)

    def prep(c, carry):
        r = rows(c)
        q = q_ref[r, qcols].astype(F32) * (RET_DK ** -0.5)
        k = k_ref[r, qcols].astype(F32)
        if rotary:
            cs, sn = cos_ref[r, :], sin_ref[r, :]
            q = q * cs + pltpu.roll(q, RET_DK // 2, 1) * sn
            k = k * cs + pltpu.roll(k, RET_DK // 2, 1) * sn
        qs_ref[r, :] = q.astype(BF16)
        kb = k.astype(BF16)
        ks_ref[r, :] = kb
        kf = kb.astype(F32)
        kd = jnp.concatenate([(kf * dvec_ref[hh, 1]).astype(BF16),
                              (kf * dvec_ref[hh, 3]).astype(BF16)], axis=1)
        kv_ref[c] = lax.dot_general(kd, v_ref[r, vcols], tn_dims, preferred_element_type=F32)
        return carry

    def scan(i, carry):
        cb = nc - 1 - i
        st_ref[i, fwd, :] = s_ref[fwd, :].astype(BF16)
        s_ref[fwd, :] = s_ref[fwd, :] * cdec_f + kv_ref[i, fwd, :]
        st_ref[cb, bwd, :] = s_ref[bwd, :].astype(BF16)
        s_ref[bwd, :] = s_ref[bwd, :] * cdec_b + kv_ref[cb, bwd, :]
        return carry

    def out_step(c, carry):
        r = rows(c)
        qc = qs_ref[r, :]
        sc = lax.dot_general(qc, ks_ref[r, :], nt_dims, preferred_element_type=F32)
        o = jnp.dot((sc * dmask_ref[hh]).astype(BF16), v_ref[r, vcols], preferred_element_type=F32)
        if cross:
            qf = qc.astype(F32)
            ql = jnp.concatenate([(qf * dvec_ref[hh, 0]).astype(BF16),
                                  (qf * dvec_ref[hh, 2]).astype(BF16)], axis=1)
            o = o + jnp.dot(ql, st_ref[c], preferred_element_type=F32)
        mu = jnp.mean(o, axis=-1, keepdims=True)
        oc = o - mu
        var = jnp.mean(oc * oc, axis=-1, keepdims=True)
        gt = g_ref[r, vcols].astype(F32)
        o_ref[r, vcols] = (gt * jax.nn.sigmoid(gt) * (oc * lax.rsqrt(var + GN_EPS))).astype(BF16)
        return carry

    def loop(body):
        if nc == 1:
            body(0, 0)
        else:
            lax.fori_loop(0, nc, body, 0, unroll=unroll)

    if has_s0:
        s_ref[fwd, :] = s0_ref[0, 0, 0, hh]
        s_ref[bwd, :] = s0_ref[0, 0, 1, hh]
    else:
        s_ref[...] = jnp.zeros_like(s_ref)
    loop(prep)
    if nc == 1:
        scan(0, 0)
    else:
        lax.fori_loop(0, nc, scan, 0)
    if emit_state:
        sn_ref[0, 0, hh] = s_ref[fwd, :]
        sn_ref[0, 1, hh] = s_ref[bwd, :]
    loop(out_step)


def _decay_tables(log_g, chunk):
    i = jnp.arange(chunk, dtype=F32)
    diff = i[:, None] - i[None, :]
    lf = log_g[0][:, None, None]
    lb = log_g[1][:, None, None]
    dmask = (jnp.where(diff >= 0, jnp.exp(jnp.maximum(diff, 0.0) * lf), 0.0)
             + jnp.where(diff <= 0, jnp.exp(jnp.maximum(-diff, 0.0) * lb), 0.0))
    lf2, lb2 = log_g[0][:, None], log_g[1][:, None]
    vecs = jnp.stack([jnp.exp((i + 1.0)[None, :] * lf2),
                      jnp.exp((chunk - 1.0 - i)[None, :] * lf2),
                      jnp.exp((chunk - i)[None, :] * lb2),
                      jnp.exp(i[None, :] * lb2)], axis=1)
    dvec = jnp.broadcast_to(vecs[..., None], vecs.shape + (LANES,))
    cdec = jnp.exp(chunk * log_g).T.reshape(-1)
    return dmask, dvec, cdec


def _retention(u, og_prev, *, n, nseq, row0, tables, rot, s0, layer, emit_state, heads):
    t = u.shape[0]
    chunk = min(RET_CHUNK, n)
    assert n % chunk == 0 and row0 % n == 0 and N_RET_HEADS % heads == 0
    rb = row0 // n
    dmask, dvec, cdec = tables
    nc = n // chunk
    kw, vw = heads * RET_DK, heads * RET_DV
    qb, vb, gb = COL_K // kw, COL_V // vw, COL_GRET // vw
    in_specs = [pl.BlockSpec((n, kw), lambda h, b, cd: (rb + b, h)),
                pl.BlockSpec((n, kw), lambda h, b, cd: (rb + b, qb + h)),
                pl.BlockSpec((n, vw), lambda h, b, cd: (rb + b, vb + h)),
                pl.BlockSpec((n, vw), lambda h, b, cd: (rb + b, gb + h)),
                pl.BlockSpec((heads, chunk, chunk), lambda h, b, cd: (h, 0, 0)),
                pl.BlockSpec((heads, 4, chunk, LANES), lambda h, b, cd: (h, 0, 0, 0))]
    args = [u, u, u, u, dmask, dvec]
    if rot is not None:
        in_specs += [pl.BlockSpec((n, RET_DK), lambda h, b, cd: (0, 0))] * 2
        args += list(rot)
    if s0 is not None:
        in_specs.append(pl.BlockSpec((1, 1, 2, heads, RET_DK, RET_DV),
                                     lambda h, b, cd: (b, layer, 0, h, 0, 0)))
        args.append(s0)
    out_shape = [jax.ShapeDtypeStruct((t, RET_V_W), BF16)]
    out_specs = [pl.BlockSpec((n, vw), lambda h, b, cd: (rb + b, h))]
    if emit_state:
        out_shape.append(jax.ShapeDtypeStruct((nseq, 2, N_RET_HEADS, RET_DK, RET_DV), F32))
        out_specs.append(pl.BlockSpec((1, 2, heads, RET_DK, RET_DV), lambda h, b, cd: (b, 0, h, 0, 0)))
    kern = functools.partial(_retention_kernel, n=n, chunk=chunk, rotary=rot is not None,
                             has_s0=s0 is not None, emit_state=emit_state, heads=heads)
    aliases = {}
    if og_prev is not None:
        in_specs.append(pl.BlockSpec(memory_space=pl.ANY))
        args.append(og_prev)
        aliases = {len(args): 0}
        kern = _drop_ref(kern, len(args))
    grid_spec = pltpu.PrefetchScalarGridSpec(
        num_scalar_prefetch=1, grid=(N_RET_HEADS // heads, nseq), in_specs=in_specs, out_specs=out_specs,
        scratch_shapes=[pltpu.VMEM((heads, n, RET_DK), BF16), pltpu.VMEM((heads, n, RET_DK), BF16),
                        pltpu.VMEM((heads, nc, 2 * RET_DK, RET_DV), F32),
                        pltpu.VMEM((heads, nc, 2 * RET_DK, RET_DV), BF16),
                        pltpu.VMEM((heads, 2 * RET_DK, RET_DV), F32)])
    outs = pl.pallas_call(
        kern, out_shape=out_shape, grid_spec=grid_spec, input_output_aliases=aliases,
        compiler_params=_cparams(("parallel", "parallel")),
        name="retention_lat" if rot is not None else "retention_ctx",
    )(cdec, *args)
    return outs if emit_state else outs[0]


def _rotary_tables(n):
    rows = n // GRID_W
    row = jnp.repeat(jnp.arange(rows), GRID_W).astype(F32)
    col = jnp.tile(jnp.arange(GRID_W), rows).astype(F32)
    n_freq = RET_DK // 4
    inv_freq = ROPE_BASE ** (-jnp.arange(n_freq, dtype=F32) / n_freq)
    ang = jnp.concatenate([row[:, None] * inv_freq, col[:, None] * inv_freq], axis=-1)
    cos, sin = jnp.cos(ang), jnp.sin(ang)
    return jnp.concatenate([cos, cos], axis=-1), jnp.concatenate([-sin, sin], axis=-1)


def _fourier_kernel(u_ref, cs_ref, ce_ref, se_ref, co_ref, so_ref, pint_ref, o_ref, fold_ref, *, n):
    h = n // 2
    gw = FOURIER_GROUP_W
    tc = u_ref.shape[1]
    rc = min(512, h)

    @pl.when(pl.program_id(2) == 0)
    def _():
        def fold_rows(r, carry):
            top = pl.ds(pl.multiple_of(r * rc, rc), rc)
            bot = pl.ds(pl.multiple_of(h + r * rc, rc), rc)
            for g in range(tc // gw):
                cols = pl.ds(g * gw, gw)
                zt = jnp.dot(u_ref[top, cols], cs_ref[...], preferred_element_type=F32)
                zb = jnp.dot(u_ref[bot, cols], cs_ref[...], preferred_element_type=F32)
                even, odd = zt + zb, zt - zb
                fold_ref[0, top, cols] = even[:, :gw].astype(BF16)
                fold_ref[1, top, cols] = even[:, gw:].astype(BF16)
                fold_ref[2, top, cols] = odd[:, :gw].astype(BF16)
                fold_ref[3, top, cols] = odd[:, gw:].astype(BF16)
            return carry

        if h == rc:
            fold_rows(0, 0)
        else:
            lax.fori_loop(0, h // rc, fold_rows, 0)

    ye = (jnp.dot(ce_ref[...], fold_ref[0], preferred_element_type=F32)
          + jnp.dot(se_ref[...], fold_ref[1], preferred_element_type=F32))
    yo = (jnp.dot(co_ref[...], fold_ref[2], preferred_element_type=F32)
          + jnp.dot(so_ref[...], fold_ref[3], preferred_element_type=F32))
    y = jnp.concatenate([ye, yo], axis=0).astype(BF16)
    o_ref[...] = jnp.dot(pint_ref[...], y, preferred_element_type=F32).astype(BF16)


def _fourier(u, uf_prev, *, n, nseq, row0, cs, mats, tr, tc):
    t = u.shape[0]
    h = n // 2
    assert n % tr == 0 and FOURIER_W % tc == 0 and tc % FOURIER_GROUP_W == 0 and row0 % n == 0
    rb = row0 // n
    nr = n // tr
    fb = COL_FOUR // tc
    half = pl.BlockSpec((tr // 2, h), lambda b, j, i: (i, 0))
    r = jnp.arange(tr, dtype=jnp.int32)
    pint = ((r // 2 + (r % 2) * (tr // 2))[:, None] == r[None, :]).astype(BF16)
    in_specs = [pl.BlockSpec((n, tc), lambda b, j, i: (rb + b, fb + j)),
                pl.BlockSpec(cs.shape, lambda b, j, i: (0, 0)),
                half, half, half, half,
                pl.BlockSpec((tr, tr), lambda b, j, i: (0, 0))]
    args = [u, cs, *mats, pint]
    aliases = {}
    kern = functools.partial(_fourier_kernel, n=n)
    if uf_prev is not None:
        in_specs.append(pl.BlockSpec(memory_space=pl.ANY))
        args.append(uf_prev)
        aliases = {len(args) - 1: 0}
        kern = _drop_ref(kern, len(args) - 1)
    return pl.pallas_call(
        kern,
        out_shape=jax.ShapeDtypeStruct((t, FOURIER_W), BF16),
        grid=(nseq, FOURIER_W // tc, nr),
        in_specs=in_specs,
        out_specs=pl.BlockSpec((tr, tc), lambda b, j, i: ((rb + b) * nr + i, j)),
        scratch_shapes=[pltpu.VMEM((4, h, tc), BF16)],
        input_output_aliases=aliases,
        compiler_params=_cparams(("parallel", "parallel", "arbitrary")),
        name="fourier",
    )(*args)


def _split_dft_mats(n, scale):
    h = n // 2
    assert n % 4 == 0
    jr = jnp.arange(h // 2, dtype=jnp.int32)
    m = jnp.arange(h, dtype=jnp.int32)
    sgn = jnp.where(m % 2 == 0, 1.0, -1.0).astype(F32)

    def mats(r):
        ang = ((r[:, None] * m[None, :]) % n).astype(F32) * (2.0 * jnp.pi / n)
        full = lambda top: jnp.concatenate([top, top * sgn[None, :]], axis=0).astype(BF16)
        return full(jnp.cos(ang) * scale), full(-jnp.sin(ang) * scale)

    ce, se = mats(2 * jr)
    co, so = mats(2 * jr + 1)
    return ce, se, co, so


def _merge_kernel(grp_ref, *refs, n_x, nct):
    del grp_ref
    x_refs = refs[:n_x]
    (og_ref, uf_ref, ga_ref, gb_ref, mod_ref, g2_ref, wro_ref, wf_ref, wo_ref, wr_ref, br_ref,
     xo_ref, h2_ref, lp_ref, tw_ref, cnt_ref) = refs[n_x:]
    tm = og_ref.shape[0]

    a = jnp.dot(og_ref[...], wro_ref[...], preferred_element_type=F32)
    b = jnp.dot(uf_ref[...], wf_ref[...], preferred_element_type=F32)
    merged = (jax.nn.sigmoid(ga_ref[...].astype(F32)) * a
              + jax.nn.sigmoid(gb_ref[...].astype(F32)) * b)
    y = jnp.dot(merged.astype(BF16), wo_ref[...], preferred_element_type=F32)
    m = mod_ref[0]
    x = _stream_tile(x_refs, pl.program_id(0), nct) + m[2:3] * y
    xo_ref[...] = x
    h2 = _rms_mod(x, g2_ref[...], m[3:4], m[4:5])
    h_hi = h2.astype(BF16)
    h2_ref[...] = h_hi
    h_lo = (h2 - h_hi.astype(F32)).astype(BF16)
    p_hi = jnp.dot(h_hi, wr_ref[...], preferred_element_type=F32)
    p_lo = jnp.dot(h_lo, wr_ref[...], preferred_element_type=F32)
    logits = p_hi + pltpu.roll(p_hi, E_PAD - N_EXPERTS, 1) + p_lo + br_ref[...]
    for s in range(tm // MOE_TILE):
        lp, tw, n = _route_tile(logits[s * MOE_TILE:(s + 1) * MOE_TILE])
        lp_ref[pl.ds(s * MOE_TILE, MOE_TILE), :] = lp
        tw_ref[pl.ds(s * MOE_TILE, MOE_TILE), :] = tw
        cnt_ref[s] = n


def _route_tile(logits):
    tt = logits.shape[0]
    lane = lax.broadcasted_iota(jnp.int32, (tt, E_PAD), 1)
    rest = logits
    vals, sels = [], []
    for _ in range(TOP_K):
        mx = jnp.max(rest, axis=-1, keepdims=True)
        idx = jnp.min(jnp.where(rest == mx, lane, E_PAD), axis=-1, keepdims=True)
        sel = lane == idx
        vals.append(mx)
        sels.append(sel)
        rest = jnp.where(sel, -jnp.inf, rest)
    exps = [jnp.exp(v - vals[0]) for v in vals]
    den = exps[0] + exps[1] + exps[2] + exps[3]

    picked = jnp.zeros((tt, E_PAD), F32)
    for sel in sels:
        picked = picked + sel.astype(F32)
    n = jnp.sum(picked, axis=0, keepdims=True).astype(jnp.int32)
    seg = ((n + (SEG_ALIGN - 1)) // SEG_ALIGN) * SEG_ALIGN
    er = lax.broadcasted_iota(jnp.int32, (E_PAD, E_PAD), 0)
    ec = lax.broadcasted_iota(jnp.int32, (E_PAD, E_PAD), 1)
    seg_rows = jnp.broadcast_to(seg.astype(F32), (SEG_ALIGN, E_PAD)).astype(BF16)
    off = jnp.dot(seg_rows, (er < ec).astype(BF16), preferred_element_type=F32)[0:1]
    ri = lax.broadcasted_iota(jnp.int32, (tt, tt), 0)
    ci = lax.broadcasted_iota(jnp.int32, (tt, tt), 1)
    before = jnp.dot((ri > ci).astype(BF16), picked.astype(BF16), preferred_element_type=F32) + off

    lp = jnp.zeros((tt, E_PAD), jnp.int32)
    tw = jnp.zeros((tt, E_PAD), F32)
    for k in range(TOP_K):
        row = jnp.sum(jnp.where(sels[k], before, 0.0), axis=-1, keepdims=True).astype(jnp.int32)
        lp = jnp.where(lane == k, row, lp)
        tw = jnp.where(lane == k, exps[k] / den, tw)
    return lp, tw, n


def _merge(xs, og, uf, u, tile_grp, mod, g2, wro, wf, wo, wr, br, tm):
    t, d = og.shape[0], wo.shape[1]
    assert tm % MOE_TILE == 0
    sub = tm // MOE_TILE
    ga_b, gb_b = COL_GA // d, COL_GA // d + 1
    full = lambda shape: pl.BlockSpec(shape, lambda i, grp: (0,) * len(shape))
    row = lambda w: pl.BlockSpec((tm, w), lambda i, grp: (i, 0))
    grid_spec = pltpu.PrefetchScalarGridSpec(
        num_scalar_prefetch=1,
        grid=(t // tm,),
        in_specs=_stream_specs(xs, tm, 1) + [
            row(RET_V_W), row(FOURIER_W),
            pl.BlockSpec((tm, d), lambda i, grp: (i, ga_b)),
            pl.BlockSpec((tm, d), lambda i, grp: (i, gb_b)),
            pl.BlockSpec((1, MOD_ROWS, d), lambda i, grp: (grp[i], 0, 0)),
            full((1, d)), full(wro.shape), full(wf.shape), full(wo.shape),
            full(wr.shape), full(br.shape)],
        out_specs=[row(d), row(d), row(E_PAD), row(E_PAD),
                   pl.BlockSpec((sub, 1, E_PAD), lambda i, grp: (i, 0, 0))],
    )
    return pl.pallas_call(
        functools.partial(_merge_kernel, n_x=len(xs), nct=xs[0].shape[0] // tm),
        out_shape=[jax.ShapeDtypeStruct((t, d), F32), jax.ShapeDtypeStruct((t, d), BF16),
                   jax.ShapeDtypeStruct((t, E_PAD), jnp.int32), jax.ShapeDtypeStruct((t, E_PAD), F32),
                   jax.ShapeDtypeStruct((t // MOE_TILE, 1, E_PAD), jnp.int32)],
        grid_spec=grid_spec,
        compiler_params=_cparams(("parallel",)),
        name="merge_route",
    )(tile_grp, *xs, og, uf, u, u, mod, g2.reshape(1, d), wro, wf, wo, wr, br)


def _router_operands(w_router, b_router):
    hi = w_router.astype(BF16)
    lo = (w_router - hi.astype(F32)).astype(BF16)
    wr = jnp.concatenate([hi, lo], axis=-1)
    wr = jnp.pad(wr, ((0, 0), (0, 0), (0, E_PAD - 2 * N_EXPERTS)))
    br = jnp.pad(b_router, ((0, 0), (0, E_PAD - N_EXPERTS)), constant_values=-1e30)
    return wr, br[:, None, :]


def _segment_copies(i, off_ref, seg_ref, gs_ref, local, remote, sem, to_remote):
    for e in range(N_EXPERTS):
        j = i * N_EXPERTS + e
        n = pl.multiple_of(seg_ref[j], SEG_ALIGN)

        @pl.when(n > 0)
        def _():
            lo = local.at[pl.ds(pl.multiple_of(off_ref[j], SEG_ALIGN), n)]
            re = remote.at[pl.ds(pl.multiple_of(gs_ref[j], SEG_ALIGN), n)]
            src, dst = (lo, re) if to_remote else (re, lo)
            pltpu.make_async_copy(src, dst, sem).start()


def _segment_wait(i, off_ref, seg_ref, local, remote, sem):
    j = i * N_EXPERTS + N_EXPERTS - 1
    tot = pl.multiple_of(off_ref[j] + seg_ref[j], SEG_ALIGN)
    pltpu.make_async_copy(local.at[pl.ds(0, tot)], remote.at[pl.ds(0, tot)], sem).wait()


def _local_row_iota(tt):
    return lax.broadcasted_iota(jnp.int32, (tt, SEG_ROWS), 1)


def _dispatch_kernel(off_ref, seg_ref, gs_ref, pend_ref, h_ref, lp_ref, xp_ref, xbuf_ref, zbuf_ref,
                     sem, zsem):
    i = pl.program_id(0)
    nt = pl.num_programs(0)
    slot = i % 2
    bm = zbuf_ref.shape[0]

    @pl.when(i == 0)
    def _():
        zbuf_ref[...] = jnp.zeros_like(zbuf_ref)

        def tail(e):
            start = jnp.where(e == 0, 0, pend_ref[jnp.maximum(e - 1, 0)])
            end = pend_ref[e]
            return end > start, xp_ref.at[pl.ds(pl.multiple_of(jnp.maximum(end - bm, 0), SEG_ALIGN), bm)]

        for e in range(N_EXPERTS):
            has, dst = tail(e)

            @pl.when(has)
            def _():
                pltpu.make_async_copy(zbuf_ref, dst, zsem).start()
        for e in range(N_EXPERTS):
            has, dst = tail(e)

            @pl.when(has)
            def _():
                pltpu.make_async_copy(zbuf_ref, dst, zsem).wait()

    lp = lp_ref[...]
    rid = _local_row_iota(lp.shape[0])
    hit = rid == lp[:, 0:1]
    for k in range(1, TOP_K):
        hit = hit | (rid == lp[:, k:k + 1])
    xbuf_ref[slot] = lax.dot_general(hit.astype(BF16), h_ref[...], (((0,), (0,)), ((), ())),
                                     preferred_element_type=F32)
    _segment_copies(i, off_ref, seg_ref, gs_ref, xbuf_ref.at[slot], xp_ref, sem.at[slot], True)

    @pl.when(i > 0)
    def _():
        _segment_wait(i - 1, off_ref, seg_ref, xbuf_ref.at[1 - slot], xp_ref, sem.at[1 - slot])

    @pl.when(i == nt - 1)
    def _():
        _segment_wait(i, off_ref, seg_ref, xbuf_ref.at[slot], xp_ref, sem.at[slot])


def _dispatch(h2, lp, lay):
    t, d = h2.shape
    tt = MOE_TILE
    grid_spec = pltpu.PrefetchScalarGridSpec(
        num_scalar_prefetch=4,
        grid=(t // tt,),
        in_specs=[pl.BlockSpec((tt, d), lambda i, *s: (i, 0)),
                  pl.BlockSpec((tt, E_PAD), lambda i, *s: (i, 0))],
        out_specs=pl.BlockSpec(memory_space=pl.ANY),
        scratch_shapes=[pltpu.VMEM((2, SEG_ROWS, d), F32), pltpu.VMEM((MOE_BLOCK, d), F32),
                        pltpu.SemaphoreType.DMA((2,)), pltpu.SemaphoreType.DMA],
    )
    return pl.pallas_call(
        _dispatch_kernel,
        out_shape=jax.ShapeDtypeStruct((lay["p_total"], d), F32),
        grid_spec=grid_spec,
        compiler_params=_cparams(("arbitrary",)),
        name="moe_dispatch",
    )(lay["off"], lay["seg"], lay["gseg"], lay["p_end"], h2, lp)


def _expert_kernel(be_ref, nu_ref, x_ref, w1_ref, b1_ref, w2_ref, b2_ref, y_ref, w1b_ref, w2b_ref):
    i = pl.program_id(0)
    de = w2b_ref.shape[0]
    used = i < nu_ref[0]

    @pl.when(used & ((i == 0) | (be_ref[i] != be_ref[jnp.maximum(i - 1, 0)])))
    def _():
        w1b_ref[...] = w1_ref[0, 0].astype(BF16)
        w2b_ref[...] = w2_ref[0, 0].astype(BF16)

    @pl.when(used)
    def _():
        hh = jnp.dot(x_ref[...].astype(BF16), w1b_ref[...], preferred_element_type=F32) + b1_ref[0, 0]
        gate = jnp.minimum(hh[:, :de], SWIGLU_LIMIT)
        lin = jnp.clip(hh[:, de:], -SWIGLU_LIMIT, SWIGLU_LIMIT)
        act = (lin + 1.0) * (gate * jax.nn.sigmoid(SWIGLU_ALPHA * gate))
        y_ref[...] = jnp.dot(act.astype(BF16), w2b_ref[...], preferred_element_type=F32) + b2_ref[0, 0]

    @pl.when(jnp.logical_not(used))
    def _():
        y_ref[...] = jnp.zeros_like(y_ref)


def _experts(xp, lay, layer, w1, b1, w2, b2):
    p_total, d = xp.shape
    depth, ne, _, de2 = w1.shape
    de = w2.shape[2]
    bm = MOE_BLOCK
    wmap = lambda i, be, nu: (layer, be[i], 0, 0)
    grid_spec = pltpu.PrefetchScalarGridSpec(
        num_scalar_prefetch=2,
        grid=(p_total // bm,),
        in_specs=[pl.BlockSpec((bm, d), lambda i, be, nu: (jnp.minimum(i, nu[0] - 1), 0)),
                  pl.BlockSpec((1, 1, d, de2), wmap),
                  pl.BlockSpec((1, 1, 1, de2), wmap),
                  pl.BlockSpec((1, 1, de, d), wmap),
                  pl.BlockSpec((1, 1, 1, d), wmap)],
        out_specs=pl.BlockSpec((bm, d), lambda i, be, nu: (i, 0)),
        scratch_shapes=[pltpu.VMEM((d, de2), BF16), pltpu.VMEM((de, d), BF16)],
    )
    return pl.pallas_call(
        _expert_kernel,
        out_shape=jax.ShapeDtypeStruct((p_total, d), F32),
        grid_spec=grid_spec,
        compiler_params=_cparams(("arbitrary",)),
        name="moe_experts",
    )(lay["blk_e"], lay["n_used"], xp, w1, b1.reshape(depth, ne, 1, de2), w2, b2.reshape(depth, ne, 1, d))


def _combine_kernel(grp_ref, off_ref, seg_ref, gs_ref, lp_ref, x_ref, tw_ref, mod_ref, fg_ref, yb_ref,
                    *refs, split_out, nct):
    del grp_ref
    o_refs = refs[:2] if split_out else refs[:1]
    ybuf_ref, sem = refs[len(o_refs):]
    i = pl.program_id(0)
    nt = pl.num_programs(0)
    slot = i % 2

    def gather(tile, s):
        _segment_copies(tile, off_ref, seg_ref, gs_ref, ybuf_ref.at[s], yb_ref, sem.at[s], False)

    @pl.when(i == 0)
    def _():
        ybuf_ref[...] = jnp.zeros_like(ybuf_ref)
        gather(0, 0)

    @pl.when(i + 1 < nt)
    def _():
        gather(i + 1, 1 - slot)

    _segment_wait(i, off_ref, seg_ref, ybuf_ref.at[slot], yb_ref, sem.at[slot])
    lp = lp_ref[...]
    tw = tw_ref[...]
    rid = _local_row_iota(lp.shape[0])
    wsel = jnp.where(rid == lp[:, 0:1], tw[:, 0:1], 0.0)
    for k in range(1, TOP_K):
        wsel = jnp.where(rid == lp[:, k:k + 1], tw[:, k:k + 1], wsel)
    acc = jnp.dot(wsel.astype(BF16), ybuf_ref[slot].astype(BF16), preferred_element_type=F32)
    x = x_ref[...] + mod_ref[0][5:6] * acc
    if not split_out:
        o_refs[0][...] = x
    else:
        y = x * lax.rsqrt(jnp.mean(x * x, axis=-1, keepdims=True) + NORM_EPS) * fg_ref[...]

        @pl.when(i < nct)
        def _():
            o_refs[0][...] = y

        @pl.when(i >= nct)
        def _():
            o_refs[1][...] = y


def _combine(x, yb, lp, tw, lay, tile_grp, mod, final_g, t_ctx, split_out):
    t, d = x.shape
    tt = MOE_TILE
    nct = t_ctx // tt
    if split_out:
        out_shape = [jax.ShapeDtypeStruct((t_ctx, d), F32), jax.ShapeDtypeStruct((t - t_ctx, d), F32)]
        out_specs = [pl.BlockSpec((tt, d), lambda i, *s: (jnp.minimum(i, nct - 1), 0)),
                     pl.BlockSpec((tt, d), lambda i, *s: (jnp.maximum(i - nct, 0), 0))]
    else:
        out_shape = [jax.ShapeDtypeStruct((t, d), F32)]
        out_specs = [pl.BlockSpec((tt, d), lambda i, *s: (i, 0))]
    grid_spec = pltpu.PrefetchScalarGridSpec(
        num_scalar_prefetch=4,
        grid=(t // tt,),
        in_specs=[pl.BlockSpec((tt, E_PAD), lambda i, *s: (i, 0)),
                  pl.BlockSpec((tt, d), lambda i, *s: (i, 0)),
                  pl.BlockSpec((tt, E_PAD), lambda i, *s: (i, 0)),
                  pl.BlockSpec((1, MOD_ROWS, d), lambda i, grp, *s: (grp[i], 0, 0)),
                  pl.BlockSpec((1, d), lambda i, *s: (0, 0)),
                  pl.BlockSpec(memory_space=pl.ANY)],
        out_specs=out_specs,
        scratch_shapes=[pltpu.VMEM((2, SEG_ROWS, d), F32), pltpu.SemaphoreType.DMA((2,))],
    )
    return pl.pallas_call(
        functools.partial(_combine_kernel, split_out=split_out, nct=nct),
        out_shape=out_shape,
        grid_spec=grid_spec,
        compiler_params=_cparams(("arbitrary",)),
        name="moe_combine",
    )(tile_grp, lay["off"], lay["seg"], lay["gseg"], lp, x, tw, mod, final_g.reshape(1, d), yb)


def _segment_layout(cnt, t):
    bm = MOE_BLOCK
    nt = t // MOE_TILE
    n = cnt[:, 0, :N_EXPERTS]
    seg = ((n + (SEG_ALIGN - 1)) // SEG_ALIGN) * SEG_ALIGN
    off = jnp.cumsum(seg, axis=1) - seg
    tot = jnp.sum(seg, axis=0)
    padded = ((tot + bm - 1) // bm) * bm
    p_end = jnp.cumsum(padded)
    gseg = (p_end - padded)[None, :] + jnp.cumsum(seg, axis=0) - seg
    max_rows = t * TOP_K + nt * N_EXPERTS * (SEG_ALIGN - 1)
    n_blk = -(-max_rows // bm) + N_EXPERTS
    blk_start = jnp.arange(n_blk, dtype=jnp.int32) * bm
    blk_e = jnp.minimum(jnp.sum(p_end[None, :] <= blk_start[:, None], axis=1), N_EXPERTS - 1)
    i32 = lambda a: a.reshape(-1).astype(jnp.int32)
    return dict(seg=i32(seg), off=i32(off), gseg=i32(gseg), p_end=i32(p_end), blk_e=i32(blk_e),
                n_used=i32(p_end[-1:] // bm), p_total=n_blk * bm)


def _tile_groups(t, t_ctx, dec_seq, tm):
    starts = jnp.arange(t // tm, dtype=jnp.int32) * tm
    return jnp.where(starts < t_ctx, 0, 1 + (starts - t_ctx) // dec_seq).astype(jnp.int32)


def kernel(x_prompt, x_sample, state_ret, c, c_ctx, w_ada, b_ada, norm1_g, norm2_g, w_in, ret_decay_logit, w_ret_o, w_four, w_out, w_router, b_router, w1, b1, w2, b2, final_g):
    batch, seq, d = x_prompt.shape
    dec_batch, dec_seq, _ = x_sample.shape
    depth = w_in.shape[0]
    t_ctx, t_lat = batch * seq, dec_batch * dec_seq
    t = t_ctx + t_lat
    tm_in = min(1024, t_ctx, dec_seq)
    tm = min(512, t_ctx, dec_seq)
    assert t_ctx % tm_in == 0 and dec_seq % tm_in == 0 and t_ctx % dec_seq == 0

    n_cond = 16
    cond = jnp.zeros((n_cond, d), F32).at[0].set(c_ctx).at[1:1 + dec_batch].set(c)
    mod = _adaln_mod(cond, w_ada, b_ada).reshape(depth, n_cond, 6, d)
    mod = jnp.pad(mod, ((0, 0), (0, 0), (0, MOD_ROWS - 6), (0, 0)))
    grp_in = _tile_groups(t, t_ctx, dec_seq, tm_in)
    grp = _tile_groups(t, t_ctx, dec_seq, tm)
    grp_moe = _tile_groups(t, t_ctx, dec_seq, MOE_TILE)

    rot = _rotary_tables(dec_seq)
    gw = FOURIER_GROUP_W
    jc = jnp.arange(gw, dtype=jnp.int32)
    ang_c = ((jc[:, None] * jc[None, :]) % gw).astype(F32) * (2.0 * jnp.pi / gw)
    cs = jnp.concatenate([jnp.cos(ang_c), jnp.sin(ang_c)], axis=1).astype(BF16)
    dm_ctx = _split_dft_mats(seq, (seq * gw) ** -0.5)
    dm_lat = _split_dft_mats(dec_seq, (dec_seq * gw) ** -0.5)
    log_g = jax.nn.log_sigmoid(ret_decay_logit.astype(F32))
    wr, br = _router_operands(w_router, b_router)

    xs = [x_prompt.reshape(t_ctx, d), x_sample.reshape(t_lat, d)]
    states = []
    for l in range(depth):
        last = l == depth - 1
        u = _in_proj(xs, grp_in, mod[l], norm1_g[l], w_in[l].astype(BF16), tm_in, 1536)
        tables = _decay_tables(log_g[l], min(RET_CHUNK, seq, dec_seq))
        og, st = _retention(u, None, n=seq, nseq=batch, row0=0, tables=tables, rot=None, s0=None,
                            layer=l, emit_state=True, heads=N_RET_HEADS)
        og = _retention(u, og, n=dec_seq, nseq=dec_batch, row0=t_ctx, tables=tables, rot=rot,
                        s0=state_ret, layer=l, emit_state=False, heads=1)
        states.append(st)
        uf = _fourier(u, None, n=seq, nseq=batch, row0=0, cs=cs, mats=dm_ctx, tr=seq, tc=FOURIER_W)
        uf = _fourier(u, uf, n=dec_seq, nseq=dec_batch, row0=t_ctx, cs=cs, mats=dm_lat,
                      tr=min(512, dec_seq), tc=512)
        x, h2, lp, tw, cnt = _merge(
            xs, og, uf, u, grp, mod[l], norm2_g[l], w_ret_o[l].astype(BF16), w_four[l].astype(BF16),
            w_out[l].astype(BF16), wr[l], br[l], tm)
        lay = _segment_layout(cnt, t)
        xp = _dispatch(h2, lp, lay)
        yb = _experts(xp, lay, l, w1, b1, w2, b2)
        xs = _combine(x, yb, lp, tw, lay, grp_moe, mod[l], final_g, t_ctx, split_out=last)

    new_state = jnp.stack(states, axis=1).astype(x_prompt.dtype)
    return (xs[0].reshape(batch, seq, d), xs[1].reshape(dec_batch, dec_seq, d), new_state)
```

```python
import functools

import jax
import jax.numpy as jnp
from jax import lax
from jax.experimental import pallas as pl
from jax.experimental.pallas import tpu as pltpu

F32 = jnp.float32
BF16 = jnp.bfloat16

N_RET_HEADS = 8
RET_DK = 128
RET_DV = 256
RET_QK_W = N_RET_HEADS * RET_DK
RET_V_W = N_RET_HEADS * RET_DV
N_FOURIER_GROUPS = 4
FOURIER_GROUP_W = 256
FOURIER_W = N_FOURIER_GROUPS * FOURIER_GROUP_W
GRID_W = 64
N_EXPERTS = 32
TOP_K = 4
SWIGLU_LIMIT = 7.0
SWIGLU_ALPHA = 1.702
ROPE_BASE = 10000.0
NORM_EPS = 1e-6
GN_EPS = 1e-5

COL_Q = 0
COL_K = COL_Q + RET_QK_W
COL_V = COL_K + RET_QK_W
COL_GRET = COL_V + RET_V_W
COL_FOUR = COL_GRET + RET_V_W
COL_GA = COL_FOUR + FOURIER_W

LANES = 128
VMEM_LIMIT_BYTES = 56 * 1024 * 1024

MOD_ROWS = 8
RET_CHUNK = 256
MOE_BLOCK = 512
MOE_TILE = 256
SEG_ALIGN = 8
SEG_ROWS = MOE_TILE * TOP_K + N_EXPERTS * SEG_ALIGN
E_PAD = LANES


def _cparams(sem, vmem=VMEM_LIMIT_BYTES):
    return pltpu.CompilerParams(dimension_semantics=sem, vmem_limit_bytes=vmem)


def _drop_ref(kern, idx):
    def wrapped(*refs):
        return kern(*refs[:idx], *refs[idx + 1:])
    return wrapped


def _mod_kernel(c_ref, w_ref, b_ref, o_ref):
    c = c_ref[...]
    s = c * jax.nn.sigmoid(c)
    o_ref[0] = jnp.dot(s, w_ref[0], precision=lax.Precision.HIGHEST,
                       preferred_element_type=F32) + b_ref[0]


def _adaln_mod(cond, w_ada, b_ada):
    depth, d, w6 = w_ada.shape
    r = cond.shape[0]
    tn = 1536
    assert w6 % tn == 0
    return pl.pallas_call(
        _mod_kernel,
        out_shape=jax.ShapeDtypeStruct((depth, r, w6), F32),
        grid=(depth, w6 // tn),
        in_specs=[pl.BlockSpec((r, d), lambda l, j: (0, 0)),
                  pl.BlockSpec((1, d, tn), lambda l, j: (l, 0, j)),
                  pl.BlockSpec((1, 1, tn), lambda l, j: (l, 0, j))],
        out_specs=pl.BlockSpec((1, r, tn), lambda l, j: (l, 0, j)),
        compiler_params=_cparams(("parallel", "parallel")),
        name="adaln_mod",
    )(cond, w_ada, b_ada.reshape(depth, 1, w6))


def _stream_specs(xs, tm, imap_args):
    d = xs[0].shape[1]
    if len(xs) == 1:
        return [pl.BlockSpec((tm, d), lambda i, *a: (i, 0))]
    nct = xs[0].shape[0] // tm
    return [pl.BlockSpec((tm, d), lambda i, *a: (jnp.minimum(i, nct - 1), 0)),
            pl.BlockSpec((tm, d), lambda i, *a: (jnp.maximum(i - nct, 0), 0))]


def _stream_tile(x_refs, i, nct):
    if len(x_refs) == 1:
        return x_refs[0][...]
    return jnp.where(i < nct, x_refs[0][...], x_refs[1][...])


def _rms_mod(x, g, shift, scale):
    y = x * lax.rsqrt(jnp.mean(x * x, axis=-1, keepdims=True) + NORM_EPS) * g
    return y * (1.0 + scale) + shift


def _in_proj_kernel(grp_ref, *refs, n_x, nct):
    del grp_ref
    x_refs = refs[:n_x]
    mod_ref, g_ref, w_ref, o_ref, h_ref = refs[n_x:]

    @pl.when(pl.program_id(1) == 0)
    def _():
        m = mod_ref[0]
        x = _stream_tile(x_refs, pl.program_id(0), nct)
        h_ref[...] = _rms_mod(x, g_ref[...], m[0:1], m[1:2]).astype(BF16)

    o_ref[...] = jnp.dot(h_ref[...], w_ref[...], preferred_element_type=F32).astype(BF16)


def _in_proj(xs, tile_grp, mod, g, w, tm, tn):
    t = sum(x.shape[0] for x in xs)
    d, n = w.shape
    assert t % tm == 0 and n % tn == 0 and all(x.shape[0] % tm == 0 for x in xs)
    x_specs = [pl.BlockSpec(s.block_shape, (lambda f: lambda i, j, grp: f(i))(s.index_map))
               for s in _stream_specs(xs, tm, 1)]
    grid_spec = pltpu.PrefetchScalarGridSpec(
        num_scalar_prefetch=1,
        grid=(t // tm, n // tn),
        in_specs=x_specs + [pl.BlockSpec((1, MOD_ROWS, d), lambda i, j, grp: (grp[i], 0, 0)),
                            pl.BlockSpec((1, d), lambda i, j, grp: (0, 0)),
                            pl.BlockSpec((d, tn), lambda i, j, grp: (0, j))],
        out_specs=pl.BlockSpec((tm, tn), lambda i, j, grp: (i, j)),
        scratch_shapes=[pltpu.VMEM((tm, d), BF16)],
    )
    return pl.pallas_call(
        functools.partial(_in_proj_kernel, n_x=len(xs), nct=xs[0].shape[0] // tm),
        out_shape=jax.ShapeDtypeStruct((t, n), BF16),
        grid_spec=grid_spec,
        compiler_params=_cparams(("parallel", "arbitrary")),
        name="in_proj",
    )(tile_grp, *xs, mod, g.reshape(1, d), w)


def _retention_kernel(cdec_ref, *refs, n, chunk, rotary, has_s0, emit_state, heads):
    for hh in range(heads):
        _retention_head(cdec_ref, refs, hh, n=n, chunk=chunk, rotary=rotary, has_s0=has_s0,
                        emit_state=emit_state, heads=heads)


def _retention_head(cdec_ref, refs, hh, *, n, chunk, rotary, has_s0, emit_state, heads):
    refs = list(refs)
    q_ref, k_ref, v_ref, g_ref, dmask_ref, dvec_ref, kdec_ref = refs[:7]
    pos = 7
    if rotary:
        cos_ref, sin_ref = refs[pos:pos + 2]
        pos += 2
    if has_s0:
        s0_ref = refs[pos]
        pos += 1
    o_ref = refs[pos]
    pos += 1
    if emit_state:
        sn_ref = refs[pos]
        pos += 1
    qs_ref, ks_ref, kv_ref, st_ref, s_ref = [ref.at[hh] for ref in refs[pos:pos + 5]]
    qcols = pl.ds(hh * RET_DK, RET_DK)
    vcols = pl.ds(hh * RET_DV, RET_DV)

    nc = n // chunk
    cross = has_s0 or nc > 1
    unroll = 16 if nc % 16 == 0 else (2 if nc % 2 == 0 else 1)
    h = pl.program_id(0) * heads + hh
    cdec_f = cdec_ref[2 * h]
    cdec_b = cdec_ref[2 * h + 1]
    fwd = pl.ds(0, RET_DK)
    bwd = pl.ds(RET_DK, RET_DK)

    def rows(c):
        return pl.ds(pl.multiple_of(c * chunk, chunk), chunk)

    def prep(c, carry):
        r = rows(c)
        q = q_ref[r, qcols].astype(F32) * (RET_DK ** -0.5)
        k = k_ref[r, qcols].astype(F32)
        if rotary:
            cs, sn = cos_ref[r, :], sin_ref[r, :]
            q = q * cs + pltpu.roll(q, RET_DK // 2, 1) * sn
            k = k * cs + pltpu.roll(k, RET_DK // 2, 1) * sn
        qs_ref[r, :] = q.astype(BF16)
        kt = k.T.astype(BF16)
        ks_ref[c] = kt
        ktf = kt.astype(F32)
        kdec = kdec_ref[hh]
        kd = jnp.concatenate([(ktf * kdec[0:1]).astype(BF16), (ktf * kdec[1:2]).astype(BF16)], axis=0)
        kv_ref[c] = jnp.dot(kd, v_ref[r, vcols], preferred_element_type=F32)
        return carry

    def scan(i, carry):
        cb = nc - 1 - i
        st_ref[i, fwd, :] = s_ref[fwd, :].astype(BF16)
        s_ref[fwd, :] = s_ref[fwd, :] * cdec_f + kv_ref[i, fwd, :]
        st_ref[cb, bwd, :] = s_ref[bwd, :].astype(BF16)
        s_ref[bwd, :] = s_ref[bwd, :] * cdec_b + kv_ref[cb, bwd, :]
        return carry

    def out_step(c, carry):
        r = rows(c)
        qc = qs_ref[r, :]
        sc = jnp.dot(qc, ks_ref[c], preferred_element_type=F32)
        o = jnp.dot((sc * dmask_ref[hh]).astype(BF16), v_ref[r, vcols], preferred_element_type=F32)
        if cross:
            qf = qc.astype(F32)
            ql = jnp.concatenate([(qf * dvec_ref[hh, 0]).astype(BF16),
                                  (qf * dvec_ref[hh, 2]).astype(BF16)], axis=1)
            o = o + jnp.dot(ql, st_ref[c], preferred_element_type=F32)
        mu = jnp.mean(o, axis=-1, keepdims=True)
        oc = o - mu
        var = jnp.mean(oc * oc, axis=-1, keepdims=True)
        gt = g_ref[r, vcols]
        o_ref[r, vcols] = (gt * jax.nn.sigmoid(gt)) * (oc * lax.rsqrt(var + GN_EPS)).astype(BF16)
        return carry

    def loop(body):
        if nc == 1:
            body(0, 0)
        else:
            lax.fori_loop(0, nc, body, 0, unroll=unroll)

    if has_s0:
        s_ref[fwd, :] = s0_ref[0, 0, 0, hh]
        s_ref[bwd, :] = s0_ref[0, 0, 1, hh]
    else:
        s_ref[...] = jnp.zeros_like(s_ref)
    loop(prep)
    if nc == 1:
        scan(0, 0)
    else:
        lax.fori_loop(0, nc, scan, 0, unroll=unroll)
    if emit_state:
        sn_ref[0, 0, hh] = s_ref[fwd, :]
        sn_ref[0, 1, hh] = s_ref[bwd, :]
    loop(out_step)


def _decay_tables(log_g, chunk):
    i = jnp.arange(chunk, dtype=F32)
    diff = i[:, None] - i[None, :]
    lf = log_g[0][:, None, None]
    lb = log_g[1][:, None, None]
    dmask = (jnp.where(diff >= 0, jnp.exp(jnp.maximum(diff, 0.0) * lf), 0.0)
             + jnp.where(diff <= 0, jnp.exp(jnp.maximum(-diff, 0.0) * lb), 0.0))
    lf2, lb2 = log_g[0][:, None], log_g[1][:, None]
    vecs = jnp.stack([jnp.exp((i + 1.0)[None, :] * lf2),
                      jnp.exp((chunk - 1.0 - i)[None, :] * lf2),
                      jnp.exp((chunk - i)[None, :] * lb2),
                      jnp.exp(i[None, :] * lb2)], axis=1)
    dvec = jnp.broadcast_to(vecs[..., None], vecs.shape + (LANES,))
    kdec = jnp.pad(vecs[:, 1::2], ((0, 0), (0, MOD_ROWS - 2), (0, 0)))
    cdec = jnp.exp(chunk * log_g).T.reshape(-1)
    return dmask, dvec, kdec, cdec


def _retention(u, og_prev, *, n, nseq, row0, tables, rot, s0, layer, emit_state, heads):
    t = u.shape[0]
    chunk = min(RET_CHUNK, n)
    assert n % chunk == 0 and row0 % n == 0 and N_RET_HEADS % heads == 0
    rb = row0 // n
    dmask, dvec, kdec, cdec = tables
    nc = n // chunk
    kw, vw = heads * RET_DK, heads * RET_DV
    qb, vb, gb = COL_K // kw, COL_V // vw, COL_GRET // vw
    in_specs = [pl.BlockSpec((n, kw), lambda h, b, cd: (rb + b, h)),
                pl.BlockSpec((n, kw), lambda h, b, cd: (rb + b, qb + h)),
                pl.BlockSpec((n, vw), lambda h, b, cd: (rb + b, vb + h)),
                pl.BlockSpec((n, vw), lambda h, b, cd: (rb + b, gb + h)),
                pl.BlockSpec((heads, chunk, chunk), lambda h, b, cd: (h, 0, 0)),
                pl.BlockSpec((heads, 4, chunk, LANES), lambda h, b, cd: (h, 0, 0, 0)),
                pl.BlockSpec((heads, MOD_ROWS, chunk), lambda h, b, cd: (h, 0, 0))]
    args = [u, u, u, u, dmask, dvec, kdec]
    if rot is not None:
        in_specs += [pl.BlockSpec((n, RET_DK), lambda h, b, cd: (0, 0))] * 2
        args += list(rot)
    if s0 is not None:
        in_specs.append(pl.BlockSpec((1, 1, 2, heads, RET_DK, RET_DV),
                                     lambda h, b, cd: (b, layer, 0, h, 0, 0)))
        args.append(s0)
    out_shape = [jax.ShapeDtypeStruct((t, RET_V_W), BF16)]
    out_specs = [pl.BlockSpec((n, vw), lambda h, b, cd: (rb + b, h))]
    if emit_state:
        out_shape.append(jax.ShapeDtypeStruct((nseq, 2, N_RET_HEADS, RET_DK, RET_DV), F32))
        out_specs.append(pl.BlockSpec((1, 2, heads, RET_DK, RET_DV), lambda h, b, cd: (b, 0, h, 0, 0)))
    kern = functools.partial(_retention_kernel, n=n, chunk=chunk, rotary=rot is not None,
                             has_s0=s0 is not None, emit_state=emit_state, heads=heads)
    aliases = {}
    if og_prev is not None:
        in_specs.append(pl.BlockSpec(memory_space=pl.ANY))
        args.append(og_prev)
        aliases = {len(args): 0}
        kern = _drop_ref(kern, len(args))
    grid_spec = pltpu.PrefetchScalarGridSpec(
        num_scalar_prefetch=1, grid=(N_RET_HEADS // heads, nseq), in_specs=in_specs, out_specs=out_specs,
        scratch_shapes=[pltpu.VMEM((heads, n, RET_DK), BF16), pltpu.VMEM((heads, nc, RET_DK, chunk), BF16),
                        pltpu.VMEM((heads, nc, 2 * RET_DK, RET_DV), F32),
                        pltpu.VMEM((heads, nc, 2 * RET_DK, RET_DV), BF16),
                        pltpu.VMEM((heads, 2 * RET_DK, RET_DV), F32)])
    outs = pl.pallas_call(
        kern, out_shape=out_shape, grid_spec=grid_spec, input_output_aliases=aliases,
        compiler_params=_cparams(("parallel", "parallel")),
        name="retention_lat" if rot is not None else "retention_ctx",
    )(cdec, *args)
    return outs if emit_state else outs[0]


def _rotary_tables(n):
    rows = n // GRID_W
    row = jnp.repeat(jnp.arange(rows), GRID_W).astype(F32)
    col = jnp.tile(jnp.arange(GRID_W), rows).astype(F32)
    n_freq = RET_DK // 4
    inv_freq = ROPE_BASE ** (-jnp.arange(n_freq, dtype=F32) / n_freq)
    ang = jnp.concatenate([row[:, None] * inv_freq, col[:, None] * inv_freq], axis=-1)
    cos, sin = jnp.cos(ang), jnp.sin(ang)
    return jnp.concatenate([cos, cos], axis=-1), jnp.concatenate([-sin, sin], axis=-1)


def _fourier_kernel(u_ref, cs_ref, ce_ref, se_ref, co_ref, so_ref, pint_ref, o_ref, fold_ref, *, n):
    h = n // 2
    gw = FOURIER_GROUP_W
    tc = u_ref.shape[1]
    rc = min(512, h)

    @pl.when(pl.program_id(2) == 0)
    def _():
        def fold_rows(r, carry):
            top = pl.ds(pl.multiple_of(r * rc, rc), rc)
            bot = pl.ds(pl.multiple_of(h + r * rc, rc), rc)
            for g in range(tc // gw):
                cols = pl.ds(g * gw, gw)
                zt = jnp.dot(u_ref[top, cols], cs_ref[...], preferred_element_type=F32)
                zb = jnp.dot(u_ref[bot, cols], cs_ref[...], preferred_element_type=F32)
                even, odd = zt + zb, zt - zb
                fold_ref[0, top, cols] = even[:, :gw].astype(BF16)
                fold_ref[1, top, cols] = even[:, gw:].astype(BF16)
                fold_ref[2, top, cols] = odd[:, :gw].astype(BF16)
                fold_ref[3, top, cols] = odd[:, gw:].astype(BF16)
            return carry

        if h == rc:
            fold_rows(0, 0)
        else:
            lax.fori_loop(0, h // rc, fold_rows, 0)

    ye = (jnp.dot(ce_ref[...], fold_ref[0], preferred_element_type=F32)
          + jnp.dot(se_ref[...], fold_ref[1], preferred_element_type=F32))
    yo = (jnp.dot(co_ref[...], fold_ref[2], preferred_element_type=F32)
          + jnp.dot(so_ref[...], fold_ref[3], preferred_element_type=F32))
    y = jnp.concatenate([ye, yo], axis=0).astype(BF16)
    o_ref[...] = jnp.dot(pint_ref[...], y, preferred_element_type=F32).astype(BF16)


def _fourier(u, uf_prev, *, n, nseq, row0, cs, mats, tr, tc):
    t = u.shape[0]
    h = n // 2
    assert n % tr == 0 and FOURIER_W % tc == 0 and tc % FOURIER_GROUP_W == 0 and row0 % n == 0
    rb = row0 // n
    nr = n // tr
    fb = COL_FOUR // tc
    half = pl.BlockSpec((tr // 2, h), lambda b, j, i: (i, 0))
    r = jnp.arange(tr, dtype=jnp.int32)
    pint = ((r // 2 + (r % 2) * (tr // 2))[:, None] == r[None, :]).astype(BF16)
    in_specs = [pl.BlockSpec((n, tc), lambda b, j, i: (rb + b, fb + j)),
                pl.BlockSpec(cs.shape, lambda b, j, i: (0, 0)),
                half, half, half, half,
                pl.BlockSpec((tr, tr), lambda b, j, i: (0, 0))]
    args = [u, cs, *mats, pint]
    aliases = {}
    kern = functools.partial(_fourier_kernel, n=n)
    if uf_prev is not None:
        in_specs.append(pl.BlockSpec(memory_space=pl.ANY))
        args.append(uf_prev)
        aliases = {len(args) - 1: 0}
        kern = _drop_ref(kern, len(args) - 1)
    return pl.pallas_call(
        kern,
        out_shape=jax.ShapeDtypeStruct((t, FOURIER_W), BF16),
        grid=(nseq, FOURIER_W // tc, nr),
        in_specs=in_specs,
        out_specs=pl.BlockSpec((tr, tc), lambda b, j, i: ((rb + b) * nr + i, j)),
        scratch_shapes=[pltpu.VMEM((4, h, tc), BF16)],
        input_output_aliases=aliases,
        compiler_params=_cparams(("parallel", "parallel", "arbitrary")),
        name="fourier",
    )(*args)


def _split_dft_mats(n, scale):
    h = n // 2
    assert n % 4 == 0
    jr = jnp.arange(h // 2, dtype=jnp.int32)
    m = jnp.arange(h, dtype=jnp.int32)
    sgn = jnp.where(m % 2 == 0, 1.0, -1.0).astype(F32)

    def mats(r):
        ang = ((r[:, None] * m[None, :]) % n).astype(F32) * (2.0 * jnp.pi / n)
        halves = jnp.stack([jnp.ones_like(sgn), sgn])[:, None, :]
        full = lambda top: (top[None] * halves).reshape(h, h).astype(BF16)
        return full(jnp.cos(ang) * scale), full(-jnp.sin(ang) * scale)

    ce, se = mats(2 * jr)
    co, so = mats(2 * jr + 1)
    return ce, se, co, so


def _merge_kernel(grp_ref, *refs, n_x, nct):
    del grp_ref
    x_refs = refs[:n_x]
    (og_ref, uf_ref, ga_ref, gb_ref, mod_ref, g2_ref, wro_ref, wf_ref, wo_ref, wr_ref, br_ref,
     xo_ref, h2_ref, lp_ref, tw_ref, cnt_ref) = refs[n_x:]
    tm = og_ref.shape[0]

    a = jnp.dot(og_ref[...], wro_ref[...], preferred_element_type=F32)
    b = jnp.dot(uf_ref[...], wf_ref[...], preferred_element_type=F32)
    merged = (jax.nn.sigmoid(ga_ref[...].astype(F32)) * a
              + jax.nn.sigmoid(gb_ref[...].astype(F32)) * b)
    y = jnp.dot(merged.astype(BF16), wo_ref[...], preferred_element_type=F32)
    m = mod_ref[0]
    x = _stream_tile(x_refs, pl.program_id(0), nct) + m[2:3] * y
    xo_ref[...] = x
    h2 = _rms_mod(x, g2_ref[...], m[3:4], m[4:5])
    h_hi = h2.astype(BF16)
    h2_ref[...] = h_hi
    h_lo = (h2 - h_hi.astype(F32)).astype(BF16)
    p_hi = jnp.dot(h_hi, wr_ref[...], preferred_element_type=F32)
    p_lo = jnp.dot(h_lo, wr_ref[...], preferred_element_type=F32)
    logits = p_hi + pltpu.roll(p_hi, E_PAD - N_EXPERTS, 1) + p_lo + br_ref[...]
    for s in range(tm // MOE_TILE):
        lp, tw, n = _route_tile(logits[s * MOE_TILE:(s + 1) * MOE_TILE])
        lp_ref[pl.ds(s * MOE_TILE, MOE_TILE), :] = lp
        tw_ref[pl.ds(s * MOE_TILE, MOE_TILE), :] = tw
        cnt_ref[s] = n


def _route_tile(logits):
    tt = logits.shape[0]
    lane = lax.broadcasted_iota(jnp.int32, (tt, E_PAD), 1)
    rest = logits
    vals, sels = [], []
    for _ in range(TOP_K):
        mx = jnp.max(rest, axis=-1, keepdims=True)
        idx = jnp.min(jnp.where(rest == mx, lane, E_PAD), axis=-1, keepdims=True)
        sel = lane == idx
        vals.append(mx)
        sels.append(sel)
        rest = jnp.where(sel, -jnp.inf, rest)
    exps = [jnp.exp(v - vals[0]) for v in vals]
    den = exps[0] + exps[1] + exps[2] + exps[3]

    picked = jnp.zeros((tt, E_PAD), F32)
    for sel in sels:
        picked = picked + sel.astype(F32)
    n = jnp.sum(picked, axis=0, keepdims=True).astype(jnp.int32)
    seg = ((n + (SEG_ALIGN - 1)) // SEG_ALIGN) * SEG_ALIGN
    er = lax.broadcasted_iota(jnp.int32, (E_PAD, E_PAD), 0)
    ec = lax.broadcasted_iota(jnp.int32, (E_PAD, E_PAD), 1)
    seg_rows = jnp.broadcast_to(seg.astype(F32), (SEG_ALIGN, E_PAD)).astype(BF16)
    off = jnp.dot(seg_rows, (er < ec).astype(BF16), preferred_element_type=F32)[0:1]
    ri = lax.broadcasted_iota(jnp.int32, (tt, tt), 0)
    ci = lax.broadcasted_iota(jnp.int32, (tt, tt), 1)
    before = jnp.dot((ri > ci).astype(BF16), picked.astype(BF16), preferred_element_type=F32) + off

    lp = jnp.zeros((tt, E_PAD), jnp.int32)
    tw = jnp.zeros((tt, E_PAD), F32)
    for k in range(TOP_K):
        row = jnp.sum(jnp.where(sels[k], before, 0.0), axis=-1, keepdims=True).astype(jnp.int32)
        lp = jnp.where(lane == k, row, lp)
        tw = jnp.where(lane == k, exps[k] / den, tw)
    return lp, tw, n


def _merge(xs, og, uf, u, tile_grp, mod, g2, wro, wf, wo, wr, br, tm):
    t, d = og.shape[0], wo.shape[1]
    assert tm % MOE_TILE == 0
    sub = tm // MOE_TILE
    ga_b, gb_b = COL_GA // d, COL_GA // d + 1
    full = lambda shape: pl.BlockSpec(shape, lambda i, grp: (0,) * len(shape))
    row = lambda w: pl.BlockSpec((tm, w), lambda i, grp: (i, 0))
    grid_spec = pltpu.PrefetchScalarGridSpec(
        num_scalar_prefetch=1,
        grid=(t // tm,),
        in_specs=_stream_specs(xs, tm, 1) + [
            row(RET_V_W), row(FOURIER_W),
            pl.BlockSpec((tm, d), lambda i, grp: (i, ga_b)),
            pl.BlockSpec((tm, d), lambda i, grp: (i, gb_b)),
            pl.BlockSpec((1, MOD_ROWS, d), lambda i, grp: (grp[i], 0, 0)),
            full((1, d)), full(wro.shape), full(wf.shape), full(wo.shape),
            full(wr.shape), full(br.shape)],
        out_specs=[row(d), row(d), row(E_PAD), row(E_PAD),
                   pl.BlockSpec((sub, 1, E_PAD), lambda i, grp: (i, 0, 0))],
    )
    return pl.pallas_call(
        functools.partial(_merge_kernel, n_x=len(xs), nct=xs[0].shape[0] // tm),
        out_shape=[jax.ShapeDtypeStruct((t, d), F32), jax.ShapeDtypeStruct((t, d), BF16),
                   jax.ShapeDtypeStruct((t, E_PAD), jnp.int32), jax.ShapeDtypeStruct((t, E_PAD), F32),
                   jax.ShapeDtypeStruct((t // MOE_TILE, 1, E_PAD), jnp.int32)],
        grid_spec=grid_spec,
        compiler_params=_cparams(("parallel",)),
        name="merge_route",
    )(tile_grp, *xs, og, uf, u, u, mod, g2.reshape(1, d), wro, wf, wo, wr, br)


def _router_operands(w_router, b_router):
    hi = w_router.astype(BF16)
    lo = (w_router - hi.astype(F32)).astype(BF16)
    wr = jnp.concatenate([hi, lo], axis=-1)
    wr = jnp.pad(wr, ((0, 0), (0, 0), (0, E_PAD - 2 * N_EXPERTS)))
    br = jnp.pad(b_router, ((0, 0), (0, E_PAD - N_EXPERTS)), constant_values=-1e30)
    return wr, br[:, None, :]


def _segment_copies(i, off_ref, seg_ref, gs_ref, local, remote, sem, to_remote):
    for e in range(N_EXPERTS):
        j = i * N_EXPERTS + e
        n = pl.multiple_of(seg_ref[j], SEG_ALIGN)

        @pl.when(n > 0)
        def _():
            lo = local.at[pl.ds(pl.multiple_of(off_ref[j], SEG_ALIGN), n)]
            re = remote.at[pl.ds(pl.multiple_of(gs_ref[j], SEG_ALIGN), n)]
            src, dst = (lo, re) if to_remote else (re, lo)
            pltpu.make_async_copy(src, dst, sem).start()


def _segment_wait(i, off_ref, seg_ref, local, remote, sem):
    j = i * N_EXPERTS + N_EXPERTS - 1
    tot = pl.multiple_of(off_ref[j] + seg_ref[j], SEG_ALIGN)
    pltpu.make_async_copy(local.at[pl.ds(0, tot)], remote.at[pl.ds(0, tot)], sem).wait()


def _local_row_iota(tt):
    return lax.broadcasted_iota(jnp.int32, (tt, SEG_ROWS), 1)


def _dispatch_kernel(off_ref, seg_ref, gs_ref, pend_ref, h_ref, lp_ref, xp_ref, xbuf_ref, zbuf_ref,
                     sem, zsem):
    i = pl.program_id(0)
    nt = pl.num_programs(0)
    slot = i % 2
    bm = zbuf_ref.shape[0]

    @pl.when(i == 0)
    def _():
        zbuf_ref[...] = jnp.zeros_like(zbuf_ref)

        def tail(e):
            start = jnp.where(e == 0, 0, pend_ref[jnp.maximum(e - 1, 0)])
            end = pend_ref[e]
            return end > start, xp_ref.at[pl.ds(pl.multiple_of(jnp.maximum(end - bm, 0), SEG_ALIGN), bm)]

        for e in range(N_EXPERTS):
            has, dst = tail(e)

            @pl.when(has)
            def _():
                pltpu.make_async_copy(zbuf_ref, dst, zsem).start()
        for e in range(N_EXPERTS):
            has, dst = tail(e)

            @pl.when(has)
            def _():
                pltpu.make_async_copy(zbuf_ref, dst, zsem).wait()

    lp = lp_ref[...]
    rid = _local_row_iota(lp.shape[0])
    hit = rid == lp[:, 0:1]
    for k in range(1, TOP_K):
        hit = hit | (rid == lp[:, k:k + 1])
    xbuf_ref[slot] = lax.dot_general(hit.astype(BF16), h_ref[...], (((0,), (0,)), ((), ())),
                                     preferred_element_type=F32)
    _segment_copies(i, off_ref, seg_ref, gs_ref, xbuf_ref.at[slot], xp_ref, sem.at[slot], True)

    @pl.when(i > 0)
    def _():
        _segment_wait(i - 1, off_ref, seg_ref, xbuf_ref.at[1 - slot], xp_ref, sem.at[1 - slot])

    @pl.when(i == nt - 1)
    def _():
        _segment_wait(i, off_ref, seg_ref, xbuf_ref.at[slot], xp_ref, sem.at[slot])


def _dispatch(h2, lp, lay):
    t, d = h2.shape
    tt = MOE_TILE
    grid_spec = pltpu.PrefetchScalarGridSpec(
        num_scalar_prefetch=4,
        grid=(t // tt,),
        in_specs=[pl.BlockSpec((tt, d), lambda i, *s: (i, 0)),
                  pl.BlockSpec((tt, E_PAD), lambda i, *s: (i, 0))],
        out_specs=pl.BlockSpec(memory_space=pl.ANY),
        scratch_shapes=[pltpu.VMEM((2, SEG_ROWS, d), F32), pltpu.VMEM((MOE_BLOCK, d), F32),
                        pltpu.SemaphoreType.DMA((2,)), pltpu.SemaphoreType.DMA],
    )
    return pl.pallas_call(
        _dispatch_kernel,
        out_shape=jax.ShapeDtypeStruct((lay["p_total"], d), F32),
        grid_spec=grid_spec,
        compiler_params=_cparams(("arbitrary",)),
        name="moe_dispatch",
    )(lay["off"], lay["seg"], lay["gseg"], lay["p_end"], h2, lp)


def _expert_kernel(be_ref, nu_ref, x_ref, w1_ref, b1_ref, w2_ref, b2_ref, y_ref, w1b_ref, w2b_ref):
    i = pl.program_id(0)
    de = w2b_ref.shape[0]
    used = i < nu_ref[0]

    @pl.when(used & ((i == 0) | (be_ref[i] != be_ref[jnp.maximum(i - 1, 0)])))
    def _():
        w1b_ref[...] = w1_ref[0, 0].astype(BF16)
        w2b_ref[...] = w2_ref[0, 0].astype(BF16)

    @pl.when(used)
    def _():
        hh = jnp.dot(x_ref[...].astype(BF16), w1b_ref[...], preferred_element_type=F32) + b1_ref[0, 0]
        gate = jnp.minimum(hh[:, :de], SWIGLU_LIMIT)
        lin = jnp.clip(hh[:, de:], -SWIGLU_LIMIT, SWIGLU_LIMIT)
        act = (lin + 1.0) * (gate * jax.nn.sigmoid(SWIGLU_ALPHA * gate))
        y_ref[...] = jnp.dot(act.astype(BF16), w2b_ref[...], preferred_element_type=F32) + b2_ref[0, 0]

    @pl.when(jnp.logical_not(used))
    def _():
        y_ref[...] = jnp.zeros_like(y_ref)


def _experts(xp, lay, layer, w1, b1, w2, b2):
    p_total, d = xp.shape
    depth, ne, _, de2 = w1.shape
    de = w2.shape[2]
    bm = MOE_BLOCK
    wmap = lambda i, be, nu: (layer, be[i], 0, 0)
    grid_spec = pltpu.PrefetchScalarGridSpec(
        num_scalar_prefetch=2,
        grid=(p_total // bm,),
        in_specs=[pl.BlockSpec((bm, d), lambda i, be, nu: (jnp.minimum(i, nu[0] - 1), 0)),
                  pl.BlockSpec((1, 1, d, de2), wmap),
                  pl.BlockSpec((1, 1, 1, de2), wmap),
                  pl.BlockSpec((1, 1, de, d), wmap),
                  pl.BlockSpec((1, 1, 1, d), wmap)],
        out_specs=pl.BlockSpec((bm, d), lambda i, be, nu: (i, 0)),
        scratch_shapes=[pltpu.VMEM((d, de2), BF16), pltpu.VMEM((de, d), BF16)],
    )
    return pl.pallas_call(
        _expert_kernel,
        out_shape=jax.ShapeDtypeStruct((p_total, d), F32),
        grid_spec=grid_spec,
        compiler_params=_cparams(("arbitrary",)),
        name="moe_experts",
    )(lay["blk_e"], lay["n_used"], xp, w1, b1.reshape(depth, ne, 1, de2), w2, b2.reshape(depth, ne, 1, d))


def _combine_kernel(grp_ref, off_ref, seg_ref, gs_ref, lp_ref, x_ref, tw_ref, mod_ref, fg_ref, yb_ref,
                    *refs, split_out, nct):
    del grp_ref
    o_refs = refs[:2] if split_out else refs[:1]
    ybuf_ref, sem = refs[len(o_refs):]
    i = pl.program_id(0)
    nt = pl.num_programs(0)
    slot = i % 2

    def gather(tile, s):
        _segment_copies(tile, off_ref, seg_ref, gs_ref, ybuf_ref.at[s], yb_ref, sem.at[s], False)

    @pl.when(i == 0)
    def _():
        ybuf_ref[...] = jnp.zeros_like(ybuf_ref)
        gather(0, 0)

    @pl.when(i + 1 < nt)
    def _():
        gather(i + 1, 1 - slot)

    _segment_wait(i, off_ref, seg_ref, ybuf_ref.at[slot], yb_ref, sem.at[slot])
    lp = lp_ref[...]
    tw = tw_ref[...]
    rid = _local_row_iota(lp.shape[0])
    wsel = jnp.where(rid == lp[:, 0:1], tw[:, 0:1], 0.0)
    for k in range(1, TOP_K):
        wsel = jnp.where(rid == lp[:, k:k + 1], tw[:, k:k + 1], wsel)
    acc = jnp.dot(wsel.astype(BF16), ybuf_ref[slot].astype(BF16), preferred_element_type=F32)
    x = x_ref[...] + mod_ref[0][5:6] * acc
    if not split_out:
        o_refs[0][...] = x
    else:
        y = x * lax.rsqrt(jnp.mean(x * x, axis=-1, keepdims=True) + NORM_EPS) * fg_ref[...]

        @pl.when(i < nct)
        def _():
            o_refs[0][...] = y

        @pl.when(i >= nct)
        def _():
            o_refs[1][...] = y


def _combine(x, yb, lp, tw, lay, tile_grp, mod, final_g, t_ctx, split_out):
    t, d = x.shape
    tt = MOE_TILE
    nct = t_ctx // tt
    if split_out:
        out_shape = [jax.ShapeDtypeStruct((t_ctx, d), F32), jax.ShapeDtypeStruct((t - t_ctx, d), F32)]
        out_specs = [pl.BlockSpec((tt, d), lambda i, *s: (jnp.minimum(i, nct - 1), 0)),
                     pl.BlockSpec((tt, d), lambda i, *s: (jnp.maximum(i - nct, 0), 0))]
    else:
        out_shape = [jax.ShapeDtypeStruct((t, d), F32)]
        out_specs = [pl.BlockSpec((tt, d), lambda i, *s: (i, 0))]
    grid_spec = pltpu.PrefetchScalarGridSpec(
        num_scalar_prefetch=4,
        grid=(t // tt,),
        in_specs=[pl.BlockSpec((tt, E_PAD), lambda i, *s: (i, 0)),
                  pl.BlockSpec((tt, d), lambda i, *s: (i, 0)),
                  pl.BlockSpec((tt, E_PAD), lambda i, *s: (i, 0)),
                  pl.BlockSpec((1, MOD_ROWS, d), lambda i, grp, *s: (grp[i], 0, 0)),
                  pl.BlockSpec((1, d), lambda i, *s: (0, 0)),
                  pl.BlockSpec(memory_space=pl.ANY)],
        out_specs=out_specs,
        scratch_shapes=[pltpu.VMEM((2, SEG_ROWS, d), F32), pltpu.SemaphoreType.DMA((2,))],
    )
    return pl.pallas_call(
        functools.partial(_combine_kernel, split_out=split_out, nct=nct),
        out_shape=out_shape,
        grid_spec=grid_spec,
        compiler_params=_cparams(("arbitrary",)),
        name="moe_combine",
    )(tile_grp, lay["off"], lay["seg"], lay["gseg"], lp, x, tw, mod, final_g.reshape(1, d), yb)


def _segment_layout(cnt, t):
    bm = MOE_BLOCK
    nt = t // MOE_TILE
    n = cnt[:, 0, :N_EXPERTS]
    seg = ((n + (SEG_ALIGN - 1)) // SEG_ALIGN) * SEG_ALIGN
    off = jnp.cumsum(seg, axis=1) - seg
    tot = jnp.sum(seg, axis=0)
    padded = ((tot + bm - 1) // bm) * bm
    p_end = jnp.cumsum(padded)
    gseg = (p_end - padded)[None, :] + jnp.cumsum(seg, axis=0) - seg
    max_rows = t * TOP_K + nt * N_EXPERTS * (SEG_ALIGN - 1)
    n_blk = -(-max_rows // bm) + N_EXPERTS
    blk_start = jnp.arange(n_blk, dtype=jnp.int32) * bm
    blk_e = jnp.minimum(jnp.sum(p_end[None, :] <= blk_start[:, None], axis=1), N_EXPERTS - 1)
    i32 = lambda a: a.reshape(-1).astype(jnp.int32)
    return dict(seg=i32(seg), off=i32(off), gseg=i32(gseg), p_end=i32(p_end), blk_e=i32(blk_e),
                n_used=i32(p_end[-1:] // bm), p_total=n_blk * bm)


def _tile_groups(t, t_ctx, dec_seq, tm):
    starts = jnp.arange(t // tm, dtype=jnp.int32) * tm
    return jnp.where(starts < t_ctx, 0, 1 + (starts - t_ctx) // dec_seq).astype(jnp.int32)


def kernel(x_prompt, x_sample, state_ret, c, c_ctx, w_ada, b_ada, norm1_g, norm2_g, w_in, ret_decay_logit, w_ret_o, w_four, w_out, w_router, b_router, w1, b1, w2, b2, final_g):
    batch, seq, d = x_prompt.shape
    dec_batch, dec_seq, _ = x_sample.shape
    depth = w_in.shape[0]
    t_ctx, t_lat = batch * seq, dec_batch * dec_seq
    t = t_ctx + t_lat
    tm_in = min(1024, t_ctx, dec_seq)
    tm = min(512, t_ctx, dec_seq)
    assert t_ctx % tm_in == 0 and dec_seq % tm_in == 0 and t_ctx % dec_seq == 0

    n_cond = 16
    cond = jnp.zeros((n_cond, d), F32).at[0].set(c_ctx).at[1:1 + dec_batch].set(c)
    mod = _adaln_mod(cond, w_ada, b_ada).reshape(depth, n_cond, 6, d)
    mod = jnp.pad(mod, ((0, 0), (0, 0), (0, MOD_ROWS - 6), (0, 0)))
    grp_in = _tile_groups(t, t_ctx, dec_seq, tm_in)
    grp = _tile_groups(t, t_ctx, dec_seq, tm)
    grp_moe = _tile_groups(t, t_ctx, dec_seq, MOE_TILE)

    rot = _rotary_tables(dec_seq)
    gw = FOURIER_GROUP_W
    jc = jnp.arange(gw, dtype=jnp.int32)
    ang_c = ((jc[:, None] * jc[None, :]) % gw).astype(F32) * (2.0 * jnp.pi / gw)
    cs = jnp.concatenate([jnp.cos(ang_c), jnp.sin(ang_c)], axis=1).astype(BF16)
    dm_ctx = _split_dft_mats(seq, (seq * gw) ** -0.5)
    dm_lat = _split_dft_mats(dec_seq, (dec_seq * gw) ** -0.5)
    log_g = jax.nn.log_sigmoid(ret_decay_logit.astype(F32))
    wr, br = _router_operands(w_router, b_router)

    xs = [x_prompt.reshape(t_ctx, d), x_sample.reshape(t_lat, d)]
    states = []
    for l in range(depth):
        last = l == depth - 1
        u = _in_proj(xs, grp_in, mod[l], norm1_g[l], w_in[l].astype(BF16), tm_in, 3072)
        tables = _decay_tables(log_g[l], min(RET_CHUNK, seq, dec_seq))
        og, st = _retention(u, None, n=seq, nseq=batch, row0=0, tables=tables, rot=None, s0=None,
                            layer=l, emit_state=True, heads=N_RET_HEADS)
        og = _retention(u, og, n=dec_seq, nseq=dec_batch, row0=t_ctx, tables=tables, rot=rot,
                        s0=state_ret, layer=l, emit_state=False, heads=1)
        states.append(st)
        uf = _fourier(u, None, n=seq, nseq=batch, row0=0, cs=cs, mats=dm_ctx, tr=seq, tc=FOURIER_W)
        uf = _fourier(u, uf, n=dec_seq, nseq=dec_batch, row0=t_ctx, cs=cs, mats=dm_lat,
                      tr=min(512, dec_seq), tc=512)
        x, h2, lp, tw, cnt = _merge(
            xs, og, uf, u, grp, mod[l], norm2_g[l], w_ret_o[l].astype(BF16), w_four[l].astype(BF16),
            w_out[l].astype(BF16), wr[l], br[l], tm)
        lay = _segment_layout(cnt, t)
        xp = _dispatch(h2, lp, lay)
        yb = _experts(xp, lay, l, w1, b1, w2, b2)
        xs = _combine(x, yb, lp, tw, lay, grp_moe, mod[l], final_g, t_ctx, split_out=last)

    new_state = jnp.stack(states, axis=1).astype(x_prompt.dtype)
    return (xs[0].reshape(batch, seq, d), xs[1].reshape(dec_batch, dec_seq, d), new_state)
```

```python
import functools

import jax
import jax.numpy as jnp
from jax import lax
from jax.experimental import pallas as pl
from jax.experimental.pallas import tpu as pltpu

F32 = jnp.float32
BF16 = jnp.bfloat16

N_RET_HEADS = 8
RET_DK = 128
RET_DV = 256
RET_QK_W = N_RET_HEADS * RET_DK
RET_V_W = N_RET_HEADS * RET_DV
N_FOURIER_GROUPS = 4
FOURIER_GROUP_W = 256
FOURIER_W = N_FOURIER_GROUPS * FOURIER_GROUP_W
GRID_W = 64
N_EXPERTS = 32
TOP_K = 4
SWIGLU_LIMIT = 7.0
SWIGLU_ALPHA = 1.702
ROPE_BASE = 10000.0
NORM_EPS = 1e-6
GN_EPS = 1e-5

COL_Q = 0
COL_K = COL_Q + RET_QK_W
COL_V = COL_K + RET_QK_W
COL_GRET = COL_V + RET_V_W
COL_FOUR = COL_GRET + RET_V_W
COL_GA = COL_FOUR + FOURIER_W

LANES = 128
VMEM_LIMIT_BYTES = 56 * 1024 * 1024

MOD_ROWS = 8
RET_CHUNK = 256
MOE_BLOCK = 512
MOE_TILE = 256
SEG_ALIGN = 8
SEG_ROWS = MOE_TILE * TOP_K + N_EXPERTS * SEG_ALIGN
E_PAD = LANES


def _cparams(sem, vmem=VMEM_LIMIT_BYTES):
    return pltpu.CompilerParams(dimension_semantics=sem, vmem_limit_bytes=vmem)


def _drop_ref(kern, idx):
    def wrapped(*refs):
        return kern(*refs[:idx], *refs[idx + 1:])
    return wrapped


def _mod_kernel(c_ref, w_ref, b_ref, o_ref):
    c = c_ref[...]
    s = c * jax.nn.sigmoid(c)
    o_ref[0] = jnp.dot(s, w_ref[0], precision=lax.Precision.HIGHEST,
                       preferred_element_type=F32) + b_ref[0]


def _adaln_mod(cond, w_ada, b_ada):
    depth, d, w6 = w_ada.shape
    r = cond.shape[0]
    tn = 1536
    assert w6 % tn == 0
    return pl.pallas_call(
        _mod_kernel,
        out_shape=jax.ShapeDtypeStruct((depth, r, w6), F32),
        grid=(depth, w6 // tn),
        in_specs=[pl.BlockSpec((r, d), lambda l, j: (0, 0)),
                  pl.BlockSpec((1, d, tn), lambda l, j: (l, 0, j)),
                  pl.BlockSpec((1, 1, tn), lambda l, j: (l, 0, j))],
        out_specs=pl.BlockSpec((1, r, tn), lambda l, j: (l, 0, j)),
        compiler_params=_cparams(("parallel", "parallel")),
        name="adaln_mod",
    )(cond, w_ada, b_ada.reshape(depth, 1, w6))


def _stream_specs(xs, tm, imap_args):
    d = xs[0].shape[1]
    if len(xs) == 1:
        return [pl.BlockSpec((tm, d), lambda i, *a: (i, 0))]
    nct = xs[0].shape[0] // tm
    return [pl.BlockSpec((tm, d), lambda i, *a: (jnp.minimum(i, nct - 1), 0)),
            pl.BlockSpec((tm, d), lambda i, *a: (jnp.maximum(i - nct, 0), 0))]


def _stream_tile(x_refs, i, nct):
    if len(x_refs) == 1:
        return x_refs[0][...]
    return jnp.where(i < nct, x_refs[0][...], x_refs[1][...])


def _rms_mod(x, g, shift, scale):
    y = x * lax.rsqrt(jnp.mean(x * x, axis=-1, keepdims=True) + NORM_EPS) * g
    return y * (1.0 + scale) + shift


def _in_proj_kernel(grp_ref, *refs, n_x, nct):
    del grp_ref
    x_refs = refs[:n_x]
    mod_ref, g_ref, w_ref, o_ref, h_ref = refs[n_x:]

    @pl.when(pl.program_id(1) == 0)
    def _():
        m = mod_ref[0]
        x = _stream_tile(x_refs, pl.program_id(0), nct)
        h_ref[...] = _rms_mod(x, g_ref[...], m[0:1], m[1:2]).astype(BF16)

    o_ref[...] = jnp.dot(h_ref[...], w_ref[...], preferred_element_type=F32).astype(BF16)


def _in_proj(xs, tile_grp, mod, g, w, tm, tn):
    t = sum(x.shape[0] for x in xs)
    d, n = w.shape
    assert t % tm == 0 and n % tn == 0 and all(x.shape[0] % tm == 0 for x in xs)
    x_specs = [pl.BlockSpec(s.block_shape, (lambda f: lambda i, j, grp: f(i))(s.index_map))
               for s in _stream_specs(xs, tm, 1)]
    grid_spec = pltpu.PrefetchScalarGridSpec(
        num_scalar_prefetch=1,
        grid=(t // tm, n // tn),
        in_specs=x_specs + [pl.BlockSpec((1, MOD_ROWS, d), lambda i, j, grp: (grp[i], 0, 0)),
                            pl.BlockSpec((1, d), lambda i, j, grp: (0, 0)),
                            pl.BlockSpec((d, tn), lambda i, j, grp: (0, j))],
        out_specs=pl.BlockSpec((tm, tn), lambda i, j, grp: (i, j)),
        scratch_shapes=[pltpu.VMEM((tm, d), BF16)],
    )
    return pl.pallas_call(
        functools.partial(_in_proj_kernel, n_x=len(xs), nct=xs[0].shape[0] // tm),
        out_shape=jax.ShapeDtypeStruct((t, n), BF16),
        grid_spec=grid_spec,
        compiler_params=_cparams(("parallel", "arbitrary")),
        name="in_proj",
    )(tile_grp, *xs, mod, g.reshape(1, d), w)


def _retention_kernel(cdec_ref, *refs, n, chunk, rotary, has_s0, emit_state, heads):
    for hh in range(heads):
        _retention_head(cdec_ref, refs, hh, n=n, chunk=chunk, rotary=rotary, has_s0=has_s0,
                        emit_state=emit_state, heads=heads)


def _retention_head(cdec_ref, refs, hh, *, n, chunk, rotary, has_s0, emit_state, heads):
    refs = list(refs)
    q_ref, k_ref, v_ref, g_ref, dmask_ref, dvec_ref, kdec_ref = refs[:7]
    pos = 7
    if rotary:
        cos_ref, sin_ref = refs[pos:pos + 2]
        pos += 2
    if has_s0:
        s0_ref = refs[pos]
        pos += 1
    o_ref = refs[pos]
    pos += 1
    if emit_state:
        sn_ref = refs[pos]
        pos += 1
    qs_ref, ks_ref, kv_ref, st_ref, s_ref = [ref.at[hh] for ref in refs[pos:pos + 5]]
    qcols = pl.ds(hh * RET_DK, RET_DK)
    vcols = pl.ds(hh * RET_DV, RET_DV)

    nc = n // chunk
    cross = has_s0 or nc > 1
    unroll = 16 if nc % 16 == 0 else (2 if nc % 2 == 0 else 1)
    h = pl.program_id(0) * heads + hh
    cdec_f = cdec_ref[2 * h]
    cdec_b = cdec_ref[2 * h + 1]
    fwd = pl.ds(0, RET_DK)
    bwd = pl.ds(RET_DK, RET_DK)

    def rows(c):
        return pl.ds(pl.multiple_of(c * chunk, chunk), chunk)

    def prep(c, carry):
        r = rows(c)
        q = q_ref[r, qcols].astype(F32) * (RET_DK ** -0.5)
        k = k_ref[r, qcols].astype(F32)
        if rotary:
            cs, sn = cos_ref[r, :], sin_ref[r, :]
            q = q * cs + pltpu.roll(q, RET_DK // 2, 1) * sn
            k = k * cs + pltpu.roll(k, RET_DK // 2, 1) * sn
        qs_ref[r, :] = q.astype(BF16)
        kt = k.T.astype(BF16)
        ks_ref[c] = kt
        ktf = kt.astype(F32)
        kdec = kdec_ref[hh]
        kd = jnp.concatenate([(ktf * kdec[0:1]).astype(BF16), (ktf * kdec[1:2]).astype(BF16)], axis=0)
        kv_ref[c] = jnp.dot(kd, v_ref[r, vcols], preferred_element_type=F32)
        return carry

    def scan(i, carry):
        cb = nc - 1 - i
        st_ref[i, fwd, :] = s_ref[fwd, :].astype(BF16)
        s_ref[fwd, :] = s_ref[fwd, :] * cdec_f + kv_ref[i, fwd, :]
        st_ref[cb, bwd, :] = s_ref[bwd, :].astype(BF16)
        s_ref[bwd, :] = s_ref[bwd, :] * cdec_b + kv_ref[cb, bwd, :]
        return carry

    def out_step(c, carry):
        r = rows(c)
        qc = qs_ref[r, :]
        sc = jnp.dot(qc, ks_ref[c], preferred_element_type=F32)
        o = jnp.dot((sc * dmask_ref[hh]).astype(BF16), v_ref[r, vcols], preferred_element_type=F32)
        if cross:
            qf = qc.astype(F32)
            ql = jnp.concatenate([(qf * dvec_ref[hh, 0]).astype(BF16),
                                  (qf * dvec_ref[hh, 2]).astype(BF16)], axis=1)
            o = o + jnp.dot(ql, st_ref[c], preferred_element_type=F32)
        mu = jnp.mean(o, axis=-1, keepdims=True)
        oc = o - mu
        var = jnp.mean(oc * oc, axis=-1, keepdims=True)
        gt = g_ref[r, vcols]
        o_ref[r, vcols] = (gt * jax.nn.sigmoid(gt)) * (oc * lax.rsqrt(var + GN_EPS)).astype(BF16)
        return carry

    def loop(body):
        if nc == 1:
            body(0, 0)
        else:
            lax.fori_loop(0, nc, body, 0, unroll=unroll)

    if has_s0:
        s_ref[fwd, :] = s0_ref[0, 0, 0, hh]
        s_ref[bwd, :] = s0_ref[0, 0, 1, hh]
    else:
        s_ref[...] = jnp.zeros_like(s_ref)
    loop(prep)
    if nc == 1:
        scan(0, 0)
    else:
        lax.fori_loop(0, nc, scan, 0, unroll=unroll)
    if emit_state:
        sn_ref[0, 0, 0, hh] = s_ref[fwd, :]
        sn_ref[0, 0, 1, hh] = s_ref[bwd, :]
    loop(out_step)


def _decay_tables(log_g, chunk):
    i = jnp.arange(chunk, dtype=F32)
    diff = i[:, None] - i[None, :]
    lf = log_g[0][:, None, None]
    lb = log_g[1][:, None, None]
    dmask = (jnp.where(diff >= 0, jnp.exp(jnp.maximum(diff, 0.0) * lf), 0.0)
             + jnp.where(diff <= 0, jnp.exp(jnp.maximum(-diff, 0.0) * lb), 0.0))
    lf2, lb2 = log_g[0][:, None], log_g[1][:, None]
    vecs = jnp.stack([jnp.exp((i + 1.0)[None, :] * lf2),
                      jnp.exp((chunk - 1.0 - i)[None, :] * lf2),
                      jnp.exp((chunk - i)[None, :] * lb2),
                      jnp.exp(i[None, :] * lb2)], axis=1)
    dvec = jnp.broadcast_to(vecs[..., None], vecs.shape + (LANES,))
    kdec = jnp.pad(vecs[:, 1::2], ((0, 0), (0, MOD_ROWS - 2), (0, 0)))
    cdec = jnp.exp(chunk * log_g).T.reshape(-1)
    return dmask, dvec, kdec, cdec


def _retention(u, og_prev, *, n, nseq, row0, tables, rot, s0, layer, emit_state, heads, depth=1, st_prev=None):
    t = u.shape[0]
    chunk = min(RET_CHUNK, n)
    assert n % chunk == 0 and row0 % n == 0 and N_RET_HEADS % heads == 0
    rb = row0 // n
    dmask, dvec, kdec, cdec = tables
    nc = n // chunk
    kw, vw = heads * RET_DK, heads * RET_DV
    qb, vb, gb = COL_K // kw, COL_V // vw, COL_GRET // vw
    in_specs = [pl.BlockSpec((n, kw), lambda h, b, cd: (rb + b, h)),
                pl.BlockSpec((n, kw), lambda h, b, cd: (rb + b, qb + h)),
                pl.BlockSpec((n, vw), lambda h, b, cd: (rb + b, vb + h)),
                pl.BlockSpec((n, vw), lambda h, b, cd: (rb + b, gb + h)),
                pl.BlockSpec((heads, chunk, chunk), lambda h, b, cd: (h, 0, 0)),
                pl.BlockSpec((heads, 4, chunk, LANES), lambda h, b, cd: (h, 0, 0, 0)),
                pl.BlockSpec((heads, MOD_ROWS, chunk), lambda h, b, cd: (h, 0, 0))]
    args = [u, u, u, u, dmask, dvec, kdec]
    if rot is not None:
        in_specs += [pl.BlockSpec((n, RET_DK), lambda h, b, cd: (0, 0))] * 2
        args += list(rot)
    if s0 is not None:
        in_specs.append(pl.BlockSpec((1, 1, 2, heads, RET_DK, RET_DV),
                                     lambda h, b, cd: (b, layer, 0, h, 0, 0)))
        args.append(s0)
    out_shape = [jax.ShapeDtypeStruct((t, RET_V_W), BF16)]
    out_specs = [pl.BlockSpec((n, vw), lambda h, b, cd: (rb + b, h))]
    if emit_state:
        out_shape.append(jax.ShapeDtypeStruct((nseq, depth, 2, N_RET_HEADS, RET_DK, RET_DV), F32))
        out_specs.append(pl.BlockSpec((1, 1, 2, heads, RET_DK, RET_DV),
                                      lambda h, b, cd: (b, layer, 0, h, 0, 0)))
    kern = functools.partial(_retention_kernel, n=n, chunk=chunk, rotary=rot is not None,
                             has_s0=s0 is not None, emit_state=emit_state, heads=heads)
    aliases = {}
    for prev, out_idx in ((og_prev, 0), (st_prev, 1)):
        if prev is not None:
            in_specs.append(pl.BlockSpec(memory_space=pl.ANY))
            args.append(prev)
            aliases[len(args)] = out_idx
            kern = _drop_ref(kern, len(args))
    grid_spec = pltpu.PrefetchScalarGridSpec(
        num_scalar_prefetch=1, grid=(N_RET_HEADS // heads, nseq), in_specs=in_specs, out_specs=out_specs,
        scratch_shapes=[pltpu.VMEM((heads, n, RET_DK), BF16), pltpu.VMEM((heads, nc, RET_DK, chunk), BF16),
                        pltpu.VMEM((heads, nc, 2 * RET_DK, RET_DV), F32),
                        pltpu.VMEM((heads, nc, 2 * RET_DK, RET_DV), BF16),
                        pltpu.VMEM((heads, 2 * RET_DK, RET_DV), F32)])
    outs = pl.pallas_call(
        kern, out_shape=out_shape, grid_spec=grid_spec, input_output_aliases=aliases,
        compiler_params=_cparams(("parallel", "parallel")),
        name="retention_lat" if rot is not None else "retention_ctx",
    )(cdec, *args)
    return outs if emit_state else outs[0]


def _rotary_tables(n):
    rows = n // GRID_W
    row = jnp.repeat(jnp.arange(rows), GRID_W).astype(F32)
    col = jnp.tile(jnp.arange(GRID_W), rows).astype(F32)
    n_freq = RET_DK // 4
    inv_freq = ROPE_BASE ** (-jnp.arange(n_freq, dtype=F32) / n_freq)
    ang = jnp.concatenate([row[:, None] * inv_freq, col[:, None] * inv_freq], axis=-1)
    cos, sin = jnp.cos(ang), jnp.sin(ang)
    return jnp.concatenate([cos, cos], axis=-1), jnp.concatenate([-sin, sin], axis=-1)


def _fourier_kernel(u_ref, cs_ref, ce_ref, se_ref, co_ref, so_ref, pint_ref, o_ref, fold_ref, *, n):
    h = n // 2
    gw = FOURIER_GROUP_W
    tc = u_ref.shape[1]
    rc = min(512, h)

    @pl.when(pl.program_id(2) == 0)
    def _():
        def fold_rows(r, carry):
            top = pl.ds(pl.multiple_of(r * rc, rc), rc)
            bot = pl.ds(pl.multiple_of(h + r * rc, rc), rc)
            for g in range(tc // gw):
                cols = pl.ds(g * gw, gw)
                zt = jnp.dot(u_ref[top, cols], cs_ref[...], preferred_element_type=F32)
                zb = jnp.dot(u_ref[bot, cols], cs_ref[...], preferred_element_type=F32)
                even, odd = zt + zb, zt - zb
                fold_ref[0, top, cols] = even[:, :gw].astype(BF16)
                fold_ref[1, top, cols] = even[:, gw:].astype(BF16)
                fold_ref[2, top, cols] = odd[:, :gw].astype(BF16)
                fold_ref[3, top, cols] = odd[:, gw:].astype(BF16)
            return carry

        if h == rc:
            fold_rows(0, 0)
        else:
            lax.fori_loop(0, h // rc, fold_rows, 0)

    ye = (jnp.dot(ce_ref[...], fold_ref[0], preferred_element_type=F32)
          + jnp.dot(se_ref[...], fold_ref[1], preferred_element_type=F32))
    yo = (jnp.dot(co_ref[...], fold_ref[2], preferred_element_type=F32)
          + jnp.dot(so_ref[...], fold_ref[3], preferred_element_type=F32))
    y = jnp.concatenate([ye, yo], axis=0).astype(BF16)
    o_ref[...] = jnp.dot(pint_ref[...], y, preferred_element_type=F32).astype(BF16)


def _fourier(u, uf_prev, *, n, nseq, row0, cs, mats, tr, tc):
    t = u.shape[0]
    h = n // 2
    assert n % tr == 0 and FOURIER_W % tc == 0 and tc % FOURIER_GROUP_W == 0 and row0 % n == 0
    rb = row0 // n
    nr = n // tr
    fb = COL_FOUR // tc
    half = pl.BlockSpec((tr // 2, h), lambda b, j, i: (i, 0))
    r = jnp.arange(tr, dtype=jnp.int32)
    pint = ((r // 2 + (r % 2) * (tr // 2))[:, None] == r[None, :]).astype(BF16)
    in_specs = [pl.BlockSpec((n, tc), lambda b, j, i: (rb + b, fb + j)),
                pl.BlockSpec(cs.shape, lambda b, j, i: (0, 0)),
                half, half, half, half,
                pl.BlockSpec((tr, tr), lambda b, j, i: (0, 0))]
    args = [u, cs, *mats, pint]
    aliases = {}
    kern = functools.partial(_fourier_kernel, n=n)
    if uf_prev is not None:
        in_specs.append(pl.BlockSpec(memory_space=pl.ANY))
        args.append(uf_prev)
        aliases = {len(args) - 1: 0}
        kern = _drop_ref(kern, len(args) - 1)
    return pl.pallas_call(
        kern,
        out_shape=jax.ShapeDtypeStruct((t, FOURIER_W), BF16),
        grid=(nseq, FOURIER_W // tc, nr),
        in_specs=in_specs,
        out_specs=pl.BlockSpec((tr, tc), lambda b, j, i: ((rb + b) * nr + i, j)),
        scratch_shapes=[pltpu.VMEM((4, h, tc), BF16)],
        input_output_aliases=aliases,
        compiler_params=_cparams(("parallel", "parallel", "arbitrary")),
        name="fourier",
    )(*args)


def _split_dft_mats(n, scale):
    h = n // 2
    assert n % 4 == 0
    jr = jnp.arange(h // 2, dtype=jnp.int32)
    m = jnp.arange(h, dtype=jnp.int32)
    sgn = jnp.where(m % 2 == 0, 1.0, -1.0).astype(F32)

    def mats(r):
        ang = ((r[:, None] * m[None, :]) % n).astype(F32) * (2.0 * jnp.pi / n)
        halves = jnp.stack([jnp.ones_like(sgn), sgn])[:, None, :]
        full = lambda top: (top[None] * halves).reshape(h, h).astype(BF16)
        return full(jnp.cos(ang) * scale), full(-jnp.sin(ang) * scale)

    ce, se = mats(2 * jr)
    co, so = mats(2 * jr + 1)
    return ce, se, co, so


def _merge_kernel(grp_ref, *refs, n_x, nct):
    del grp_ref
    x_refs = refs[:n_x]
    (og_ref, uf_ref, ga_ref, gb_ref, mod_ref, g2_ref, wro_ref, wf_ref, wo_ref, wr_ref, br_ref,
     xo_ref, h2_ref, lp_ref, tw_ref, cnt_ref) = refs[n_x:]
    tm = og_ref.shape[0]

    a = jnp.dot(og_ref[...], wro_ref[...], preferred_element_type=F32)
    b = jnp.dot(uf_ref[...], wf_ref[...], preferred_element_type=F32)
    merged = (jax.nn.sigmoid(ga_ref[...].astype(F32)) * a
              + jax.nn.sigmoid(gb_ref[...].astype(F32)) * b)
    y = jnp.dot(merged.astype(BF16), wo_ref[...], preferred_element_type=F32)
    m = mod_ref[0]
    x = _stream_tile(x_refs, pl.program_id(0), nct) + m[2:3] * y
    xo_ref[...] = x
    h2 = _rms_mod(x, g2_ref[...], m[3:4], m[4:5])
    h_hi = h2.astype(BF16)
    h2_ref[...] = h_hi
    h_lo = (h2 - h_hi.astype(F32)).astype(BF16)
    p_hi = jnp.dot(h_hi, wr_ref[...], preferred_element_type=F32)
    p_lo = jnp.dot(h_lo, wr_ref[...], preferred_element_type=F32)
    logits = p_hi + pltpu.roll(p_hi, E_PAD - N_EXPERTS, 1) + p_lo + br_ref[...]
    for s in range(tm // MOE_TILE):
        lp, tw, n = _route_tile(logits[s * MOE_TILE:(s + 1) * MOE_TILE])
        lp_ref[pl.ds(s * MOE_TILE, MOE_TILE), :] = lp
        tw_ref[pl.ds(s * MOE_TILE, MOE_TILE), :] = tw
        cnt_ref[s] = n


def _route_tile(logits):
    tt = logits.shape[0]
    lane = lax.broadcasted_iota(jnp.int32, (tt, E_PAD), 1)
    rest = logits
    vals, sels = [], []
    for _ in range(TOP_K):
        mx = jnp.max(rest, axis=-1, keepdims=True)
        idx = jnp.min(jnp.where(rest == mx, lane, E_PAD), axis=-1, keepdims=True)
        sel = lane == idx
        vals.append(mx)
        sels.append(sel)
        rest = jnp.where(sel, -jnp.inf, rest)
    exps = [jnp.exp(v - vals[0]) for v in vals]
    den = exps[0] + exps[1] + exps[2] + exps[3]

    picked = jnp.zeros((tt, E_PAD), F32)
    for sel in sels:
        picked = picked + sel.astype(F32)
    n = jnp.sum(picked, axis=0, keepdims=True).astype(jnp.int32)
    seg = ((n + (SEG_ALIGN - 1)) // SEG_ALIGN) * SEG_ALIGN
    er = lax.broadcasted_iota(jnp.int32, (E_PAD, E_PAD), 0)
    ec = lax.broadcasted_iota(jnp.int32, (E_PAD, E_PAD), 1)
    seg_rows = jnp.broadcast_to(seg.astype(F32), (SEG_ALIGN, E_PAD)).astype(BF16)
    off = jnp.dot(seg_rows, (er < ec).astype(BF16), preferred_element_type=F32)[0:1]
    ri = lax.broadcasted_iota(jnp.int32, (tt, tt), 0)
    ci = lax.broadcasted_iota(jnp.int32, (tt, tt), 1)
    before = jnp.dot((ri > ci).astype(BF16), picked.astype(BF16), preferred_element_type=F32) + off

    lp = jnp.zeros((tt, E_PAD), jnp.int32)
    tw = jnp.zeros((tt, E_PAD), F32)
    for k in range(TOP_K):
        row = jnp.sum(jnp.where(sels[k], before, 0.0), axis=-1, keepdims=True).astype(jnp.int32)
        lp = jnp.where(lane == k, row, lp)
        tw = jnp.where(lane == k, exps[k] / den, tw)
    return lp, tw, n


def _merge(xs, og, uf, u, tile_grp, mod, g2, wro, wf, wo, wr, br, tm):
    t, d = og.shape[0], wo.shape[1]
    assert tm % MOE_TILE == 0
    sub = tm // MOE_TILE
    ga_b, gb_b = COL_GA // d, COL_GA // d + 1
    full = lambda shape: pl.BlockSpec(shape, lambda i, grp: (0,) * len(shape))
    row = lambda w: pl.BlockSpec((tm, w), lambda i, grp: (i, 0))
    grid_spec = pltpu.PrefetchScalarGridSpec(
        num_scalar_prefetch=1,
        grid=(t // tm,),
        in_specs=_stream_specs(xs, tm, 1) + [
            row(RET_V_W), row(FOURIER_W),
            pl.BlockSpec((tm, d), lambda i, grp: (i, ga_b)),
            pl.BlockSpec((tm, d), lambda i, grp: (i, gb_b)),
            pl.BlockSpec((1, MOD_ROWS, d), lambda i, grp: (grp[i], 0, 0)),
            full((1, d)), full(wro.shape), full(wf.shape), full(wo.shape),
            full(wr.shape), full(br.shape)],
        out_specs=[row(d), row(d), row(E_PAD), row(E_PAD),
                   pl.BlockSpec((sub, 1, E_PAD), lambda i, grp: (i, 0, 0))],
    )
    return pl.pallas_call(
        functools.partial(_merge_kernel, n_x=len(xs), nct=xs[0].shape[0] // tm),
        out_shape=[jax.ShapeDtypeStruct((t, d), F32), jax.ShapeDtypeStruct((t, d), BF16),
                   jax.ShapeDtypeStruct((t, E_PAD), jnp.int32), jax.ShapeDtypeStruct((t, E_PAD), F32),
                   jax.ShapeDtypeStruct((t // MOE_TILE, 1, E_PAD), jnp.int32)],
        grid_spec=grid_spec,
        compiler_params=_cparams(("parallel",)),
        name="merge_route",
    )(tile_grp, *xs, og, uf, u, u, mod, g2.reshape(1, d), wro, wf, wo, wr, br)


def _router_operands(w_router, b_router):
    hi = w_router.astype(BF16)
    lo = (w_router - hi.astype(F32)).astype(BF16)
    wr = jnp.concatenate([hi, lo], axis=-1)
    wr = jnp.pad(wr, ((0, 0), (0, 0), (0, E_PAD - 2 * N_EXPERTS)))
    br = jnp.pad(b_router, ((0, 0), (0, E_PAD - N_EXPERTS)), constant_values=-1e30)
    return wr, br[:, None, :]


def _segment_copies(i, off_ref, seg_ref, gs_ref, local, remote, sem, to_remote):
    for e in range(N_EXPERTS):
        j = i * N_EXPERTS + e
        n = pl.multiple_of(seg_ref[j], SEG_ALIGN)

        @pl.when(n > 0)
        def _():
            lo = local.at[pl.ds(pl.multiple_of(off_ref[j], SEG_ALIGN), n)]
            re = remote.at[pl.ds(pl.multiple_of(gs_ref[j], SEG_ALIGN), n)]
            src, dst = (lo, re) if to_remote else (re, lo)
            pltpu.make_async_copy(src, dst, sem).start()


def _segment_wait(i, off_ref, seg_ref, local, remote, sem):
    j = i * N_EXPERTS + N_EXPERTS - 1
    tot = pl.multiple_of(off_ref[j] + seg_ref[j], SEG_ALIGN)
    pltpu.make_async_copy(local.at[pl.ds(0, tot)], remote.at[pl.ds(0, tot)], sem).wait()


def _local_row_iota(tt):
    return lax.broadcasted_iota(jnp.int32, (tt, SEG_ROWS), 1)


def _dispatch_kernel(off_ref, seg_ref, gs_ref, pend_ref, h_ref, lp_ref, xp_ref, xbuf_ref, zbuf_ref,
                     sem, zsem):
    i = pl.program_id(0)
    nt = pl.num_programs(0)
    slot = i % 2
    bm = zbuf_ref.shape[0]

    @pl.when(i == 0)
    def _():
        zbuf_ref[...] = jnp.zeros_like(zbuf_ref)

        def tail(e):
            start = jnp.where(e == 0, 0, pend_ref[jnp.maximum(e - 1, 0)])
            end = pend_ref[e]
            return end > start, xp_ref.at[pl.ds(pl.multiple_of(jnp.maximum(end - bm, 0), SEG_ALIGN), bm)]

        for e in range(N_EXPERTS):
            has, dst = tail(e)

            @pl.when(has)
            def _():
                pltpu.make_async_copy(zbuf_ref, dst, zsem).start()
        for e in range(N_EXPERTS):
            has, dst = tail(e)

            @pl.when(has)
            def _():
                pltpu.make_async_copy(zbuf_ref, dst, zsem).wait()

    lp = lp_ref[...]
    rid = _local_row_iota(lp.shape[0])
    hit = rid == lp[:, 0:1]
    for k in range(1, TOP_K):
        hit = hit | (rid == lp[:, k:k + 1])
    xbuf_ref[slot] = lax.dot_general(hit.astype(BF16), h_ref[...], (((0,), (0,)), ((), ())),
                                     preferred_element_type=F32)
    _segment_copies(i, off_ref, seg_ref, gs_ref, xbuf_ref.at[slot], xp_ref, sem.at[slot], True)

    @pl.when(i > 0)
    def _():
        _segment_wait(i - 1, off_ref, seg_ref, xbuf_ref.at[1 - slot], xp_ref, sem.at[1 - slot])

    @pl.when(i == nt - 1)
    def _():
        _segment_wait(i, off_ref, seg_ref, xbuf_ref.at[slot], xp_ref, sem.at[slot])


def _dispatch(h2, lp, lay):
    t, d = h2.shape
    tt = MOE_TILE
    grid_spec = pltpu.PrefetchScalarGridSpec(
        num_scalar_prefetch=4,
        grid=(t // tt,),
        in_specs=[pl.BlockSpec((tt, d), lambda i, *s: (i, 0)),
                  pl.BlockSpec((tt, E_PAD), lambda i, *s: (i, 0))],
        out_specs=pl.BlockSpec(memory_space=pl.ANY),
        scratch_shapes=[pltpu.VMEM((2, SEG_ROWS, d), F32), pltpu.VMEM((MOE_BLOCK, d), F32),
                        pltpu.SemaphoreType.DMA((2,)), pltpu.SemaphoreType.DMA],
    )
    return pl.pallas_call(
        _dispatch_kernel,
        out_shape=jax.ShapeDtypeStruct((lay["p_total"], d), F32),
        grid_spec=grid_spec,
        compiler_params=_cparams(("arbitrary",)),
        name="moe_dispatch",
    )(lay["off"], lay["seg"], lay["gseg"], lay["p_end"], h2, lp)


def _expert_kernel(be_ref, nu_ref, nxt_ref, x_ref, w1_ref, b1_ref, w2_ref, b2_ref, y_ref,
                   w1s_ref, w2s_ref, w1b_ref, w2b_ref, sem, *, layer):
    i = pl.program_id(0)
    de = w2b_ref.shape[0]
    used = i < nu_ref[0]
    e = be_ref[i]

    def fetch(ex):
        return (pltpu.make_async_copy(w1_ref.at[layer, ex], w1s_ref, sem.at[0]),
                pltpu.make_async_copy(w2_ref.at[layer, ex], w2s_ref, sem.at[1]))

    @pl.when(i == 0)
    def _():
        for cp in fetch(e):
            cp.start()

    @pl.when(used & ((i == 0) | (e != be_ref[jnp.maximum(i - 1, 0)])))
    def _():
        for cp in fetch(e):
            cp.wait()
        w1b_ref[...] = w1s_ref[...].astype(BF16)
        w2b_ref[...] = w2s_ref[...].astype(BF16)
        nxt = nxt_ref[e]

        @pl.when(nxt >= 0)
        def _():
            for cp in fetch(nxt):
                cp.start()

    @pl.when(used)
    def _():
        hh = jnp.dot(x_ref[...].astype(BF16), w1b_ref[...], preferred_element_type=F32) + b1_ref[0, 0]
        gate = jnp.minimum(hh[:, :de], SWIGLU_LIMIT)
        lin = jnp.clip(hh[:, de:], -SWIGLU_LIMIT, SWIGLU_LIMIT)
        act = (lin + 1.0) * (gate * jax.nn.sigmoid(SWIGLU_ALPHA * gate))
        y_ref[...] = jnp.dot(act.astype(BF16), w2b_ref[...], preferred_element_type=F32) + b2_ref[0, 0]

    @pl.when(jnp.logical_not(used))
    def _():
        y_ref[...] = jnp.zeros_like(y_ref)


def _experts(xp, lay, layer, w1, b1, w2, b2):
    p_total, d = xp.shape
    depth, ne, _, de2 = w1.shape
    de = w2.shape[2]
    bm = MOE_BLOCK
    bmap = lambda i, be, nu, nxt: (layer, be[i], 0, 0)
    grid_spec = pltpu.PrefetchScalarGridSpec(
        num_scalar_prefetch=3,
        grid=(p_total // bm,),
        in_specs=[pl.BlockSpec((bm, d), lambda i, be, nu, nxt: (jnp.minimum(i, nu[0] - 1), 0)),
                  pl.BlockSpec(memory_space=pl.ANY),
                  pl.BlockSpec((1, 1, 1, de2), bmap),
                  pl.BlockSpec(memory_space=pl.ANY),
                  pl.BlockSpec((1, 1, 1, d), bmap)],
        out_specs=pl.BlockSpec((bm, d), lambda i, be, nu, nxt: (i, 0)),
        scratch_shapes=[pltpu.VMEM((d, de2), F32), pltpu.VMEM((de, d), F32),
                        pltpu.VMEM((d, de2), BF16), pltpu.VMEM((de, d), BF16),
                        pltpu.SemaphoreType.DMA((2,))],
    )
    return pl.pallas_call(
        functools.partial(_expert_kernel, layer=layer),
        out_shape=jax.ShapeDtypeStruct((p_total, d), F32),
        grid_spec=grid_spec,
        compiler_params=_cparams(("arbitrary",)),
        name="moe_experts",
    )(lay["blk_e"], lay["n_used"], lay["nxt"], xp, w1, b1.reshape(depth, ne, 1, de2), w2,
      b2.reshape(depth, ne, 1, d))


def _combine_kernel(grp_ref, off_ref, seg_ref, gs_ref, lp_ref, x_ref, tw_ref, mod_ref, fg_ref, yb_ref,
                    *refs, split_out, nct):
    del grp_ref
    o_refs = refs[:2] if split_out else refs[:1]
    ybuf_ref, sem = refs[len(o_refs):]
    i = pl.program_id(0)
    nt = pl.num_programs(0)
    slot = i % 2

    def gather(tile, s):
        _segment_copies(tile, off_ref, seg_ref, gs_ref, ybuf_ref.at[s], yb_ref, sem.at[s], False)

    @pl.when(i == 0)
    def _():
        ybuf_ref[...] = jnp.zeros_like(ybuf_ref)
        gather(0, 0)

    @pl.when(i + 1 < nt)
    def _():
        gather(i + 1, 1 - slot)

    _segment_wait(i, off_ref, seg_ref, ybuf_ref.at[slot], yb_ref, sem.at[slot])
    lp = lp_ref[...]
    tw = tw_ref[...]
    rid = _local_row_iota(lp.shape[0])
    wsel = jnp.where(rid == lp[:, 0:1], tw[:, 0:1], 0.0)
    for k in range(1, TOP_K):
        wsel = jnp.where(rid == lp[:, k:k + 1], tw[:, k:k + 1], wsel)
    acc = jnp.dot(wsel.astype(BF16), ybuf_ref[slot].astype(BF16), preferred_element_type=F32)
    x = x_ref[...] + mod_ref[0][5:6] * acc
    if not split_out:
        o_refs[0][...] = x
    else:
        y = x * lax.rsqrt(jnp.mean(x * x, axis=-1, keepdims=True) + NORM_EPS) * fg_ref[...]

        @pl.when(i < nct)
        def _():
            o_refs[0][...] = y

        @pl.when(i >= nct)
        def _():
            o_refs[1][...] = y


def _combine(x, yb, lp, tw, lay, tile_grp, mod, final_g, t_ctx, split_out):
    t, d = x.shape
    tt = MOE_TILE
    nct = t_ctx // tt
    if split_out:
        out_shape = [jax.ShapeDtypeStruct((t_ctx, d), F32), jax.ShapeDtypeStruct((t - t_ctx, d), F32)]
        out_specs = [pl.BlockSpec((tt, d), lambda i, *s: (jnp.minimum(i, nct - 1), 0)),
                     pl.BlockSpec((tt, d), lambda i, *s: (jnp.maximum(i - nct, 0), 0))]
    else:
        out_shape = [jax.ShapeDtypeStruct((t, d), F32)]
        out_specs = [pl.BlockSpec((tt, d), lambda i, *s: (i, 0))]
    grid_spec = pltpu.PrefetchScalarGridSpec(
        num_scalar_prefetch=4,
        grid=(t // tt,),
        in_specs=[pl.BlockSpec((tt, E_PAD), lambda i, *s: (i, 0)),
                  pl.BlockSpec((tt, d), lambda i, *s: (i, 0)),
                  pl.BlockSpec((tt, E_PAD), lambda i, *s: (i, 0)),
                  pl.BlockSpec((1, MOD_ROWS, d), lambda i, grp, *s: (grp[i], 0, 0)),
                  pl.BlockSpec((1, d), lambda i, *s: (0, 0)),
                  pl.BlockSpec(memory_space=pl.ANY)],
        out_specs=out_specs,
        scratch_shapes=[pltpu.VMEM((2, SEG_ROWS, d), F32), pltpu.SemaphoreType.DMA((2,))],
    )
    return pl.pallas_call(
        functools.partial(_combine_kernel, split_out=split_out, nct=nct),
        out_shape=out_shape,
        grid_spec=grid_spec,
        compiler_params=_cparams(("arbitrary",)),
        name="moe_combine",
    )(tile_grp, lay["off"], lay["seg"], lay["gseg"], lp, x, tw, mod, final_g.reshape(1, d), yb)


def _segment_layout(cnt, t):
    bm = MOE_BLOCK
    nt = t // MOE_TILE
    n = cnt[:, 0, :N_EXPERTS]
    seg = ((n + (SEG_ALIGN - 1)) // SEG_ALIGN) * SEG_ALIGN
    off = jnp.cumsum(seg, axis=1) - seg
    tot = jnp.sum(seg, axis=0)
    padded = ((tot + bm - 1) // bm) * bm
    p_end = jnp.cumsum(padded)
    gseg = (p_end - padded)[None, :] + jnp.cumsum(seg, axis=0) - seg
    max_rows = t * TOP_K + nt * N_EXPERTS * (SEG_ALIGN - 1)
    n_blk = -(-max_rows // bm) + N_EXPERTS
    blk_start = jnp.arange(n_blk, dtype=jnp.int32) * bm
    blk_e = jnp.minimum(jnp.sum(p_end[None, :] <= blk_start[:, None], axis=1), N_EXPERTS - 1)
    ex = jnp.arange(N_EXPERTS, dtype=jnp.int32)
    later = (ex[None, :] > ex[:, None]) & (tot[None, :] > 0)
    nxt = jnp.where(jnp.any(later, axis=1), jnp.argmax(later, axis=1), -1)
    i32 = lambda a: a.reshape(-1).astype(jnp.int32)
    return dict(seg=i32(seg), off=i32(off), gseg=i32(gseg), p_end=i32(p_end), blk_e=i32(blk_e),
                n_used=i32(p_end[-1:] // bm), nxt=i32(nxt), p_total=n_blk * bm)


def _tile_groups(t, t_ctx, dec_seq, tm):
    starts = jnp.arange(t // tm, dtype=jnp.int32) * tm
    return jnp.where(starts < t_ctx, 0, 1 + (starts - t_ctx) // dec_seq).astype(jnp.int32)


def kernel(x_prompt, x_sample, state_ret, c, c_ctx, w_ada, b_ada, norm1_g, norm2_g, w_in, ret_decay_logit, w_ret_o, w_four, w_out, w_router, b_router, w1, b1, w2, b2, final_g):
    batch, seq, d = x_prompt.shape
    dec_batch, dec_seq, _ = x_sample.shape
    depth = w_in.shape[0]
    t_ctx, t_lat = batch * seq, dec_batch * dec_seq
    t = t_ctx + t_lat
    tm_in = min(1024, t_ctx, dec_seq)
    tm = min(512, t_ctx, dec_seq)
    assert t_ctx % tm_in == 0 and dec_seq % tm_in == 0 and t_ctx % dec_seq == 0

    n_cond = 16
    cond = jnp.zeros((n_cond, d), F32).at[0].set(c_ctx).at[1:1 + dec_batch].set(c)
    mod = _adaln_mod(cond, w_ada, b_ada).reshape(depth, n_cond, 6, d)
    mod = jnp.pad(mod, ((0, 0), (0, 0), (0, MOD_ROWS - 6), (0, 0)))
    grp_in = _tile_groups(t, t_ctx, dec_seq, tm_in)
    grp = _tile_groups(t, t_ctx, dec_seq, tm)
    grp_moe = _tile_groups(t, t_ctx, dec_seq, MOE_TILE)

    rot = _rotary_tables(dec_seq)
    gw = FOURIER_GROUP_W
    jc = jnp.arange(gw, dtype=jnp.int32)
    ang_c = ((jc[:, None] * jc[None, :]) % gw).astype(F32) * (2.0 * jnp.pi / gw)
    cs = jnp.concatenate([jnp.cos(ang_c), jnp.sin(ang_c)], axis=1).astype(BF16)
    dm_ctx = _split_dft_mats(seq, (seq * gw) ** -0.5)
    dm_lat = _split_dft_mats(dec_seq, (dec_seq * gw) ** -0.5)
    log_g = jax.nn.log_sigmoid(ret_decay_logit.astype(F32))
    wr, br = _router_operands(w_router, b_router)

    xs = [x_prompt.reshape(t_ctx, d), x_sample.reshape(t_lat, d)]
    new_state = None
    for l in range(depth):
        last = l == depth - 1
        u = _in_proj(xs, grp_in, mod[l], norm1_g[l], w_in[l].astype(BF16), tm_in, 3072)
        tables = _decay_tables(log_g[l], min(RET_CHUNK, seq, dec_seq))
        og, new_state = _retention(u, None, n=seq, nseq=batch, row0=0, tables=tables, rot=None, s0=None,
                                   layer=l, emit_state=True, heads=N_RET_HEADS, depth=depth,
                                   st_prev=new_state)
        og = _retention(u, og, n=dec_seq, nseq=dec_batch, row0=t_ctx, tables=tables, rot=rot,
                        s0=state_ret, layer=l, emit_state=False, heads=1)
        uf = _fourier(u, None, n=seq, nseq=batch, row0=0, cs=cs, mats=dm_ctx, tr=seq, tc=FOURIER_W)
        uf = _fourier(u, uf, n=dec_seq, nseq=dec_batch, row0=t_ctx, cs=cs, mats=dm_lat,
                      tr=min(512, dec_seq), tc=512)
        x, h2, lp, tw, cnt = _merge(
            xs, og, uf, u, grp, mod[l], norm2_g[l], w_ret_o[l].astype(BF16), w_four[l].astype(BF16),
            w_out[l].astype(BF16), wr[l], br[l], tm)
        lay = _segment_layout(cnt, t)
        xp = _dispatch(h2, lp, lay)
        yb = _experts(xp, lay, l, w1, b1, w2, b2)
        xs = _combine(x, yb, lp, tw, lay, grp_moe, mod[l], final_g, t_ctx, split_out=last)

    return (xs[0].reshape(batch, seq, d), xs[1].reshape(dec_batch, dec_seq, d), new_state)
```

```python
import functools

import jax
import jax.numpy as jnp
from jax import lax
from jax.experimental import pallas as pl
from jax.experimental.pallas import tpu as pltpu

F32 = jnp.float32
BF16 = jnp.bfloat16
U32 = jnp.uint32

N_RET_HEADS = 8
RET_DK = 128
RET_DV = 256
RET_QK_W = N_RET_HEADS * RET_DK
RET_V_W = N_RET_HEADS * RET_DV
N_FOURIER_GROUPS = 4
FOURIER_GROUP_W = 256
FOURIER_W = N_FOURIER_GROUPS * FOURIER_GROUP_W
GRID_W = 64
N_EXPERTS = 32
TOP_K = 4
SWIGLU_LIMIT = 7.0
SWIGLU_ALPHA = 1.702
ROPE_BASE = 10000.0
NORM_EPS = 1e-6
GN_EPS = 1e-5

COL_Q = 0
COL_K = COL_Q + RET_QK_W
COL_V = COL_K + RET_QK_W
COL_GRET = COL_V + RET_V_W
COL_FOUR = COL_GRET + RET_V_W
COL_GA = COL_FOUR + FOURIER_W

LANES = 128
VMEM_LIMIT_BYTES = 56 * 1024 * 1024

MOD_ROWS = 8
RET_CHUNK = 256
MOE_BLOCK = 512
MOE_TILE = 256
SEG_ALIGN = 8
SEG_ROWS = MOE_TILE * TOP_K + N_EXPERTS * SEG_ALIGN
E_PAD = LANES


def _cparams(sem, vmem=VMEM_LIMIT_BYTES):
    return pltpu.CompilerParams(dimension_semantics=sem, vmem_limit_bytes=vmem)


def _drop_ref(kern, idx):
    def wrapped(*refs):
        return kern(*refs[:idx], *refs[idx + 1:])
    return wrapped


def _mod_kernel(c_ref, w_ref, b_ref, o_ref):
    c = c_ref[...]
    s = c * jax.nn.sigmoid(c)
    o_ref[0] = jnp.dot(s, w_ref[0], precision=lax.Precision.HIGHEST,
                       preferred_element_type=F32) + b_ref[0]


def _adaln_mod(cond, w_ada, b_ada):
    depth, d, w6 = w_ada.shape
    r = cond.shape[0]
    tn = 1536
    assert w6 % tn == 0
    return pl.pallas_call(
        _mod_kernel,
        out_shape=jax.ShapeDtypeStruct((depth, r, w6), F32),
        grid=(depth, w6 // tn),
        in_specs=[pl.BlockSpec((r, d), lambda l, j: (0, 0)),
                  pl.BlockSpec((1, d, tn), lambda l, j: (l, 0, j)),
                  pl.BlockSpec((1, 1, tn), lambda l, j: (l, 0, j))],
        out_specs=pl.BlockSpec((1, r, tn), lambda l, j: (l, 0, j)),
        compiler_params=_cparams(("parallel", "parallel")),
        name="adaln_mod",
    )(cond, w_ada, b_ada.reshape(depth, 1, w6))


def _stream_specs(xs, tm, imap_args):
    d = xs[0].shape[1]
    if len(xs) == 1:
        return [pl.BlockSpec((tm, d), lambda i, *a: (i, 0))]
    nct = xs[0].shape[0] // tm
    return [pl.BlockSpec((tm, d), lambda i, *a: (jnp.minimum(i, nct - 1), 0)),
            pl.BlockSpec((tm, d), lambda i, *a: (jnp.maximum(i - nct, 0), 0))]


def _stream_tile(x_refs, i, nct):
    if len(x_refs) == 1:
        return x_refs[0][...]
    return jnp.where(i < nct, x_refs[0][...], x_refs[1][...])


def _rms_mod(x, g, shift, scale):
    y = x * lax.rsqrt(jnp.mean(x * x, axis=-1, keepdims=True) + NORM_EPS) * g
    return y * (1.0 + scale) + shift


def _in_proj_kernel(grp_ref, *refs, n_x, nct):
    del grp_ref
    x_refs = refs[:n_x]
    mod_ref, g_ref, w_ref, o_ref, h_ref = refs[n_x:]

    @pl.when(pl.program_id(1) == 0)
    def _():
        m = mod_ref[0]
        x = _stream_tile(x_refs, pl.program_id(0), nct)
        h_ref[...] = _rms_mod(x, g_ref[...], m[0:1], m[1:2]).astype(BF16)

    o_ref[...] = jnp.dot(h_ref[...], w_ref[...], preferred_element_type=F32).astype(BF16)


def _in_proj(xs, tile_grp, mod, g, w, tm, tn):
    t = sum(x.shape[0] for x in xs)
    d, n = w.shape
    assert t % tm == 0 and n % tn == 0 and all(x.shape[0] % tm == 0 for x in xs)
    x_specs = [pl.BlockSpec(s.block_shape, (lambda f: lambda i, j, grp: f(i))(s.index_map))
               for s in _stream_specs(xs, tm, 1)]
    grid_spec = pltpu.PrefetchScalarGridSpec(
        num_scalar_prefetch=1,
        grid=(t // tm, n // tn),
        in_specs=x_specs + [pl.BlockSpec((1, MOD_ROWS, d), lambda i, j, grp: (grp[i], 0, 0)),
                            pl.BlockSpec((1, d), lambda i, j, grp: (0, 0)),
                            pl.BlockSpec((d, tn), lambda i, j, grp: (0, j))],
        out_specs=pl.BlockSpec((tm, tn), lambda i, j, grp: (i, j)),
        scratch_shapes=[pltpu.VMEM((tm, d), BF16)],
    )
    return pl.pallas_call(
        functools.partial(_in_proj_kernel, n_x=len(xs), nct=xs[0].shape[0] // tm),
        out_shape=jax.ShapeDtypeStruct((t, n), BF16),
        grid_spec=grid_spec,
        compiler_params=_cparams(("parallel", "arbitrary")),
        name="in_proj",
    )(tile_grp, *xs, mod, g.reshape(1, d), w)


def _retention_kernel(cdec_ref, *refs, n, chunk, rotary, has_s0, emit_state, heads):
    for hh in range(heads):
        _retention_head(cdec_ref, refs, hh, n=n, chunk=chunk, rotary=rotary, has_s0=has_s0,
                        emit_state=emit_state, heads=heads)


def _retention_head(cdec_ref, refs, hh, *, n, chunk, rotary, has_s0, emit_state, heads):
    refs = list(refs)
    q_ref, k_ref, v_ref, g_ref, dmask_ref, dvec_ref, kdec_ref = refs[:7]
    pos = 7
    if rotary:
        cos_ref, sin_ref = refs[pos:pos + 2]
        pos += 2
    if has_s0:
        s0_ref = refs[pos]
        pos += 1
    o_ref = refs[pos]
    pos += 1
    if emit_state:
        sn_ref = refs[pos]
        pos += 1
    qs_ref, ks_ref, kv_ref, st_ref, s_ref = [ref.at[hh] for ref in refs[pos:pos + 5]]
    qcols = pl.ds(hh * RET_DK, RET_DK)
    vcols = pl.ds(hh * RET_DV, RET_DV)

    nc = n // chunk
    cross = has_s0 or nc > 1
    unroll = 16 if nc % 16 == 0 else (2 if nc % 2 == 0 else 1)
    h = pl.program_id(0) * heads + hh
    cdec_f = cdec_ref[2 * h]
    cdec_b = cdec_ref[2 * h + 1]
    fwd = pl.ds(0, RET_DK)
    bwd = pl.ds(RET_DK, RET_DK)

    def rows(c):
        return pl.ds(pl.multiple_of(c * chunk, chunk), chunk)

    def prep(c, carry):
        r = rows(c)
        q = q_ref[r, qcols].astype(F32) * (RET_DK ** -0.5)
        k = k_ref[r, qcols].astype(F32)
        if rotary:
            cs, sn = cos_ref[r, :], sin_ref[r, :]
            q = q * cs + pltpu.roll(q, RET_DK // 2, 1) * sn
            k = k * cs + pltpu.roll(k, RET_DK // 2, 1) * sn
        qs_ref[r, :] = q.astype(BF16)
        kt = k.T.astype(BF16)
        ks_ref[c] = kt
        ktf = kt.astype(F32)
        kdec = kdec_ref[hh]
        kd = jnp.concatenate([(ktf * kdec[0:1]).astype(BF16), (ktf * kdec[1:2]).astype(BF16)], axis=0)
        kv_ref[c] = jnp.dot(kd, v_ref[r, vcols], preferred_element_type=F32)
        return carry

    def scan(i, carry):
        cb = nc - 1 - i
        st_ref[i, fwd, :] = s_ref[fwd, :].astype(BF16)
        s_ref[fwd, :] = s_ref[fwd, :] * cdec_f + kv_ref[i, fwd, :]
        st_ref[cb, bwd, :] = s_ref[bwd, :].astype(BF16)
        s_ref[bwd, :] = s_ref[bwd, :] * cdec_b + kv_ref[cb, bwd, :]
        return carry

    def out_step(c, carry):
        r = rows(c)
        qc = qs_ref[r, :]
        sc = jnp.dot(qc, ks_ref[c], preferred_element_type=F32)
        o = jnp.dot((sc * dmask_ref[hh]).astype(BF16), v_ref[r, vcols], preferred_element_type=F32)
        if cross:
            qf = qc.astype(F32)
            ql = jnp.concatenate([(qf * dvec_ref[hh, 0]).astype(BF16),
                                  (qf * dvec_ref[hh, 2]).astype(BF16)], axis=1)
            o = o + jnp.dot(ql, st_ref[c], preferred_element_type=F32)
        mu = jnp.mean(o, axis=-1, keepdims=True)
        oc = o - mu
        var = jnp.mean(oc * oc, axis=-1, keepdims=True)
        gt = g_ref[r, vcols]
        o_ref[r, vcols] = (gt * jax.nn.sigmoid(gt)) * (oc * lax.rsqrt(var + GN_EPS)).astype(BF16)
        return carry

    def loop(body):
        if nc == 1:
            body(0, 0)
        else:
            lax.fori_loop(0, nc, body, 0, unroll=unroll)

    if has_s0:
        s_ref[fwd, :] = s0_ref[0, 0, 0, hh]
        s_ref[bwd, :] = s0_ref[0, 0, 1, hh]
    else:
        s_ref[...] = jnp.zeros_like(s_ref)
    loop(prep)
    if nc == 1:
        scan(0, 0)
    else:
        lax.fori_loop(0, nc, scan, 0, unroll=unroll)
    if emit_state:
        sn_ref[0, 0, 0, hh] = s_ref[fwd, :]
        sn_ref[0, 0, 1, hh] = s_ref[bwd, :]
    loop(out_step)


def _decay_tables(log_g, chunk):
    i = jnp.arange(chunk, dtype=F32)
    diff = i[:, None] - i[None, :]
    lf = log_g[0][:, None, None]
    lb = log_g[1][:, None, None]
    dmask = (jnp.where(diff >= 0, jnp.exp(jnp.maximum(diff, 0.0) * lf), 0.0)
             + jnp.where(diff <= 0, jnp.exp(jnp.maximum(-diff, 0.0) * lb), 0.0))
    lf2, lb2 = log_g[0][:, None], log_g[1][:, None]
    vecs = jnp.stack([jnp.exp((i + 1.0)[None, :] * lf2),
                      jnp.exp((chunk - 1.0 - i)[None, :] * lf2),
                      jnp.exp((chunk - i)[None, :] * lb2),
                      jnp.exp(i[None, :] * lb2)], axis=1)
    dvec = jnp.broadcast_to(vecs[..., None], vecs.shape + (LANES,))
    kdec = jnp.pad(vecs[:, 1::2], ((0, 0), (0, MOD_ROWS - 2), (0, 0)))
    cdec = jnp.exp(chunk * log_g).T.reshape(-1)
    return dmask, dvec, kdec, cdec


def _retention(u, og_prev, *, n, nseq, row0, tables, rot, s0, layer, emit_state, heads, depth=1, st_prev=None):
    t = u.shape[0]
    chunk = min(RET_CHUNK, n)
    assert n % chunk == 0 and row0 % n == 0 and N_RET_HEADS % heads == 0
    rb = row0 // n
    dmask, dvec, kdec, cdec = tables
    nc = n // chunk
    kw, vw = heads * RET_DK, heads * RET_DV
    qb, vb, gb = COL_K // kw, COL_V // vw, COL_GRET // vw
    in_specs = [pl.BlockSpec((n, kw), lambda h, b, cd: (rb + b, h)),
                pl.BlockSpec((n, kw), lambda h, b, cd: (rb + b, qb + h)),
                pl.BlockSpec((n, vw), lambda h, b, cd: (rb + b, vb + h)),
                pl.BlockSpec((n, vw), lambda h, b, cd: (rb + b, gb + h)),
                pl.BlockSpec((heads, chunk, chunk), lambda h, b, cd: (h, 0, 0)),
                pl.BlockSpec((heads, 4, chunk, LANES), lambda h, b, cd: (h, 0, 0, 0)),
                pl.BlockSpec((heads, MOD_ROWS, chunk), lambda h, b, cd: (h, 0, 0))]
    args = [u, u, u, u, dmask, dvec, kdec]
    if rot is not None:
        in_specs += [pl.BlockSpec((n, RET_DK), lambda h, b, cd: (0, 0))] * 2
        args += list(rot)
    if s0 is not None:
        in_specs.append(pl.BlockSpec((1, 1, 2, heads, RET_DK, RET_DV),
                                     lambda h, b, cd: (b, layer, 0, h, 0, 0)))
        args.append(s0)
    out_shape = [jax.ShapeDtypeStruct((t, RET_V_W), BF16)]
    out_specs = [pl.BlockSpec((n, vw), lambda h, b, cd: (rb + b, h))]
    if emit_state:
        out_shape.append(jax.ShapeDtypeStruct((nseq, depth, 2, N_RET_HEADS, RET_DK, RET_DV), F32))
        out_specs.append(pl.BlockSpec((1, 1, 2, heads, RET_DK, RET_DV),
                                      lambda h, b, cd: (b, layer, 0, h, 0, 0)))
    kern = functools.partial(_retention_kernel, n=n, chunk=chunk, rotary=rot is not None,
                             has_s0=s0 is not None, emit_state=emit_state, heads=heads)
    aliases = {}
    for prev, out_idx in ((og_prev, 0), (st_prev, 1)):
        if prev is not None:
            in_specs.append(pl.BlockSpec(memory_space=pl.ANY))
            args.append(prev)
            aliases[len(args)] = out_idx
            kern = _drop_ref(kern, len(args))
    grid_spec = pltpu.PrefetchScalarGridSpec(
        num_scalar_prefetch=1, grid=(N_RET_HEADS // heads, nseq), in_specs=in_specs, out_specs=out_specs,
        scratch_shapes=[pltpu.VMEM((heads, n, RET_DK), BF16), pltpu.VMEM((heads, nc, RET_DK, chunk), BF16),
                        pltpu.VMEM((heads, nc, 2 * RET_DK, RET_DV), F32),
                        pltpu.VMEM((heads, nc, 2 * RET_DK, RET_DV), BF16),
                        pltpu.VMEM((heads, 2 * RET_DK, RET_DV), F32)])
    outs = pl.pallas_call(
        kern, out_shape=out_shape, grid_spec=grid_spec, input_output_aliases=aliases,
        compiler_params=_cparams(("parallel", "parallel")),
        name="retention_lat" if rot is not None else "retention_ctx",
    )(cdec, *args)
    return outs if emit_state else outs[0]


def _rotary_tables(n):
    rows = n // GRID_W
    row = jnp.repeat(jnp.arange(rows), GRID_W).astype(F32)
    col = jnp.tile(jnp.arange(GRID_W), rows).astype(F32)
    n_freq = RET_DK // 4
    inv_freq = ROPE_BASE ** (-jnp.arange(n_freq, dtype=F32) / n_freq)
    ang = jnp.concatenate([row[:, None] * inv_freq, col[:, None] * inv_freq], axis=-1)
    cos, sin = jnp.cos(ang), jnp.sin(ang)
    return jnp.concatenate([cos, cos], axis=-1), jnp.concatenate([-sin, sin], axis=-1)


def _fourier_kernel(u_ref, cs_ref, ce_ref, se_ref, co_ref, so_ref, pint_ref, o_ref, fold_ref, *, n):
    h = n // 2
    gw = FOURIER_GROUP_W
    tc = u_ref.shape[1]
    rc = min(512, h)

    @pl.when(pl.program_id(2) == 0)
    def _():
        def fold_rows(r, carry):
            top = pl.ds(pl.multiple_of(r * rc, rc), rc)
            bot = pl.ds(pl.multiple_of(h + r * rc, rc), rc)
            for g in range(tc // gw):
                cols = pl.ds(g * gw, gw)
                zt = jnp.dot(u_ref[top, cols], cs_ref[...], preferred_element_type=F32)
                zb = jnp.dot(u_ref[bot, cols], cs_ref[...], preferred_element_type=F32)
                even, odd = zt + zb, zt - zb
                fold_ref[0, top, cols] = even[:, :gw].astype(BF16)
                fold_ref[1, top, cols] = even[:, gw:].astype(BF16)
                fold_ref[2, top, cols] = odd[:, :gw].astype(BF16)
                fold_ref[3, top, cols] = odd[:, gw:].astype(BF16)
            return carry

        if h == rc:
            fold_rows(0, 0)
        else:
            lax.fori_loop(0, h // rc, fold_rows, 0)

    ye = (jnp.dot(ce_ref[...], fold_ref[0], preferred_element_type=F32)
          + jnp.dot(se_ref[...], fold_ref[1], preferred_element_type=F32))
    yo = (jnp.dot(co_ref[...], fold_ref[2], preferred_element_type=F32)
          + jnp.dot(so_ref[...], fold_ref[3], preferred_element_type=F32))
    y = jnp.concatenate([ye, yo], axis=0).astype(BF16)
    o_ref[...] = jnp.dot(pint_ref[...], y, preferred_element_type=F32).astype(BF16)


def _fourier(u, uf_prev, *, n, nseq, row0, cs, mats, tr, tc):
    t = u.shape[0]
    h = n // 2
    assert n % tr == 0 and FOURIER_W % tc == 0 and tc % FOURIER_GROUP_W == 0 and row0 % n == 0
    rb = row0 // n
    nr = n // tr
    fb = COL_FOUR // tc
    half = pl.BlockSpec((tr // 2, h), lambda b, j, i: (i, 0))
    r = jnp.arange(tr, dtype=jnp.int32)
    pint = ((r // 2 + (r % 2) * (tr // 2))[:, None] == r[None, :]).astype(BF16)
    in_specs = [pl.BlockSpec((n, tc), lambda b, j, i: (rb + b, fb + j)),
                pl.BlockSpec(cs.shape, lambda b, j, i: (0, 0)),
                half, half, half, half,
                pl.BlockSpec((tr, tr), lambda b, j, i: (0, 0))]
    args = [u, cs, *mats, pint]
    aliases = {}
    kern = functools.partial(_fourier_kernel, n=n)
    if uf_prev is not None:
        in_specs.append(pl.BlockSpec(memory_space=pl.ANY))
        args.append(uf_prev)
        aliases = {len(args) - 1: 0}
        kern = _drop_ref(kern, len(args) - 1)
    return pl.pallas_call(
        kern,
        out_shape=jax.ShapeDtypeStruct((t, FOURIER_W), BF16),
        grid=(nseq, FOURIER_W // tc, nr),
        in_specs=in_specs,
        out_specs=pl.BlockSpec((tr, tc), lambda b, j, i: ((rb + b) * nr + i, j)),
        scratch_shapes=[pltpu.VMEM((4, h, tc), BF16)],
        input_output_aliases=aliases,
        compiler_params=_cparams(("parallel", "parallel", "arbitrary")),
        name="fourier",
    )(*args)


def _split_dft_mats(n, scale):
    h = n // 2
    assert n % 4 == 0
    jr = jnp.arange(h // 2, dtype=jnp.int32)
    m = jnp.arange(h, dtype=jnp.int32)
    sgn = jnp.where(m % 2 == 0, 1.0, -1.0).astype(F32)

    def mats(r):
        ang = ((r[:, None] * m[None, :]) % n).astype(F32) * (2.0 * jnp.pi / n)
        halves = jnp.stack([jnp.ones_like(sgn), sgn])[:, None, :]
        full = lambda top: (top[None] * halves).reshape(h, h).astype(BF16)
        return full(jnp.cos(ang) * scale), full(-jnp.sin(ang) * scale)

    ce, se = mats(2 * jr)
    co, so = mats(2 * jr + 1)
    return ce, se, co, so


def _merge_kernel(grp_ref, *refs, n_x, nct):
    del grp_ref
    x_refs = refs[:n_x]
    (og_ref, uf_ref, ga_ref, gb_ref, mod_ref, g2_ref, wro_ref, wf_ref, wo_ref, wr_ref, br_ref,
     xo_ref, h2_ref, lp_ref, tw_ref, cnt_ref) = refs[n_x:]
    tm = og_ref.shape[0]

    a = jnp.dot(og_ref[...], wro_ref[...], preferred_element_type=F32)
    b = jnp.dot(uf_ref[...], wf_ref[...], preferred_element_type=F32)
    merged = (jax.nn.sigmoid(ga_ref[...].astype(F32)) * a
              + jax.nn.sigmoid(gb_ref[...].astype(F32)) * b)
    y = jnp.dot(merged.astype(BF16), wo_ref[...], preferred_element_type=F32)
    m = mod_ref[0]
    x = _stream_tile(x_refs, pl.program_id(0), nct) + m[2:3] * y
    xo_ref[...] = x
    h2 = _rms_mod(x, g2_ref[...], m[3:4], m[4:5])
    h_hi = h2.astype(BF16)
    h2_ref[...] = h_hi
    h_lo = (h2 - h_hi.astype(F32)).astype(BF16)
    p_hi = jnp.dot(h_hi, wr_ref[...], preferred_element_type=F32)
    p_lo = jnp.dot(h_lo, wr_ref[...], preferred_element_type=F32)
    logits = p_hi + pltpu.roll(p_hi, E_PAD - N_EXPERTS, 1) + p_lo + br_ref[...]
    for s in range(tm // MOE_TILE):
        lp, tw, n = _route_tile(logits[s * MOE_TILE:(s + 1) * MOE_TILE])
        lp_ref[pl.ds(s * MOE_TILE, MOE_TILE), :] = lp
        tw_ref[pl.ds(s * MOE_TILE, MOE_TILE), :] = tw
        cnt_ref[s] = n


def _route_tile(logits):
    tt = logits.shape[0]
    lane = lax.broadcasted_iota(jnp.int32, (tt, E_PAD), 1)
    rest = logits
    vals, sels = [], []
    for _ in range(TOP_K):
        mx = jnp.max(rest, axis=-1, keepdims=True)
        idx = jnp.min(jnp.where(rest == mx, lane, E_PAD), axis=-1, keepdims=True)
        sel = lane == idx
        vals.append(mx)
        sels.append(sel)
        rest = jnp.where(sel, -jnp.inf, rest)
    exps = [jnp.exp(v - vals[0]) for v in vals]
    den = exps[0] + exps[1] + exps[2] + exps[3]

    picked = jnp.zeros((tt, E_PAD), F32)
    for sel in sels:
        picked = picked + sel.astype(F32)
    n = jnp.sum(picked, axis=0, keepdims=True).astype(jnp.int32)
    seg = ((n + (SEG_ALIGN - 1)) // SEG_ALIGN) * SEG_ALIGN
    er = lax.broadcasted_iota(jnp.int32, (E_PAD, E_PAD), 0)
    ec = lax.broadcasted_iota(jnp.int32, (E_PAD, E_PAD), 1)
    seg_rows = jnp.broadcast_to(seg.astype(F32), (SEG_ALIGN, E_PAD)).astype(BF16)
    off = jnp.dot(seg_rows, (er < ec).astype(BF16), preferred_element_type=F32)[0:1]
    ri = lax.broadcasted_iota(jnp.int32, (tt, tt), 0)
    ci = lax.broadcasted_iota(jnp.int32, (tt, tt), 1)
    before = jnp.dot((ri > ci).astype(BF16), picked.astype(BF16), preferred_element_type=F32) + off

    lp = jnp.zeros((tt, E_PAD), jnp.int32)
    tw = jnp.zeros((tt, E_PAD), F32)
    for k in range(TOP_K):
        row = jnp.sum(jnp.where(sels[k], before, 0.0), axis=-1, keepdims=True).astype(jnp.int32)
        lp = jnp.where(lane == k, row, lp)
        tw = jnp.where(lane == k, exps[k] / den, tw)
    return lp, tw, n


def _merge(xs, og, uf, u, tile_grp, mod, g2, wro, wf, wo, wr, br, tm):
    t, d = og.shape[0], wo.shape[1]
    assert tm % MOE_TILE == 0
    sub = tm // MOE_TILE
    ga_b, gb_b = COL_GA // d, COL_GA // d + 1
    full = lambda shape: pl.BlockSpec(shape, lambda i, grp: (0,) * len(shape))
    row = lambda w: pl.BlockSpec((tm, w), lambda i, grp: (i, 0))
    grid_spec = pltpu.PrefetchScalarGridSpec(
        num_scalar_prefetch=1,
        grid=(t // tm,),
        in_specs=_stream_specs(xs, tm, 1) + [
            row(RET_V_W), row(FOURIER_W),
            pl.BlockSpec((tm, d), lambda i, grp: (i, ga_b)),
            pl.BlockSpec((tm, d), lambda i, grp: (i, gb_b)),
            pl.BlockSpec((1, MOD_ROWS, d), lambda i, grp: (grp[i], 0, 0)),
            full((1, d)), full(wro.shape), full(wf.shape), full(wo.shape),
            full(wr.shape), full(br.shape)],
        out_specs=[row(d), row(d), row(E_PAD), row(E_PAD),
                   pl.BlockSpec((sub, 1, E_PAD), lambda i, grp: (i, 0, 0))],
    )
    return pl.pallas_call(
        functools.partial(_merge_kernel, n_x=len(xs), nct=xs[0].shape[0] // tm),
        out_shape=[jax.ShapeDtypeStruct((t, d), F32), jax.ShapeDtypeStruct((t, d), BF16),
                   jax.ShapeDtypeStruct((t, E_PAD), jnp.int32), jax.ShapeDtypeStruct((t, E_PAD), F32),
                   jax.ShapeDtypeStruct((t // MOE_TILE, 1, E_PAD), jnp.int32)],
        grid_spec=grid_spec,
        compiler_params=_cparams(("parallel",)),
        name="merge_route",
    )(tile_grp, *xs, og, uf, u, u, mod, g2.reshape(1, d), wro, wf, wo, wr, br)


def _router_operands(w_router, b_router):
    hi = w_router.astype(BF16)
    lo = (w_router - hi.astype(F32)).astype(BF16)
    wr = jnp.concatenate([hi, lo], axis=-1)
    wr = jnp.pad(wr, ((0, 0), (0, 0), (0, E_PAD - 2 * N_EXPERTS)))
    br = jnp.pad(b_router, ((0, 0), (0, E_PAD - N_EXPERTS)), constant_values=-1e30)
    return wr, br[:, None, :]


def _segment_copies(i, off_ref, seg_ref, gs_ref, local, remote, sem, to_remote):
    for e in range(N_EXPERTS):
        j = i * N_EXPERTS + e
        n = pl.multiple_of(seg_ref[j], SEG_ALIGN)

        @pl.when(n > 0)
        def _():
            lo = local.at[pl.ds(pl.multiple_of(off_ref[j], SEG_ALIGN), n)]
            re = remote.at[pl.ds(pl.multiple_of(gs_ref[j], SEG_ALIGN), n)]
            src, dst = (lo, re) if to_remote else (re, lo)
            pltpu.make_async_copy(src, dst, sem).start()


def _segment_wait(i, off_ref, seg_ref, local, remote, sem):
    j = i * N_EXPERTS + N_EXPERTS - 1
    tot = pl.multiple_of(off_ref[j] + seg_ref[j], SEG_ALIGN)
    pltpu.make_async_copy(local.at[pl.ds(0, tot)], remote.at[pl.ds(0, tot)], sem).wait()


def _pack_bf16_pairs(x):
    w = x.shape[1] // 2
    bits = lax.bitcast_convert_type(x, U32)
    return bits[:, :w] | (bits[:, w:] >> 16)


def _unpack_bf16_pairs(p):
    hi = lax.bitcast_convert_type(p & jnp.uint32(0xFFFF0000), F32)
    lo = lax.bitcast_convert_type(p << 16, F32)
    return jnp.concatenate([hi, lo], axis=1).astype(BF16)


def _local_row_iota(tt):
    return lax.broadcasted_iota(jnp.int32, (tt, SEG_ROWS), 1)


def _dispatch_kernel(off_ref, seg_ref, gs_ref, pend_ref, h_ref, lp_ref, xp_ref, xbuf_ref, zbuf_ref,
                     sem, zsem):
    i = pl.program_id(0)
    nt = pl.num_programs(0)
    slot = i % 2
    bm = zbuf_ref.shape[0]

    @pl.when(i == 0)
    def _():
        zbuf_ref[...] = jnp.zeros_like(zbuf_ref)

        def tail(e):
            start = jnp.where(e == 0, 0, pend_ref[jnp.maximum(e - 1, 0)])
            end = pend_ref[e]
            return end > start, xp_ref.at[pl.ds(pl.multiple_of(jnp.maximum(end - bm, 0), SEG_ALIGN), bm)]

        for e in range(N_EXPERTS):
            has, dst = tail(e)

            @pl.when(has)
            def _():
                pltpu.make_async_copy(zbuf_ref, dst, zsem).start()
        for e in range(N_EXPERTS):
            has, dst = tail(e)

            @pl.when(has)
            def _():
                pltpu.make_async_copy(zbuf_ref, dst, zsem).wait()

    lp = lp_ref[...]
    rid = _local_row_iota(lp.shape[0])
    hit = rid == lp[:, 0:1]
    for k in range(1, TOP_K):
        hit = hit | (rid == lp[:, k:k + 1])
    xbuf_ref[slot] = _pack_bf16_pairs(lax.dot_general(hit.astype(BF16), h_ref[...], (((0,), (0,)), ((), ())),
                                                      preferred_element_type=F32))
    _segment_copies(i, off_ref, seg_ref, gs_ref, xbuf_ref.at[slot], xp_ref, sem.at[slot], True)

    @pl.when(i > 0)
    def _():
        _segment_wait(i - 1, off_ref, seg_ref, xbuf_ref.at[1 - slot], xp_ref, sem.at[1 - slot])

    @pl.when(i == nt - 1)
    def _():
        _segment_wait(i, off_ref, seg_ref, xbuf_ref.at[slot], xp_ref, sem.at[slot])


def _dispatch(h2, lp, lay):
    t, d = h2.shape
    tt = MOE_TILE
    grid_spec = pltpu.PrefetchScalarGridSpec(
        num_scalar_prefetch=4,
        grid=(t // tt,),
        in_specs=[pl.BlockSpec((tt, d), lambda i, *s: (i, 0)),
                  pl.BlockSpec((tt, E_PAD), lambda i, *s: (i, 0))],
        out_specs=pl.BlockSpec(memory_space=pl.ANY),
        scratch_shapes=[pltpu.VMEM((2, SEG_ROWS, d // 2), U32), pltpu.VMEM((MOE_BLOCK, d // 2), U32),
                        pltpu.SemaphoreType.DMA((2,)), pltpu.SemaphoreType.DMA],
    )
    return pl.pallas_call(
        _dispatch_kernel,
        out_shape=jax.ShapeDtypeStruct((lay["p_total"], d // 2), U32),
        grid_spec=grid_spec,
        compiler_params=_cparams(("arbitrary",)),
        name="moe_dispatch",
    )(lay["off"], lay["seg"], lay["gseg"], lay["p_end"], h2, lp)


def _expert_kernel(be_ref, nu_ref, nxt_ref, x_ref, w1_ref, b1_ref, w2_ref, b2_ref, y_ref,
                   w1s_ref, w2s_ref, w1b_ref, w2b_ref, sem, *, layer):
    i = pl.program_id(0)
    de = w2b_ref.shape[0]
    used = i < nu_ref[0]
    e = be_ref[i]

    def fetch(ex):
        return (pltpu.make_async_copy(w1_ref.at[layer, ex], w1s_ref, sem.at[0]),
                pltpu.make_async_copy(w2_ref.at[layer, ex], w2s_ref, sem.at[1]))

    @pl.when(i == 0)
    def _():
        for cp in fetch(e):
            cp.start()

    @pl.when(used & ((i == 0) | (e != be_ref[jnp.maximum(i - 1, 0)])))
    def _():
        for cp in fetch(e):
            cp.wait()
        w1b_ref[...] = w1s_ref[...].astype(BF16)
        w2b_ref[...] = w2s_ref[...].astype(BF16)
        nxt = nxt_ref[e]

        @pl.when(nxt >= 0)
        def _():
            for cp in fetch(nxt):
                cp.start()

    @pl.when(used)
    def _():
        hh = jnp.dot(_unpack_bf16_pairs(x_ref[...]), w1b_ref[...], preferred_element_type=F32) + b1_ref[0, 0]
        gate = jnp.minimum(hh[:, :de], SWIGLU_LIMIT)
        lin = jnp.clip(hh[:, de:], -SWIGLU_LIMIT, SWIGLU_LIMIT)
        act = (lin + 1.0) * (gate * jax.nn.sigmoid(SWIGLU_ALPHA * gate))
        y = jnp.dot(act.astype(BF16), w2b_ref[...], preferred_element_type=F32) + b2_ref[0, 0]
        y_ref[...] = _pack_bf16_pairs(y.astype(BF16).astype(F32))

    @pl.when(jnp.logical_not(used))
    def _():
        y_ref[...] = jnp.zeros_like(y_ref)


def _experts(xp, lay, layer, w1, b1, w2, b2):
    p_total, dh = xp.shape
    depth, ne, d, de2 = w1.shape
    assert d == 2 * dh
    de = w2.shape[2]
    bm = MOE_BLOCK
    bmap = lambda i, be, nu, nxt: (layer, be[i], 0, 0)
    grid_spec = pltpu.PrefetchScalarGridSpec(
        num_scalar_prefetch=3,
        grid=(p_total // bm,),
        in_specs=[pl.BlockSpec((bm, dh), lambda i, be, nu, nxt: (jnp.minimum(i, nu[0] - 1), 0)),
                  pl.BlockSpec(memory_space=pl.ANY),
                  pl.BlockSpec((1, 1, 1, de2), bmap),
                  pl.BlockSpec(memory_space=pl.ANY),
                  pl.BlockSpec((1, 1, 1, d), bmap)],
        out_specs=pl.BlockSpec((bm, dh), lambda i, be, nu, nxt: (i, 0)),
        scratch_shapes=[pltpu.VMEM((d, de2), F32), pltpu.VMEM((de, d), F32),
                        pltpu.VMEM((d, de2), BF16), pltpu.VMEM((de, d), BF16),
                        pltpu.SemaphoreType.DMA((2,))],
    )
    return pl.pallas_call(
        functools.partial(_expert_kernel, layer=layer),
        out_shape=jax.ShapeDtypeStruct((p_total, dh), U32),
        grid_spec=grid_spec,
        compiler_params=_cparams(("arbitrary",)),
        name="moe_experts",
    )(lay["blk_e"], lay["n_used"], lay["nxt"], xp, w1, b1.reshape(depth, ne, 1, de2), w2,
      b2.reshape(depth, ne, 1, d))


def _combine_kernel(grp_ref, off_ref, seg_ref, gs_ref, lp_ref, x_ref, tw_ref, mod_ref, fg_ref, yb_ref,
                    *refs, split_out, nct):
    del grp_ref
    o_refs = refs[:2] if split_out else refs[:1]
    ybuf_ref, sem = refs[len(o_refs):]
    i = pl.program_id(0)
    nt = pl.num_programs(0)
    slot = i % 2

    def gather(tile, s):
        _segment_copies(tile, off_ref, seg_ref, gs_ref, ybuf_ref.at[s], yb_ref, sem.at[s], False)

    @pl.when(i == 0)
    def _():
        ybuf_ref[...] = jnp.zeros_like(ybuf_ref)
        gather(0, 0)

    @pl.when(i + 1 < nt)
    def _():
        gather(i + 1, 1 - slot)

    _segment_wait(i, off_ref, seg_ref, ybuf_ref.at[slot], yb_ref, sem.at[slot])
    lp = lp_ref[...]
    tw = tw_ref[...]
    rid = _local_row_iota(lp.shape[0])
    wsel = jnp.where(rid == lp[:, 0:1], tw[:, 0:1], 0.0)
    for k in range(1, TOP_K):
        wsel = jnp.where(rid == lp[:, k:k + 1], tw[:, k:k + 1], wsel)
    acc = jnp.dot(wsel.astype(BF16), _unpack_bf16_pairs(ybuf_ref[slot]), preferred_element_type=F32)
    x = x_ref[...] + mod_ref[0][5:6] * acc
    if not split_out:
        o_refs[0][...] = x
    else:
        y = x * lax.rsqrt(jnp.mean(x * x, axis=-1, keepdims=True) + NORM_EPS) * fg_ref[...]

        @pl.when(i < nct)
        def _():
            o_refs[0][...] = y

        @pl.when(i >= nct)
        def _():
            o_refs[1][...] = y


def _combine(x, yb, lp, tw, lay, tile_grp, mod, final_g, t_ctx, split_out):
    t, d = x.shape
    tt = MOE_TILE
    nct = t_ctx // tt
    if split_out:
        out_shape = [jax.ShapeDtypeStruct((t_ctx, d), F32), jax.ShapeDtypeStruct((t - t_ctx, d), F32)]
        out_specs = [pl.BlockSpec((tt, d), lambda i, *s: (jnp.minimum(i, nct - 1), 0)),
                     pl.BlockSpec((tt, d), lambda i, *s: (jnp.maximum(i - nct, 0), 0))]
    else:
        out_shape = [jax.ShapeDtypeStruct((t, d), F32)]
        out_specs = [pl.BlockSpec((tt, d), lambda i, *s: (i, 0))]
    grid_spec = pltpu.PrefetchScalarGridSpec(
        num_scalar_prefetch=4,
        grid=(t // tt,),
        in_specs=[pl.BlockSpec((tt, E_PAD), lambda i, *s: (i, 0)),
                  pl.BlockSpec((tt, d), lambda i, *s: (i, 0)),
                  pl.BlockSpec((tt, E_PAD), lambda i, *s: (i, 0)),
                  pl.BlockSpec((1, MOD_ROWS, d), lambda i, grp, *s: (grp[i], 0, 0)),
                  pl.BlockSpec((1, d), lambda i, *s: (0, 0)),
                  pl.BlockSpec(memory_space=pl.ANY)],
        out_specs=out_specs,
        scratch_shapes=[pltpu.VMEM((2, SEG_ROWS, d // 2), U32), pltpu.SemaphoreType.DMA((2,))],
    )
    return pl.pallas_call(
        functools.partial(_combine_kernel, split_out=split_out, nct=nct),
        out_shape=out_shape,
        grid_spec=grid_spec,
        compiler_params=_cparams(("arbitrary",)),
        name="moe_combine",
    )(tile_grp, lay["off"], lay["seg"], lay["gseg"], lp, x, tw, mod, final_g.reshape(1, d), yb)


def _segment_layout(cnt, t):
    bm = MOE_BLOCK
    nt = t // MOE_TILE
    n = cnt[:, 0, :N_EXPERTS]
    seg = ((n + (SEG_ALIGN - 1)) // SEG_ALIGN) * SEG_ALIGN
    off = jnp.cumsum(seg, axis=1) - seg
    tot = jnp.sum(seg, axis=0)
    padded = ((tot + bm - 1) // bm) * bm
    p_end = jnp.cumsum(padded)
    gseg = (p_end - padded)[None, :] + jnp.cumsum(seg, axis=0) - seg
    max_rows = t * TOP_K + nt * N_EXPERTS * (SEG_ALIGN - 1)
    n_blk = -(-max_rows // bm) + N_EXPERTS
    blk_start = jnp.arange(n_blk, dtype=jnp.int32) * bm
    blk_e = jnp.minimum(jnp.sum(p_end[None, :] <= blk_start[:, None], axis=1), N_EXPERTS - 1)
    ex = jnp.arange(N_EXPERTS, dtype=jnp.int32)
    later = (ex[None, :] > ex[:, None]) & (tot[None, :] > 0)
    nxt = jnp.where(jnp.any(later, axis=1), jnp.argmax(later, axis=1), -1)
    i32 = lambda a: a.reshape(-1).astype(jnp.int32)
    return dict(seg=i32(seg), off=i32(off), gseg=i32(gseg), p_end=i32(p_end), blk_e=i32(blk_e),
                n_used=i32(p_end[-1:] // bm), nxt=i32(nxt), p_total=n_blk * bm)


def _tile_groups(t, t_ctx, dec_seq, tm):
    starts = jnp.arange(t // tm, dtype=jnp.int32) * tm
    return jnp.where(starts < t_ctx, 0, 1 + (starts - t_ctx) // dec_seq).astype(jnp.int32)


def kernel(x_prompt, x_sample, state_ret, c, c_ctx, w_ada, b_ada, norm1_g, norm2_g, w_in, ret_decay_logit, w_ret_o, w_four, w_out, w_router, b_router, w1, b1, w2, b2, final_g):
    batch, seq, d = x_prompt.shape
    dec_batch, dec_seq, _ = x_sample.shape
    depth = w_in.shape[0]
    t_ctx, t_lat = batch * seq, dec_batch * dec_seq
    t = t_ctx + t_lat
    tm_in = min(1024, t_ctx, dec_seq)
    tm = min(512, t_ctx, dec_seq)
    assert t_ctx % tm_in == 0 and dec_seq % tm_in == 0 and t_ctx % dec_seq == 0

    n_cond = 16
    cond = jnp.zeros((n_cond, d), F32).at[0].set(c_ctx).at[1:1 + dec_batch].set(c)
    mod = _adaln_mod(cond, w_ada, b_ada).reshape(depth, n_cond, 6, d)
    mod = jnp.pad(mod, ((0, 0), (0, 0), (0, MOD_ROWS - 6), (0, 0)))
    grp_in = _tile_groups(t, t_ctx, dec_seq, tm_in)
    grp = _tile_groups(t, t_ctx, dec_seq, tm)
    grp_moe = _tile_groups(t, t_ctx, dec_seq, MOE_TILE)

    rot = _rotary_tables(dec_seq)
    gw = FOURIER_GROUP_W
    jc = jnp.arange(gw, dtype=jnp.int32)
    ang_c = ((jc[:, None] * jc[None, :]) % gw).astype(F32) * (2.0 * jnp.pi / gw)
    cs = jnp.concatenate([jnp.cos(ang_c), jnp.sin(ang_c)], axis=1).astype(BF16)
    dm_ctx = _split_dft_mats(seq, (seq * gw) ** -0.5)
    dm_lat = _split_dft_mats(dec_seq, (dec_seq * gw) ** -0.5)
    log_g = jax.nn.log_sigmoid(ret_decay_logit.astype(F32))
    wr, br = _router_operands(w_router, b_router)

    xs = [x_prompt.reshape(t_ctx, d), x_sample.reshape(t_lat, d)]
    new_state = None
    for l in range(depth):
        last = l == depth - 1
        u = _in_proj(xs, grp_in, mod[l], norm1_g[l], w_in[l].astype(BF16), tm_in, 3072)
        tables = _decay_tables(log_g[l], min(RET_CHUNK, seq, dec_seq))
        og, new_state = _retention(u, None, n=seq, nseq=batch, row0=0, tables=tables, rot=None, s0=None,
                                   layer=l, emit_state=True, heads=N_RET_HEADS, depth=depth,
                                   st_prev=new_state)
        og = _retention(u, og, n=dec_seq, nseq=dec_batch, row0=t_ctx, tables=tables, rot=rot,
                        s0=state_ret, layer=l, emit_state=False, heads=1)
        uf = _fourier(u, None, n=seq, nseq=batch, row0=0, cs=cs, mats=dm_ctx, tr=seq, tc=FOURIER_W)
        uf = _fourier(u, uf, n=dec_seq, nseq=dec_batch, row0=t_ctx, cs=cs, mats=dm_lat,
                      tr=min(512, dec_seq), tc=512)
        x, h2, lp, tw, cnt = _merge(
            xs, og, uf, u, grp, mod[l], norm2_g[l], w_ret_o[l].astype(BF16), w_four[l].astype(BF16),
            w_out[l].astype(BF16), wr[l], br[l], tm)
        lay = _segment_layout(cnt, t)
        xp = _dispatch(h2, lp, lay)
        yb = _experts(xp, lay, l, w1, b1, w2, b2)
        xs = _combine(x, yb, lp, tw, lay, grp_moe, mod[l], final_g, t_ctx, split_out=last)

    return (xs[0].reshape(batch, seq, d), xs[1].reshape(dec_batch, dec_seq, d), new_state)
```

```python
import functools

import jax
import jax.numpy as jnp
from jax import lax
from jax.experimental import pallas as pl
from jax.experimental.pallas import tpu as pltpu

F32 = jnp.float32
BF16 = jnp.bfloat16
U32 = jnp.uint32

N_RET_HEADS = 8
RET_DK = 128
RET_DV = 256
RET_QK_W = N_RET_HEADS * RET_DK
RET_V_W = N_RET_HEADS * RET_DV
N_FOURIER_GROUPS = 4
FOURIER_GROUP_W = 256
FOURIER_W = N_FOURIER_GROUPS * FOURIER_GROUP_W
GRID_W = 64
N_EXPERTS = 32
TOP_K = 4
SWIGLU_LIMIT = 7.0
SWIGLU_ALPHA = 1.702
ROPE_BASE = 10000.0
NORM_EPS = 1e-6
GN_EPS = 1e-5

COL_Q = 0
COL_K = COL_Q + RET_QK_W
COL_V = COL_K + RET_QK_W
COL_GRET = COL_V + RET_V_W
COL_FOUR = COL_GRET + RET_V_W
COL_GA = COL_FOUR + FOURIER_W

LANES = 128
VMEM_LIMIT_BYTES = 56 * 1024 * 1024

MOD_ROWS = 8
RET_CHUNK = 256
MOE_BLOCK = 512
MOE_TILE = 256
SEG_ALIGN = 8
SEG_ROWS = MOE_TILE * TOP_K + N_EXPERTS * SEG_ALIGN
E_PAD = LANES


def _cparams(sem, vmem=VMEM_LIMIT_BYTES):
    return pltpu.CompilerParams(dimension_semantics=sem, vmem_limit_bytes=vmem)


def _drop_ref(kern, idx):
    def wrapped(*refs):
        return kern(*refs[:idx], *refs[idx + 1:])
    return wrapped


def _mod_kernel(c_ref, w_ref, b_ref, o_ref):
    c = c_ref[...]
    s = c * jax.nn.sigmoid(c)
    o_ref[0] = jnp.dot(s, w_ref[0], precision=lax.Precision.HIGHEST,
                       preferred_element_type=F32) + b_ref[0]


def _adaln_mod(cond, w_ada, b_ada):
    depth, d, w6 = w_ada.shape
    r = cond.shape[0]
    tn = 1536
    assert w6 % tn == 0
    return pl.pallas_call(
        _mod_kernel,
        out_shape=jax.ShapeDtypeStruct((depth, r, w6), F32),
        grid=(depth, w6 // tn),
        in_specs=[pl.BlockSpec((r, d), lambda l, j: (0, 0)),
                  pl.BlockSpec((1, d, tn), lambda l, j: (l, 0, j)),
                  pl.BlockSpec((1, 1, tn), lambda l, j: (l, 0, j))],
        out_specs=pl.BlockSpec((1, r, tn), lambda l, j: (l, 0, j)),
        compiler_params=_cparams(("parallel", "parallel")),
        name="adaln_mod",
    )(cond, w_ada, b_ada.reshape(depth, 1, w6))


def _stream_specs(xs, tm, imap_args):
    d = xs[0].shape[1]
    if len(xs) == 1:
        return [pl.BlockSpec((tm, d), lambda i, *a: (i, 0))]
    nct = xs[0].shape[0] // tm
    return [pl.BlockSpec((tm, d), lambda i, *a: (jnp.minimum(i, nct - 1), 0)),
            pl.BlockSpec((tm, d), lambda i, *a: (jnp.maximum(i - nct, 0), 0))]


def _stream_tile(x_refs, i, nct):
    if len(x_refs) == 1:
        return x_refs[0][...]
    return jnp.where(i < nct, x_refs[0][...], x_refs[1][...])


def _rms_mod(x, g, shift, scale):
    y = x * lax.rsqrt(jnp.mean(x * x, axis=-1, keepdims=True) + NORM_EPS) * g
    return y * (1.0 + scale) + shift


def _in_proj_kernel(grp_ref, *refs, n_x, nct):
    del grp_ref
    x_refs = refs[:n_x]
    mod_ref, g_ref, w_ref, o_ref, h_ref = refs[n_x:]

    @pl.when(pl.program_id(1) == 0)
    def _():
        m = mod_ref[0]
        x = _stream_tile(x_refs, pl.program_id(0), nct)
        h_ref[...] = _rms_mod(x, g_ref[...], m[0:1], m[1:2]).astype(BF16)

    o_ref[...] = jnp.dot(h_ref[...], w_ref[...], preferred_element_type=F32).astype(BF16)


def _in_proj(xs, tile_grp, mod, g, w, tm, tn):
    t = sum(x.shape[0] for x in xs)
    d, n = w.shape
    assert t % tm == 0 and n % tn == 0 and all(x.shape[0] % tm == 0 for x in xs)
    x_specs = [pl.BlockSpec(s.block_shape, (lambda f: lambda i, j, grp: f(i))(s.index_map))
               for s in _stream_specs(xs, tm, 1)]
    grid_spec = pltpu.PrefetchScalarGridSpec(
        num_scalar_prefetch=1,
        grid=(t // tm, n // tn),
        in_specs=x_specs + [pl.BlockSpec((1, MOD_ROWS, d), lambda i, j, grp: (grp[i], 0, 0)),
                            pl.BlockSpec((1, d), lambda i, j, grp: (0, 0)),
                            pl.BlockSpec((d, tn), lambda i, j, grp: (0, j))],
        out_specs=pl.BlockSpec((tm, tn), lambda i, j, grp: (i, j)),
        scratch_shapes=[pltpu.VMEM((tm, d), BF16)],
    )
    return pl.pallas_call(
        functools.partial(_in_proj_kernel, n_x=len(xs), nct=xs[0].shape[0] // tm),
        out_shape=jax.ShapeDtypeStruct((t, n), BF16),
        grid_spec=grid_spec,
        compiler_params=_cparams(("parallel", "arbitrary")),
        name="in_proj",
    )(tile_grp, *xs, mod, g.reshape(1, d), w)


def _retention_kernel(cdec_ref, *refs, n, chunk, rotary, has_s0, emit_state, heads):
    for hh in range(heads):
        _retention_head(cdec_ref, refs, hh, n=n, chunk=chunk, rotary=rotary, has_s0=has_s0,
                        emit_state=emit_state, heads=heads)


def _retention_head(cdec_ref, refs, hh, *, n, chunk, rotary, has_s0, emit_state, heads):
    refs = list(refs)
    q_ref, k_ref, v_ref, g_ref, dmask_ref, dvec_ref, kdec_ref = refs[:7]
    pos = 7
    if rotary:
        cos_ref, sin_ref = refs[pos:pos + 2]
        pos += 2
    if has_s0:
        s0_ref = refs[pos]
        pos += 1
    o_ref = refs[pos]
    pos += 1
    if emit_state:
        sn_ref = refs[pos]
        pos += 1
    qs_ref, ks_ref, kv_ref, st_ref, s_ref = [ref.at[hh] for ref in refs[pos:pos + 5]]
    qcols = pl.ds(hh * RET_DK, RET_DK)
    vcols = pl.ds(hh * RET_DV, RET_DV)

    nc = n // chunk
    cross = has_s0 or nc > 1
    unroll = 16 if nc % 16 == 0 else (2 if nc % 2 == 0 else 1)
    h = pl.program_id(0) * heads + hh
    cdec_f = cdec_ref[2 * h]
    cdec_b = cdec_ref[2 * h + 1]
    fwd = pl.ds(0, RET_DK)
    bwd = pl.ds(RET_DK, RET_DK)

    def rows(c):
        return pl.ds(pl.multiple_of(c * chunk, chunk), chunk)

    def prep(c, carry):
        r = rows(c)
        q = q_ref[r, qcols].astype(F32) * (RET_DK ** -0.5)
        k = k_ref[r, qcols].astype(F32)
        if rotary:
            cs, sn = cos_ref[r, :], sin_ref[r, :]
            q = q * cs + pltpu.roll(q, RET_DK // 2, 1) * sn
            k = k * cs + pltpu.roll(k, RET_DK // 2, 1) * sn
        qs_ref[r, :] = q.astype(BF16)
        kt = k.T.astype(BF16)
        ks_ref[c] = kt
        ktf = kt.astype(F32)
        kdec = kdec_ref[hh]
        kd = jnp.concatenate([(ktf * kdec[0:1]).astype(BF16), (ktf * kdec[1:2]).astype(BF16)], axis=0)
        kv_ref[c] = jnp.dot(kd, v_ref[r, vcols], preferred_element_type=F32)
        return carry

    def scan(i, carry):
        cb = nc - 1 - i
        st_ref[i, fwd, :] = s_ref[fwd, :].astype(BF16)
        s_ref[fwd, :] = s_ref[fwd, :] * cdec_f + kv_ref[i, fwd, :]
        st_ref[cb, bwd, :] = s_ref[bwd, :].astype(BF16)
        s_ref[bwd, :] = s_ref[bwd, :] * cdec_b + kv_ref[cb, bwd, :]
        return carry

    def out_step(c, carry):
        r = rows(c)
        qc = qs_ref[r, :]
        sc = jnp.dot(qc, ks_ref[c], preferred_element_type=F32)
        o = jnp.dot((sc * dmask_ref[hh]).astype(BF16), v_ref[r, vcols], preferred_element_type=F32)
        if cross:
            qf = qc.astype(F32)
            ql = jnp.concatenate([(qf * dvec_ref[hh, 0]).astype(BF16),
                                  (qf * dvec_ref[hh, 2]).astype(BF16)], axis=1)
            o = o + jnp.dot(ql, st_ref[c], preferred_element_type=F32)
        mu = jnp.mean(o, axis=-1, keepdims=True)
        oc = o - mu
        var = jnp.mean(oc * oc, axis=-1, keepdims=True)
        gt = g_ref[r, vcols]
        o_ref[r, vcols] = (gt * jax.nn.sigmoid(gt)) * (oc * lax.rsqrt(var + GN_EPS)).astype(BF16)
        return carry

    def loop(body):
        if nc == 1:
            body(0, 0)
        else:
            lax.fori_loop(0, nc, body, 0, unroll=unroll)

    if has_s0:
        s_ref[fwd, :] = s0_ref[0, 0, 0, hh]
        s_ref[bwd, :] = s0_ref[0, 0, 1, hh]
    else:
        s_ref[...] = jnp.zeros_like(s_ref)
    loop(prep)
    if nc == 1:
        scan(0, 0)
    else:
        lax.fori_loop(0, nc, scan, 0, unroll=unroll)
    if emit_state:
        sn_ref[0, 0, 0, hh] = s_ref[fwd, :]
        sn_ref[0, 0, 1, hh] = s_ref[bwd, :]
    loop(out_step)


def _decay_tables(log_g, chunk):
    i = jnp.arange(chunk, dtype=F32)
    diff = i[:, None] - i[None, :]
    lf = log_g[0][:, None, None]
    lb = log_g[1][:, None, None]
    dmask = (jnp.where(diff >= 0, jnp.exp(jnp.maximum(diff, 0.0) * lf), 0.0)
             + jnp.where(diff <= 0, jnp.exp(jnp.maximum(-diff, 0.0) * lb), 0.0))
    lf2, lb2 = log_g[0][:, None], log_g[1][:, None]
    vecs = jnp.stack([jnp.exp((i + 1.0)[None, :] * lf2),
                      jnp.exp((chunk - 1.0 - i)[None, :] * lf2),
                      jnp.exp((chunk - i)[None, :] * lb2),
                      jnp.exp(i[None, :] * lb2)], axis=1)
    dvec = jnp.broadcast_to(vecs[..., None], vecs.shape + (LANES,))
    kdec = jnp.pad(vecs[:, 1::2], ((0, 0), (0, MOD_ROWS - 2), (0, 0)))
    cdec = jnp.exp(chunk * log_g).T.reshape(-1)
    return dmask, dvec, kdec, cdec


def _retention(u, og_prev, *, n, nseq, row0, tables, rot, s0, layer, emit_state, heads, depth=1, st_prev=None):
    t = u.shape[0]
    chunk = min(RET_CHUNK, n)
    assert n % chunk == 0 and row0 % n == 0 and N_RET_HEADS % heads == 0
    rb = row0 // n
    dmask, dvec, kdec, cdec = tables
    nc = n // chunk
    kw, vw = heads * RET_DK, heads * RET_DV
    qb, vb, gb = COL_K // kw, COL_V // vw, COL_GRET // vw
    in_specs = [pl.BlockSpec((n, kw), lambda h, b, cd: (rb + b, h)),
                pl.BlockSpec((n, kw), lambda h, b, cd: (rb + b, qb + h)),
                pl.BlockSpec((n, vw), lambda h, b, cd: (rb + b, vb + h)),
                pl.BlockSpec((n, vw), lambda h, b, cd: (rb + b, gb + h)),
                pl.BlockSpec((heads, chunk, chunk), lambda h, b, cd: (h, 0, 0)),
                pl.BlockSpec((heads, 4, chunk, LANES), lambda h, b, cd: (h, 0, 0, 0)),
                pl.BlockSpec((heads, MOD_ROWS, chunk), lambda h, b, cd: (h, 0, 0))]
    args = [u, u, u, u, dmask, dvec, kdec]
    if rot is not None:
        in_specs += [pl.BlockSpec((n, RET_DK), lambda h, b, cd: (0, 0))] * 2
        args += list(rot)
    if s0 is not None:
        in_specs.append(pl.BlockSpec((1, 1, 2, heads, RET_DK, RET_DV),
                                     lambda h, b, cd: (b, layer, 0, h, 0, 0)))
        args.append(s0)
    out_shape = [jax.ShapeDtypeStruct((t, RET_V_W), BF16)]
    out_specs = [pl.BlockSpec((n, vw), lambda h, b, cd: (rb + b, h))]
    if emit_state:
        out_shape.append(jax.ShapeDtypeStruct((nseq, depth, 2, N_RET_HEADS, RET_DK, RET_DV), F32))
        out_specs.append(pl.BlockSpec((1, 1, 2, heads, RET_DK, RET_DV),
                                      lambda h, b, cd: (b, layer, 0, h, 0, 0)))
    kern = functools.partial(_retention_kernel, n=n, chunk=chunk, rotary=rot is not None,
                             has_s0=s0 is not None, emit_state=emit_state, heads=heads)
    aliases = {}
    for prev, out_idx in ((og_prev, 0), (st_prev, 1)):
        if prev is not None:
            in_specs.append(pl.BlockSpec(memory_space=pl.ANY))
            args.append(prev)
            aliases[len(args)] = out_idx
            kern = _drop_ref(kern, len(args))
    grid_spec = pltpu.PrefetchScalarGridSpec(
        num_scalar_prefetch=1, grid=(N_RET_HEADS // heads, nseq), in_specs=in_specs, out_specs=out_specs,
        scratch_shapes=[pltpu.VMEM((heads, n, RET_DK), BF16), pltpu.VMEM((heads, nc, RET_DK, chunk), BF16),
                        pltpu.VMEM((heads, nc, 2 * RET_DK, RET_DV), F32),
                        pltpu.VMEM((heads, nc, 2 * RET_DK, RET_DV), BF16),
                        pltpu.VMEM((heads, 2 * RET_DK, RET_DV), F32)])
    outs = pl.pallas_call(
        kern, out_shape=out_shape, grid_spec=grid_spec, input_output_aliases=aliases,
        compiler_params=_cparams(("parallel", "parallel")),
        name="retention_lat" if rot is not None else "retention_ctx",
    )(cdec, *args)
    return outs if emit_state else outs[0]


def _rotary_tables(n):
    rows = n // GRID_W
    row = jnp.repeat(jnp.arange(rows), GRID_W).astype(F32)
    col = jnp.tile(jnp.arange(GRID_W), rows).astype(F32)
    n_freq = RET_DK // 4
    inv_freq = ROPE_BASE ** (-jnp.arange(n_freq, dtype=F32) / n_freq)
    ang = jnp.concatenate([row[:, None] * inv_freq, col[:, None] * inv_freq], axis=-1)
    cos, sin = jnp.cos(ang), jnp.sin(ang)
    return jnp.concatenate([cos, cos], axis=-1), jnp.concatenate([-sin, sin], axis=-1)


def _fourier_kernel(u_ref, cs_ref, ce_ref, se_ref, co_ref, so_ref, pint_ref, o_ref, fold_ref, *, n):
    h = n // 2
    gw = FOURIER_GROUP_W
    tc = u_ref.shape[1]
    rc = min(512, h)

    @pl.when(pl.program_id(2) == 0)
    def _():
        def fold_rows(r, carry):
            top = pl.ds(pl.multiple_of(r * rc, rc), rc)
            bot = pl.ds(pl.multiple_of(h + r * rc, rc), rc)
            for g in range(tc // gw):
                cols = pl.ds(g * gw, gw)
                zt = jnp.dot(u_ref[top, cols], cs_ref[...], preferred_element_type=F32)
                zb = jnp.dot(u_ref[bot, cols], cs_ref[...], preferred_element_type=F32)
                even, odd = zt + zb, zt - zb
                fold_ref[0, top, cols] = even[:, :gw].astype(BF16)
                fold_ref[1, top, cols] = even[:, gw:].astype(BF16)
                fold_ref[2, top, cols] = odd[:, :gw].astype(BF16)
                fold_ref[3, top, cols] = odd[:, gw:].astype(BF16)
            return carry

        if h == rc:
            fold_rows(0, 0)
        else:
            lax.fori_loop(0, h // rc, fold_rows, 0)

    ye = (jnp.dot(ce_ref[...], fold_ref[0], preferred_element_type=F32)
          + jnp.dot(se_ref[...], fold_ref[1], preferred_element_type=F32))
    yo = (jnp.dot(co_ref[...], fold_ref[2], preferred_element_type=F32)
          + jnp.dot(so_ref[...], fold_ref[3], preferred_element_type=F32))
    y = jnp.concatenate([ye, yo], axis=0).astype(BF16)
    o_ref[...] = jnp.dot(pint_ref[...], y, preferred_element_type=F32).astype(BF16)


def _fourier(u, uf_prev, *, n, nseq, row0, cs, mats, tr, tc):
    t = u.shape[0]
    h = n // 2
    assert n % tr == 0 and FOURIER_W % tc == 0 and tc % FOURIER_GROUP_W == 0 and row0 % n == 0
    rb = row0 // n
    nr = n // tr
    fb = COL_FOUR // tc
    half = pl.BlockSpec((tr // 2, h), lambda b, j, i: (i, 0))
    r = jnp.arange(tr, dtype=jnp.int32)
    pint = ((r // 2 + (r % 2) * (tr // 2))[:, None] == r[None, :]).astype(BF16)
    in_specs = [pl.BlockSpec((n, tc), lambda b, j, i: (rb + b, fb + j)),
                pl.BlockSpec(cs.shape, lambda b, j, i: (0, 0)),
                half, half, half, half,
                pl.BlockSpec((tr, tr), lambda b, j, i: (0, 0))]
    args = [u, cs, *mats, pint]
    aliases = {}
    kern = functools.partial(_fourier_kernel, n=n)
    if uf_prev is not None:
        in_specs.append(pl.BlockSpec(memory_space=pl.ANY))
        args.append(uf_prev)
        aliases = {len(args) - 1: 0}
        kern = _drop_ref(kern, len(args) - 1)
    return pl.pallas_call(
        kern,
        out_shape=jax.ShapeDtypeStruct((t, FOURIER_W), BF16),
        grid=(nseq, FOURIER_W // tc, nr),
        in_specs=in_specs,
        out_specs=pl.BlockSpec((tr, tc), lambda b, j, i: ((rb + b) * nr + i, j)),
        scratch_shapes=[pltpu.VMEM((4, h, tc), BF16)],
        input_output_aliases=aliases,
        compiler_params=_cparams(("parallel", "parallel", "arbitrary")),
        name="fourier",
    )(*args)


def _split_dft_mats(n, scale):
    h = n // 2
    assert n % 4 == 0
    jr = jnp.arange(h // 2, dtype=jnp.int32)
    m = jnp.arange(h, dtype=jnp.int32)
    sgn = jnp.where(m % 2 == 0, 1.0, -1.0).astype(F32)

    def mats(r):
        ang = ((r[:, None] * m[None, :]) % n).astype(F32) * (2.0 * jnp.pi / n)
        halves = jnp.stack([jnp.ones_like(sgn), sgn])[:, None, :]
        full = lambda top: (top[None] * halves).reshape(h, h).astype(BF16)
        return full(jnp.cos(ang) * scale), full(-jnp.sin(ang) * scale)

    ce, se = mats(2 * jr)
    co, so = mats(2 * jr + 1)
    return ce, se, co, so


def _merge_kernel(grp_ref, *refs, n_x, nct):
    del grp_ref
    x_refs = refs[:n_x]
    (og_ref, uf_ref, ga_ref, gb_ref, mod_ref, g2_ref, wro_ref, wf_ref, wo_ref, wr_ref, br_ref,
     xo_ref, h2_ref, lp_ref, tw_ref, cnt_ref) = refs[n_x:]
    tm = og_ref.shape[0]

    a = jnp.dot(og_ref[...], wro_ref[...], preferred_element_type=F32)
    b = jnp.dot(uf_ref[...], wf_ref[...], preferred_element_type=F32)
    merged = (jax.nn.sigmoid(ga_ref[...].astype(F32)) * a
              + jax.nn.sigmoid(gb_ref[...].astype(F32)) * b)
    y = jnp.dot(merged.astype(BF16), wo_ref[...], preferred_element_type=F32)
    m = mod_ref[0]
    x = _stream_tile(x_refs, pl.program_id(0), nct) + m[2:3] * y
    xo_ref[...] = x
    h2 = _rms_mod(x, g2_ref[...], m[3:4], m[4:5])
    h_hi = h2.astype(BF16)
    h2_ref[...] = h_hi
    h_lo = (h2 - h_hi.astype(F32)).astype(BF16)
    p_hi = jnp.dot(h_hi, wr_ref[...], preferred_element_type=F32)
    p_lo = jnp.dot(h_lo, wr_ref[...], preferred_element_type=F32)
    logits = p_hi + pltpu.roll(p_hi, E_PAD - N_EXPERTS, 1) + p_lo + br_ref[...]
    for s in range(tm // MOE_TILE):
        lp, tw, n = _route_tile(logits[s * MOE_TILE:(s + 1) * MOE_TILE])
        lp_ref[pl.ds(s * MOE_TILE, MOE_TILE), :] = lp
        tw_ref[pl.ds(s * MOE_TILE, MOE_TILE), :] = tw
        cnt_ref[s] = n


def _route_tile(logits):
    tt = logits.shape[0]
    lane = lax.broadcasted_iota(jnp.int32, (tt, E_PAD), 1)
    rest = logits
    vals, sels = [], []
    for _ in range(TOP_K):
        mx = jnp.max(rest, axis=-1, keepdims=True)
        idx = jnp.min(jnp.where(rest == mx, lane, E_PAD), axis=-1, keepdims=True)
        sel = lane == idx
        vals.append(mx)
        sels.append(sel)
        rest = jnp.where(sel, -jnp.inf, rest)
    exps = [jnp.exp(v - vals[0]) for v in vals]
    den = exps[0] + exps[1] + exps[2] + exps[3]

    picked = jnp.zeros((tt, E_PAD), F32)
    for sel in sels:
        picked = picked + sel.astype(F32)
    n = jnp.sum(picked, axis=0, keepdims=True).astype(jnp.int32)
    seg = ((n + (SEG_ALIGN - 1)) // SEG_ALIGN) * SEG_ALIGN
    er = lax.broadcasted_iota(jnp.int32, (E_PAD, E_PAD), 0)
    ec = lax.broadcasted_iota(jnp.int32, (E_PAD, E_PAD), 1)
    seg_rows = jnp.broadcast_to(seg.astype(F32), (SEG_ALIGN, E_PAD)).astype(BF16)
    off = jnp.dot(seg_rows, (er < ec).astype(BF16), preferred_element_type=F32)[0:1]
    ri = lax.broadcasted_iota(jnp.int32, (tt, tt), 0)
    ci = lax.broadcasted_iota(jnp.int32, (tt, tt), 1)
    before = jnp.dot((ri > ci).astype(BF16), picked.astype(BF16), preferred_element_type=F32) + off

    lp = jnp.zeros((tt, E_PAD), jnp.int32)
    tw = jnp.zeros((tt, E_PAD), F32)
    for k in range(TOP_K):
        row = jnp.sum(jnp.where(sels[k], before, 0.0), axis=-1, keepdims=True).astype(jnp.int32)
        lp = jnp.where(lane == k, row, lp)
        tw = jnp.where(lane == k, exps[k] / den, tw)
    return lp, tw, n


def _merge(xs, og, uf, u, tile_grp, mod, g2, wro, wf, wo, wr, br, tm):
    t, d = og.shape[0], wo.shape[1]
    assert tm % MOE_TILE == 0
    sub = tm // MOE_TILE
    ga_b, gb_b = COL_GA // d, COL_GA // d + 1
    full = lambda shape: pl.BlockSpec(shape, lambda i, grp: (0,) * len(shape))
    row = lambda w: pl.BlockSpec((tm, w), lambda i, grp: (i, 0))
    grid_spec = pltpu.PrefetchScalarGridSpec(
        num_scalar_prefetch=1,
        grid=(t // tm,),
        in_specs=_stream_specs(xs, tm, 1) + [
            row(RET_V_W), row(FOURIER_W),
            pl.BlockSpec((tm, d), lambda i, grp: (i, ga_b)),
            pl.BlockSpec((tm, d), lambda i, grp: (i, gb_b)),
            pl.BlockSpec((1, MOD_ROWS, d), lambda i, grp: (grp[i], 0, 0)),
            full((1, d)), full(wro.shape), full(wf.shape), full(wo.shape),
            full(wr.shape), full(br.shape)],
        out_specs=[row(d), row(d), row(E_PAD), row(E_PAD),
                   pl.BlockSpec((sub, 1, E_PAD), lambda i, grp: (i, 0, 0))],
    )
    return pl.pallas_call(
        functools.partial(_merge_kernel, n_x=len(xs), nct=xs[0].shape[0] // tm),
        out_shape=[jax.ShapeDtypeStruct((t, d), F32), jax.ShapeDtypeStruct((t, d), BF16),
                   jax.ShapeDtypeStruct((t, E_PAD), jnp.int32), jax.ShapeDtypeStruct((t, E_PAD), F32),
                   jax.ShapeDtypeStruct((t // MOE_TILE, 1, E_PAD), jnp.int32)],
        grid_spec=grid_spec,
        compiler_params=_cparams(("parallel",)),
        name="merge_route",
    )(tile_grp, *xs, og, uf, u, u, mod, g2.reshape(1, d), wro, wf, wo, wr, br)


def _router_operands(w_router, b_router):
    hi = w_router.astype(BF16)
    lo = (w_router - hi.astype(F32)).astype(BF16)
    wr = jnp.concatenate([hi, lo], axis=-1)
    wr = jnp.pad(wr, ((0, 0), (0, 0), (0, E_PAD - 2 * N_EXPERTS)))
    br = jnp.pad(b_router, ((0, 0), (0, E_PAD - N_EXPERTS)), constant_values=-1e30)
    return wr, br[:, None, :]


def _segment_copies(i, off_ref, seg_ref, gs_ref, local, remote, sem, to_remote):
    for e in range(N_EXPERTS):
        j = i * N_EXPERTS + e
        n = pl.multiple_of(seg_ref[j], SEG_ALIGN)

        @pl.when(n > 0)
        def _():
            lo = local.at[pl.ds(pl.multiple_of(off_ref[j], SEG_ALIGN), n)]
            re = remote.at[pl.ds(pl.multiple_of(gs_ref[j], SEG_ALIGN), n)]
            src, dst = (lo, re) if to_remote else (re, lo)
            pltpu.make_async_copy(src, dst, sem).start()


def _segment_wait(i, off_ref, seg_ref, local, remote, sem):
    j = i * N_EXPERTS + N_EXPERTS - 1
    tot = pl.multiple_of(off_ref[j] + seg_ref[j], SEG_ALIGN)
    pltpu.make_async_copy(local.at[pl.ds(0, tot)], remote.at[pl.ds(0, tot)], sem).wait()


def _pack_bf16_pairs(x):
    w = x.shape[1] // 2
    bits = lax.bitcast_convert_type(x, U32)
    return bits[:, :w] | (bits[:, w:] >> 16)


def _unpack_bf16_pairs(p):
    hi = lax.bitcast_convert_type(p & jnp.uint32(0xFFFF0000), F32)
    lo = lax.bitcast_convert_type(p << 16, F32)
    return jnp.concatenate([hi, lo], axis=1).astype(BF16)


def _local_row_iota(tt):
    return lax.broadcasted_iota(jnp.int32, (tt, SEG_ROWS), 1)


def _dispatch_kernel(off_ref, seg_ref, gs_ref, pend_ref, h_ref, lp_ref, xp_ref, xbuf_ref, zbuf_ref,
                     sem, zsem):
    i = pl.program_id(0)
    nt = pl.num_programs(0)
    slot = i % 2
    bm = zbuf_ref.shape[0]

    @pl.when(i == 0)
    def _():
        zbuf_ref[...] = jnp.zeros_like(zbuf_ref)

        def tail(e):
            start = jnp.where(e == 0, 0, pend_ref[jnp.maximum(e - 1, 0)])
            end = pend_ref[e]
            return end > start, xp_ref.at[pl.ds(pl.multiple_of(jnp.maximum(end - bm, 0), SEG_ALIGN), bm)]

        for e in range(N_EXPERTS):
            has, dst = tail(e)

            @pl.when(has)
            def _():
                pltpu.make_async_copy(zbuf_ref, dst, zsem).start()
        for e in range(N_EXPERTS):
            has, dst = tail(e)

            @pl.when(has)
            def _():
                pltpu.make_async_copy(zbuf_ref, dst, zsem).wait()

    lp = lp_ref[...]
    rid = _local_row_iota(lp.shape[0])
    hit = rid == lp[:, 0:1]
    for k in range(1, TOP_K):
        hit = hit | (rid == lp[:, k:k + 1])
    xbuf_ref[slot] = _pack_bf16_pairs(lax.dot_general(hit.astype(BF16), h_ref[...], (((0,), (0,)), ((), ())),
                                                      preferred_element_type=F32))
    _segment_copies(i, off_ref, seg_ref, gs_ref, xbuf_ref.at[slot], xp_ref, sem.at[slot], True)

    @pl.when(i > 0)
    def _():
        _segment_wait(i - 1, off_ref, seg_ref, xbuf_ref.at[1 - slot], xp_ref, sem.at[1 - slot])

    @pl.when(i == nt - 1)
    def _():
        _segment_wait(i, off_ref, seg_ref, xbuf_ref.at[slot], xp_ref, sem.at[slot])


def _dispatch(h2, lp, lay):
    t, d = h2.shape
    tt = MOE_TILE
    grid_spec = pltpu.PrefetchScalarGridSpec(
        num_scalar_prefetch=4,
        grid=(t // tt,),
        in_specs=[pl.BlockSpec((tt, d), lambda i, *s: (i, 0)),
                  pl.BlockSpec((tt, E_PAD), lambda i, *s: (i, 0))],
        out_specs=pl.BlockSpec(memory_space=pl.ANY),
        scratch_shapes=[pltpu.VMEM((2, SEG_ROWS, d // 2), U32), pltpu.VMEM((MOE_BLOCK, d // 2), U32),
                        pltpu.SemaphoreType.DMA((2,)), pltpu.SemaphoreType.DMA],
    )
    return pl.pallas_call(
        _dispatch_kernel,
        out_shape=jax.ShapeDtypeStruct((lay["p_total"], d // 2), U32),
        grid_spec=grid_spec,
        compiler_params=_cparams(("arbitrary",)),
        name="moe_dispatch",
    )(lay["off"], lay["seg"], lay["gseg"], lay["p_end"], h2, lp)


def _expert_kernel(be_ref, nu_ref, nxt_ref, x_ref, w1_ref, b1_ref, w2_ref, b2_ref, y_ref,
                   w1s_ref, w2s_ref, w1b_ref, w2b_ref, sem, *, layer):
    i = pl.program_id(0)
    de = w2b_ref.shape[0]
    used = i < nu_ref[0]
    e = be_ref[i]

    def fetch(ex):
        return (pltpu.make_async_copy(w1_ref.at[layer, ex], w1s_ref, sem.at[0]),
                pltpu.make_async_copy(w2_ref.at[layer, ex], w2s_ref, sem.at[1]))

    @pl.when(i == 0)
    def _():
        for cp in fetch(e):
            cp.start()

    @pl.when(used & ((i == 0) | (e != be_ref[jnp.maximum(i - 1, 0)])))
    def _():
        for cp in fetch(e):
            cp.wait()
        w1b_ref[...] = w1s_ref[...].astype(BF16)
        w2b_ref[...] = w2s_ref[...].astype(BF16)
        nxt = nxt_ref[e]

        @pl.when(nxt >= 0)
        def _():
            for cp in fetch(nxt):
                cp.start()

    @pl.when(used)
    def _():
        hh = jnp.dot(_unpack_bf16_pairs(x_ref[...]), w1b_ref[...], preferred_element_type=F32) + b1_ref[0, 0]
        gate = jnp.minimum(hh[:, :de], SWIGLU_LIMIT)
        lin = jnp.clip(hh[:, de:], -SWIGLU_LIMIT, SWIGLU_LIMIT)
        act = (lin + 1.0) * (gate * jax.nn.sigmoid(SWIGLU_ALPHA * gate))
        y = jnp.dot(act.astype(BF16), w2b_ref[...], preferred_element_type=F32) + b2_ref[0, 0]
        y_ref[...] = _pack_bf16_pairs(y.astype(BF16).astype(F32))

    @pl.when(jnp.logical_not(used))
    def _():
        y_ref[...] = jnp.zeros_like(y_ref)


def _experts(xp, lay, layer, w1, b1, w2, b2):
    p_total, dh = xp.shape
    depth, ne, d, de2 = w1.shape
    assert d == 2 * dh
    de = w2.shape[2]
    bm = MOE_BLOCK
    bmap = lambda i, be, nu, nxt: (layer, be[i], 0, 0)
    grid_spec = pltpu.PrefetchScalarGridSpec(
        num_scalar_prefetch=3,
        grid=(p_total // bm,),
        in_specs=[pl.BlockSpec((bm, dh), lambda i, be, nu, nxt: (jnp.minimum(i, nu[0] - 1), 0)),
                  pl.BlockSpec(memory_space=pl.ANY),
                  pl.BlockSpec((1, 1, 1, de2), bmap),
                  pl.BlockSpec(memory_space=pl.ANY),
                  pl.BlockSpec((1, 1, 1, d), bmap)],
        out_specs=pl.BlockSpec((bm, dh), lambda i, be, nu, nxt: (i, 0)),
        scratch_shapes=[pltpu.VMEM((d, de2), F32), pltpu.VMEM((de, d), F32),
                        pltpu.VMEM((d, de2), BF16), pltpu.VMEM((de, d), BF16),
                        pltpu.SemaphoreType.DMA((2,))],
    )
    return pl.pallas_call(
        functools.partial(_expert_kernel, layer=layer),
        out_shape=jax.ShapeDtypeStruct((p_total, dh), U32),
        grid_spec=grid_spec,
        compiler_params=_cparams(("arbitrary",)),
        name="moe_experts",
    )(lay["blk_e"], lay["n_used"], lay["nxt"], xp, w1, b1.reshape(depth, ne, 1, de2), w2,
      b2.reshape(depth, ne, 1, d))


def _combine_kernel(grp_ref, off_ref, seg_ref, gs_ref, lp_ref, x_ref, tw_ref, mod_ref, fg_ref, yb_ref,
                    *refs, split_out, nct):
    del grp_ref
    o_refs = refs[:2] if split_out else refs[:1]
    ybuf_ref, sem = refs[len(o_refs):]
    i = pl.program_id(0)
    nt = pl.num_programs(0)
    slot = i % 2

    def gather(tile, s):
        _segment_copies(tile, off_ref, seg_ref, gs_ref, ybuf_ref.at[s], yb_ref, sem.at[s], False)

    @pl.when(i == 0)
    def _():
        ybuf_ref[...] = jnp.zeros_like(ybuf_ref)
        gather(0, 0)

    @pl.when(i + 1 < nt)
    def _():
        gather(i + 1, 1 - slot)

    _segment_wait(i, off_ref, seg_ref, ybuf_ref.at[slot], yb_ref, sem.at[slot])
    lp = lp_ref[...]
    tw = tw_ref[...]
    rid = _local_row_iota(lp.shape[0])
    wsel = jnp.where(rid == lp[:, 0:1], tw[:, 0:1], 0.0)
    for k in range(1, TOP_K):
        wsel = jnp.where(rid == lp[:, k:k + 1], tw[:, k:k + 1], wsel)
    acc = jnp.dot(wsel.astype(BF16), _unpack_bf16_pairs(ybuf_ref[slot]), preferred_element_type=F32)
    x = x_ref[...] + mod_ref[0][5:6] * acc
    if not split_out:
        o_refs[0][...] = x
    else:
        y = x * lax.rsqrt(jnp.mean(x * x, axis=-1, keepdims=True) + NORM_EPS) * fg_ref[...]

        @pl.when(i < nct)
        def _():
            o_refs[0][...] = y

        @pl.when(i >= nct)
        def _():
            o_refs[1][...] = y


def _combine(x, yb, lp, tw, lay, tile_grp, mod, final_g, t_ctx, split_out):
    t, d = x.shape
    tt = MOE_TILE
    nct = t_ctx // tt
    if split_out:
        out_shape = [jax.ShapeDtypeStruct((t_ctx, d), F32), jax.ShapeDtypeStruct((t - t_ctx, d), F32)]
        out_specs = [pl.BlockSpec((tt, d), lambda i, *s: (jnp.minimum(i, nct - 1), 0)),
                     pl.BlockSpec((tt, d), lambda i, *s: (jnp.maximum(i - nct, 0), 0))]
    else:
        out_shape = [jax.ShapeDtypeStruct((t, d), F32)]
        out_specs = [pl.BlockSpec((tt, d), lambda i, *s: (i, 0))]
    grid_spec = pltpu.PrefetchScalarGridSpec(
        num_scalar_prefetch=4,
        grid=(t // tt,),
        in_specs=[pl.BlockSpec((tt, E_PAD), lambda i, *s: (i, 0)),
                  pl.BlockSpec((tt, d), lambda i, *s: (i, 0)),
                  pl.BlockSpec((tt, E_PAD), lambda i, *s: (i, 0)),
                  pl.BlockSpec((1, MOD_ROWS, d), lambda i, grp, *s: (grp[i], 0, 0)),
                  pl.BlockSpec((1, d), lambda i, *s: (0, 0)),
                  pl.BlockSpec(memory_space=pl.ANY)],
        out_specs=out_specs,
        scratch_shapes=[pltpu.VMEM((2, SEG_ROWS, d // 2), U32), pltpu.SemaphoreType.DMA((2,))],
    )
    return pl.pallas_call(
        functools.partial(_combine_kernel, split_out=split_out, nct=nct),
        out_shape=out_shape,
        grid_spec=grid_spec,
        compiler_params=_cparams(("arbitrary",)),
        name="moe_combine",
    )(tile_grp, lay["off"], lay["seg"], lay["gseg"], lp, x, tw, mod, final_g.reshape(1, d), yb)


def _segment_layout(cnt, t):
    bm = MOE_BLOCK
    nt = t // MOE_TILE
    n = cnt[:, 0, :N_EXPERTS]
    seg = ((n + (SEG_ALIGN - 1)) // SEG_ALIGN) * SEG_ALIGN
    off = jnp.cumsum(seg, axis=1) - seg
    tot = jnp.sum(seg, axis=0)
    padded = ((tot + bm - 1) // bm) * bm
    p_end = jnp.cumsum(padded)
    gseg = (p_end - padded)[None, :] + jnp.cumsum(seg, axis=0) - seg
    max_rows = t * TOP_K + nt * N_EXPERTS * (SEG_ALIGN - 1)
    n_blk = -(-max_rows // bm) + N_EXPERTS
    blk_start = jnp.arange(n_blk, dtype=jnp.int32) * bm
    blk_e = jnp.minimum(jnp.sum(p_end[None, :] <= blk_start[:, None], axis=1), N_EXPERTS - 1)
    ex = jnp.arange(N_EXPERTS, dtype=jnp.int32)
    later = (ex[None, :] > ex[:, None]) & (tot[None, :] > 0)
    nxt = jnp.where(jnp.any(later, axis=1), jnp.argmax(later, axis=1), -1)
    i32 = lambda a: a.reshape(-1).astype(jnp.int32)
    return dict(seg=i32(seg), off=i32(off), gseg=i32(gseg), p_end=i32(p_end), blk_e=i32(blk_e),
                n_used=i32(p_end[-1:] // bm), nxt=i32(nxt), p_total=n_blk * bm)


def _tile_groups(t, t_ctx, dec_seq, tm):
    starts = jnp.arange(t // tm, dtype=jnp.int32) * tm
    return jnp.where(starts < t_ctx, 0, 1 + (starts - t_ctx) // dec_seq).astype(jnp.int32)


def kernel(x_prompt, x_sample, state_ret, c, c_ctx, w_ada, b_ada, norm1_g, norm2_g, w_in, ret_decay_logit, w_ret_o, w_four, w_out, w_router, b_router, w1, b1, w2, b2, final_g):
    batch, seq, d = x_prompt.shape
    dec_batch, dec_seq, _ = x_sample.shape
    depth = w_in.shape[0]
    t_ctx, t_lat = batch * seq, dec_batch * dec_seq
    t = t_ctx + t_lat
    tm_in = min(1024, t_ctx, dec_seq)
    tm = min(512, t_ctx, dec_seq)
    assert t_ctx % tm_in == 0 and dec_seq % tm_in == 0 and t_ctx % dec_seq == 0

    n_cond = 16
    cond = jnp.zeros((n_cond, d), F32).at[0].set(c_ctx).at[1:1 + dec_batch].set(c)
    mod = _adaln_mod(cond, w_ada, b_ada).reshape(depth, n_cond, 6, d)
    mod = jnp.pad(mod, ((0, 0), (0, 0), (0, MOD_ROWS - 6), (0, 0)))
    grp_in = _tile_groups(t, t_ctx, dec_seq, tm_in)
    grp = _tile_groups(t, t_ctx, dec_seq, tm)
    grp_moe = _tile_groups(t, t_ctx, dec_seq, MOE_TILE)

    rot = _rotary_tables(dec_seq)
    gw = FOURIER_GROUP_W
    jc = jnp.arange(gw, dtype=jnp.int32)
    ang_c = ((jc[:, None] * jc[None, :]) % gw).astype(F32) * (2.0 * jnp.pi / gw)
    cs = jnp.concatenate([jnp.cos(ang_c), jnp.sin(ang_c)], axis=1).astype(BF16)
    dm_ctx = _split_dft_mats(seq, (seq * gw) ** -0.5)
    dm_lat = _split_dft_mats(dec_seq, (dec_seq * gw) ** -0.5)
    log_g = jax.nn.log_sigmoid(ret_decay_logit.astype(F32))
    wr, br = _router_operands(w_router, b_router)

    xs = [x_prompt.reshape(t_ctx, d), x_sample.reshape(t_lat, d)]
    new_state = None
    for l in range(depth):
        last = l == depth - 1
        tall = len(xs) == 1 and t_ctx % (2 * tm_in) == 0 and dec_seq % (2 * tm_in) == 0
        if tall:
            u = _in_proj(xs, _tile_groups(t, t_ctx, dec_seq, 2 * tm_in), mod[l], norm1_g[l],
                         w_in[l].astype(BF16), 2 * tm_in, 1536)
        else:
            u = _in_proj(xs, grp_in, mod[l], norm1_g[l], w_in[l].astype(BF16), tm_in, 3072)
        tables = _decay_tables(log_g[l], min(RET_CHUNK, seq, dec_seq))
        og, new_state = _retention(u, None, n=seq, nseq=batch, row0=0, tables=tables, rot=None, s0=None,
                                   layer=l, emit_state=True, heads=N_RET_HEADS, depth=depth,
                                   st_prev=new_state)
        og = _retention(u, og, n=dec_seq, nseq=dec_batch, row0=t_ctx, tables=tables, rot=rot,
                        s0=state_ret, layer=l, emit_state=False, heads=1)
        uf = _fourier(u, None, n=seq, nseq=batch, row0=0, cs=cs, mats=dm_ctx, tr=seq, tc=FOURIER_W)
        uf = _fourier(u, uf, n=dec_seq, nseq=dec_batch, row0=t_ctx, cs=cs, mats=dm_lat,
                      tr=min(512, dec_seq), tc=FOURIER_W)
        x, h2, lp, tw, cnt = _merge(
            xs, og, uf, u, grp, mod[l], norm2_g[l], w_ret_o[l].astype(BF16), w_four[l].astype(BF16),
            w_out[l].astype(BF16), wr[l], br[l], tm)
        lay = _segment_layout(cnt, t)
        xp = _dispatch(h2, lp, lay)
        yb = _experts(xp, lay, l, w1, b1, w2, b2)
        xs = _combine(x, yb, lp, tw, lay, grp_moe, mod[l], final_g, t_ctx, split_out=last)

    return (xs[0].reshape(batch, seq, d), xs[1].reshape(dec_batch, dec_seq, d), new_state)
```

```python
import functools

import jax
import jax.numpy as jnp
from jax import lax
from jax.experimental import pallas as pl
from jax.experimental.pallas import tpu as pltpu

F32 = jnp.float32
BF16 = jnp.bfloat16
U32 = jnp.uint32

N_RET_HEADS = 8
RET_DK = 128
RET_DV = 256
RET_QK_W = N_RET_HEADS * RET_DK
RET_V_W = N_RET_HEADS * RET_DV
N_FOURIER_GROUPS = 4
FOURIER_GROUP_W = 256
FOURIER_W = N_FOURIER_GROUPS * FOURIER_GROUP_W
GRID_W = 64
N_EXPERTS = 32
TOP_K = 4
SWIGLU_LIMIT = 7.0
SWIGLU_ALPHA = 1.702
ROPE_BASE = 10000.0
NORM_EPS = 1e-6
GN_EPS = 1e-5

COL_Q = 0
COL_K = COL_Q + RET_QK_W
COL_V = COL_K + RET_QK_W
COL_GRET = COL_V + RET_V_W
COL_FOUR = COL_GRET + RET_V_W
COL_GA = COL_FOUR + FOURIER_W

LANES = 128
VMEM_LIMIT_BYTES = 56 * 1024 * 1024

MOD_ROWS = 8
RET_CHUNK = 256
MOE_BLOCK = 512
MOE_TILE = 256
SEG_ALIGN = 8
SEG_ROWS = MOE_TILE * TOP_K + N_EXPERTS * SEG_ALIGN
E_PAD = LANES


def _cparams(sem, vmem=VMEM_LIMIT_BYTES):
    return pltpu.CompilerParams(dimension_semantics=sem, vmem_limit_bytes=vmem)


def _drop_ref(kern, idx):
    def wrapped(*refs):
        return kern(*refs[:idx], *refs[idx + 1:])
    return wrapped


def _mod_kernel(c_ref, w_ref, b_ref, o_ref):
    c = c_ref[...]
    s = c * jax.nn.sigmoid(c)
    o_ref[0] = jnp.dot(s, w_ref[0], precision=lax.Precision.HIGHEST,
                       preferred_element_type=F32) + b_ref[0]


def _adaln_mod(cond, w_ada, b_ada):
    depth, d, w6 = w_ada.shape
    r = cond.shape[0]
    tn = 1536
    assert w6 % tn == 0
    return pl.pallas_call(
        _mod_kernel,
        out_shape=jax.ShapeDtypeStruct((depth, r, w6), F32),
        grid=(depth, w6 // tn),
        in_specs=[pl.BlockSpec((r, d), lambda l, j: (0, 0)),
                  pl.BlockSpec((1, d, tn), lambda l, j: (l, 0, j)),
                  pl.BlockSpec((1, 1, tn), lambda l, j: (l, 0, j))],
        out_specs=pl.BlockSpec((1, r, tn), lambda l, j: (l, 0, j)),
        compiler_params=_cparams(("parallel", "parallel")),
        name="adaln_mod",
    )(cond, w_ada, b_ada.reshape(depth, 1, w6))


def _stream_specs(xs, tm, imap_args):
    d = xs[0].shape[1]
    if len(xs) == 1:
        return [pl.BlockSpec((tm, d), lambda i, *a: (i, 0))]
    nct = xs[0].shape[0] // tm
    return [pl.BlockSpec((tm, d), lambda i, *a: (jnp.minimum(i, nct - 1), 0)),
            pl.BlockSpec((tm, d), lambda i, *a: (jnp.maximum(i - nct, 0), 0))]


def _stream_tile(x_refs, i, nct):
    if len(x_refs) == 1:
        return x_refs[0][...]
    return jnp.where(i < nct, x_refs[0][...], x_refs[1][...])


def _rms_mod(x, g, shift, scale):
    y = x * lax.rsqrt(jnp.mean(x * x, axis=-1, keepdims=True) + NORM_EPS) * g
    return y * (1.0 + scale) + shift


def _in_proj_kernel(grp_ref, *refs, n_x, nct):
    del grp_ref
    x_refs = refs[:n_x]
    mod_ref, g_ref, w_ref, o_ref, h_ref = refs[n_x:]

    @pl.when(pl.program_id(1) == 0)
    def _():
        m = mod_ref[0]
        x = _stream_tile(x_refs, pl.program_id(0), nct)
        h_ref[...] = _rms_mod(x, g_ref[...], m[0:1], m[1:2]).astype(BF16)

    o_ref[...] = jnp.dot(h_ref[...], w_ref[...], preferred_element_type=F32).astype(BF16)


def _in_proj(xs, tile_grp, mod, g, w, tm, tn):
    t = sum(x.shape[0] for x in xs)
    d, n = w.shape
    assert t % tm == 0 and n % tn == 0 and all(x.shape[0] % tm == 0 for x in xs)
    x_specs = [pl.BlockSpec(s.block_shape, (lambda f: lambda i, j, grp: f(i))(s.index_map))
               for s in _stream_specs(xs, tm, 1)]
    grid_spec = pltpu.PrefetchScalarGridSpec(
        num_scalar_prefetch=1,
        grid=(t // tm, n // tn),
        in_specs=x_specs + [pl.BlockSpec((1, MOD_ROWS, d), lambda i, j, grp: (grp[i], 0, 0)),
                            pl.BlockSpec((1, d), lambda i, j, grp: (0, 0)),
                            pl.BlockSpec((d, tn), lambda i, j, grp: (0, j))],
        out_specs=pl.BlockSpec((tm, tn), lambda i, j, grp: (i, j)),
        scratch_shapes=[pltpu.VMEM((tm, d), BF16)],
    )
    return pl.pallas_call(
        functools.partial(_in_proj_kernel, n_x=len(xs), nct=xs[0].shape[0] // tm),
        out_shape=jax.ShapeDtypeStruct((t, n), BF16),
        grid_spec=grid_spec,
        compiler_params=_cparams(("parallel", "arbitrary")),
        name="in_proj",
    )(tile_grp, *xs, mod, g.reshape(1, d), w)


def _retention_kernel(cdec_ref, *refs, n, chunk, rotary, has_s0, emit_state, heads):
    for hh in range(heads):
        _retention_head(cdec_ref, refs, hh, n=n, chunk=chunk, rotary=rotary, has_s0=has_s0,
                        emit_state=emit_state, heads=heads)


def _retention_head(cdec_ref, refs, hh, *, n, chunk, rotary, has_s0, emit_state, heads):
    refs = list(refs)
    q_ref, k_ref, v_ref, g_ref, dmask_ref, dvec_ref, kdec_ref = refs[:7]
    pos = 7
    if rotary:
        cos_ref, sin_ref = refs[pos:pos + 2]
        pos += 2
    if has_s0:
        s0_ref = refs[pos]
        pos += 1
    o_ref = refs[pos]
    pos += 1
    if emit_state:
        sn_ref = refs[pos]
        pos += 1
    qs_ref, ks_ref, kv_ref, st_ref, s_ref = [ref.at[hh] for ref in refs[pos:pos + 5]]
    qcols = pl.ds(hh * RET_DK, RET_DK)
    vcols = pl.ds(hh * RET_DV, RET_DV)

    nc = n // chunk
    cross = has_s0 or nc > 1
    unroll = 16 if nc % 16 == 0 else (2 if nc % 2 == 0 else 1)
    h = pl.program_id(0) * heads + hh
    cdec_f = cdec_ref[2 * h]
    cdec_b = cdec_ref[2 * h + 1]
    fwd = pl.ds(0, RET_DK)
    bwd = pl.ds(RET_DK, RET_DK)

    def rows(c):
        return pl.ds(pl.multiple_of(c * chunk, chunk), chunk)

    def prep(c, carry):
        r = rows(c)
        q = q_ref[r, qcols].astype(F32) * (RET_DK ** -0.5)
        k = k_ref[r, qcols].astype(F32)
        if rotary:
            cs, sn = cos_ref[r, :], sin_ref[r, :]
            q = q * cs + pltpu.roll(q, RET_DK // 2, 1) * sn
            k = k * cs + pltpu.roll(k, RET_DK // 2, 1) * sn
        qs_ref[r, :] = q.astype(BF16)
        kt = k.T.astype(BF16)
        ks_ref[c] = kt
        ktf = kt.astype(F32)
        kdec = kdec_ref[hh]
        kd = jnp.concatenate([(ktf * kdec[0:1]).astype(BF16), (ktf * kdec[1:2]).astype(BF16)], axis=0)
        kv_ref[c] = jnp.dot(kd, v_ref[r, vcols], preferred_element_type=F32)
        return carry

    def scan(i, carry):
        cb = nc - 1 - i
        st_ref[i, fwd, :] = s_ref[fwd, :].astype(BF16)
        s_ref[fwd, :] = s_ref[fwd, :] * cdec_f + kv_ref[i, fwd, :]
        st_ref[cb, bwd, :] = s_ref[bwd, :].astype(BF16)
        s_ref[bwd, :] = s_ref[bwd, :] * cdec_b + kv_ref[cb, bwd, :]
        return carry

    def out_step(c, carry):
        r = rows(c)
        qc = qs_ref[r, :]
        sc = jnp.dot(qc, ks_ref[c], preferred_element_type=F32)
        o = jnp.dot((sc * dmask_ref[hh]).astype(BF16), v_ref[r, vcols], preferred_element_type=F32)
        if cross:
            qf = qc.astype(F32)
            ql = jnp.concatenate([(qf * dvec_ref[hh, 0]).astype(BF16),
                                  (qf * dvec_ref[hh, 2]).astype(BF16)], axis=1)
            o = o + jnp.dot(ql, st_ref[c], preferred_element_type=F32)
        mu = jnp.mean(o, axis=-1, keepdims=True)
        oc = o - mu
        var = jnp.mean(oc * oc, axis=-1, keepdims=True)
        gt = g_ref[r, vcols]
        o_ref[r, vcols] = (gt * jax.nn.sigmoid(gt)) * (oc * lax.rsqrt(var + GN_EPS)).astype(BF16)
        return carry

    def loop(body):
        if nc == 1:
            body(0, 0)
        else:
            lax.fori_loop(0, nc, body, 0, unroll=unroll)

    if has_s0:
        s_ref[fwd, :] = s0_ref[0, 0, 0, hh]
        s_ref[bwd, :] = s0_ref[0, 0, 1, hh]
    else:
        s_ref[...] = jnp.zeros_like(s_ref)
    loop(prep)
    if nc == 1:
        scan(0, 0)
    else:
        lax.fori_loop(0, nc, scan, 0, unroll=unroll)
    if emit_state:
        sn_ref[0, 0, 0, hh] = s_ref[fwd, :]
        sn_ref[0, 0, 1, hh] = s_ref[bwd, :]
    loop(out_step)


def _decay_tables(log_g, chunk):
    i = jnp.arange(chunk, dtype=F32)
    diff = i[:, None] - i[None, :]
    lf = log_g[0][:, None, None]
    lb = log_g[1][:, None, None]
    dmask = (jnp.where(diff >= 0, jnp.exp(jnp.maximum(diff, 0.0) * lf), 0.0)
             + jnp.where(diff <= 0, jnp.exp(jnp.maximum(-diff, 0.0) * lb), 0.0))
    lf2, lb2 = log_g[0][:, None], log_g[1][:, None]
    vecs = jnp.stack([jnp.exp((i + 1.0)[None, :] * lf2),
                      jnp.exp((chunk - 1.0 - i)[None, :] * lf2),
                      jnp.exp((chunk - i)[None, :] * lb2),
                      jnp.exp(i[None, :] * lb2)], axis=1)
    dvec = jnp.broadcast_to(vecs[..., None], vecs.shape + (LANES,))
    kdec = jnp.pad(vecs[:, 1::2], ((0, 0), (0, MOD_ROWS - 2), (0, 0)))
    cdec = jnp.exp(chunk * log_g).T.reshape(-1)
    return dmask, dvec, kdec, cdec


def _retention(u, og_prev, *, n, nseq, row0, tables, rot, s0, layer, emit_state, heads, depth=1, st_prev=None):
    t = u.shape[0]
    chunk = min(RET_CHUNK, n)
    assert n % chunk == 0 and row0 % n == 0 and N_RET_HEADS % heads == 0
    rb = row0 // n
    dmask, dvec, kdec, cdec = tables
    nc = n // chunk
    kw, vw = heads * RET_DK, heads * RET_DV
    qb, vb, gb = COL_K // kw, COL_V // vw, COL_GRET // vw
    in_specs = [pl.BlockSpec((n, kw), lambda h, b, cd: (rb + b, h)),
                pl.BlockSpec((n, kw), lambda h, b, cd: (rb + b, qb + h)),
                pl.BlockSpec((n, vw), lambda h, b, cd: (rb + b, vb + h)),
                pl.BlockSpec((n, vw), lambda h, b, cd: (rb + b, gb + h)),
                pl.BlockSpec((heads, chunk, chunk), lambda h, b, cd: (h, 0, 0)),
                pl.BlockSpec((heads, 4, chunk, LANES), lambda h, b, cd: (h, 0, 0, 0)),
                pl.BlockSpec((heads, MOD_ROWS, chunk), lambda h, b, cd: (h, 0, 0))]
    args = [u, u, u, u, dmask, dvec, kdec]
    if rot is not None:
        in_specs += [pl.BlockSpec((n, RET_DK), lambda h, b, cd: (0, 0))] * 2
        args += list(rot)
    if s0 is not None:
        in_specs.append(pl.BlockSpec((1, 1, 2, heads, RET_DK, RET_DV),
                                     lambda h, b, cd: (b, layer, 0, h, 0, 0)))
        args.append(s0)
    out_shape = [jax.ShapeDtypeStruct((t, RET_V_W), BF16)]
    out_specs = [pl.BlockSpec((n, vw), lambda h, b, cd: (rb + b, h))]
    if emit_state:
        out_shape.append(jax.ShapeDtypeStruct((nseq, depth, 2, N_RET_HEADS, RET_DK, RET_DV), F32))
        out_specs.append(pl.BlockSpec((1, 1, 2, heads, RET_DK, RET_DV),
                                      lambda h, b, cd: (b, layer, 0, h, 0, 0)))
    kern = functools.partial(_retention_kernel, n=n, chunk=chunk, rotary=rot is not None,
                             has_s0=s0 is not None, emit_state=emit_state, heads=heads)
    aliases = {}
    for prev, out_idx in ((og_prev, 0), (st_prev, 1)):
        if prev is not None:
            in_specs.append(pl.BlockSpec(memory_space=pl.ANY))
            args.append(prev)
            aliases[len(args)] = out_idx
            kern = _drop_ref(kern, len(args))
    grid_spec = pltpu.PrefetchScalarGridSpec(
        num_scalar_prefetch=1, grid=(N_RET_HEADS // heads, nseq), in_specs=in_specs, out_specs=out_specs,
        scratch_shapes=[pltpu.VMEM((heads, n, RET_DK), BF16), pltpu.VMEM((heads, nc, RET_DK, chunk), BF16),
                        pltpu.VMEM((heads, nc, 2 * RET_DK, RET_DV), F32),
                        pltpu.VMEM((heads, nc, 2 * RET_DK, RET_DV), BF16),
                        pltpu.VMEM((heads, 2 * RET_DK, RET_DV), F32)])
    outs = pl.pallas_call(
        kern, out_shape=out_shape, grid_spec=grid_spec, input_output_aliases=aliases,
        compiler_params=_cparams(("parallel", "parallel")),
        name="retention_lat" if rot is not None else "retention_ctx",
    )(cdec, *args)
    return outs if emit_state else outs[0]


def _rotary_tables(n):
    rows = n // GRID_W
    row = jnp.repeat(jnp.arange(rows), GRID_W).astype(F32)
    col = jnp.tile(jnp.arange(GRID_W), rows).astype(F32)
    n_freq = RET_DK // 4
    inv_freq = ROPE_BASE ** (-jnp.arange(n_freq, dtype=F32) / n_freq)
    ang = jnp.concatenate([row[:, None] * inv_freq, col[:, None] * inv_freq], axis=-1)
    cos, sin = jnp.cos(ang), jnp.sin(ang)
    return jnp.concatenate([cos, cos], axis=-1), jnp.concatenate([-sin, sin], axis=-1)


def _fourier_kernel(u_ref, cs_ref, ce_ref, se_ref, co_ref, so_ref, pint_ref, o_ref, fold_ref, *, n):
    h = n // 2
    gw = FOURIER_GROUP_W
    tc = u_ref.shape[1]
    rc = min(512, h)

    @pl.when(pl.program_id(2) == 0)
    def _():
        def fold_rows(r, carry):
            top = pl.ds(pl.multiple_of(r * rc, rc), rc)
            bot = pl.ds(pl.multiple_of(h + r * rc, rc), rc)
            for g in range(tc // gw):
                cols = pl.ds(g * gw, gw)
                zt = jnp.dot(u_ref[top, cols], cs_ref[...], preferred_element_type=F32)
                zb = jnp.dot(u_ref[bot, cols], cs_ref[...], preferred_element_type=F32)
                even, odd = zt + zb, zt - zb
                fold_ref[0, top, cols] = even[:, :gw].astype(BF16)
                fold_ref[1, top, cols] = even[:, gw:].astype(BF16)
                fold_ref[2, top, cols] = odd[:, :gw].astype(BF16)
                fold_ref[3, top, cols] = odd[:, gw:].astype(BF16)
            return carry

        if h == rc:
            fold_rows(0, 0)
        else:
            lax.fori_loop(0, h // rc, fold_rows, 0)

    ye = (jnp.dot(ce_ref[...], fold_ref[0], preferred_element_type=F32)
          + jnp.dot(se_ref[...], fold_ref[1], preferred_element_type=F32))
    yo = (jnp.dot(co_ref[...], fold_ref[2], preferred_element_type=F32)
          + jnp.dot(so_ref[...], fold_ref[3], preferred_element_type=F32))
    y = jnp.concatenate([ye, yo], axis=0).astype(BF16)
    o_ref[...] = jnp.dot(pint_ref[...], y, preferred_element_type=F32).astype(BF16)


def _fourier(u, uf_prev, *, n, nseq, row0, cs, mats, tr, tc):
    t = u.shape[0]
    h = n // 2
    assert n % tr == 0 and FOURIER_W % tc == 0 and tc % FOURIER_GROUP_W == 0 and row0 % n == 0
    rb = row0 // n
    nr = n // tr
    fb = COL_FOUR // tc
    half = pl.BlockSpec((tr // 2, h), lambda b, j, i: (i, 0))
    r = jnp.arange(tr, dtype=jnp.int32)
    pint = ((r // 2 + (r % 2) * (tr // 2))[:, None] == r[None, :]).astype(BF16)
    in_specs = [pl.BlockSpec((n, tc), lambda b, j, i: (rb + b, fb + j)),
                pl.BlockSpec(cs.shape, lambda b, j, i: (0, 0)),
                half, half, half, half,
                pl.BlockSpec((tr, tr), lambda b, j, i: (0, 0))]
    args = [u, cs, *mats, pint]
    aliases = {}
    kern = functools.partial(_fourier_kernel, n=n)
    if uf_prev is not None:
        in_specs.append(pl.BlockSpec(memory_space=pl.ANY))
        args.append(uf_prev)
        aliases = {len(args) - 1: 0}
        kern = _drop_ref(kern, len(args) - 1)
    return pl.pallas_call(
        kern,
        out_shape=jax.ShapeDtypeStruct((t, FOURIER_W), BF16),
        grid=(nseq, FOURIER_W // tc, nr),
        in_specs=in_specs,
        out_specs=pl.BlockSpec((tr, tc), lambda b, j, i: ((rb + b) * nr + i, j)),
        scratch_shapes=[pltpu.VMEM((4, h, tc), BF16)],
        input_output_aliases=aliases,
        compiler_params=_cparams(("parallel", "parallel", "arbitrary")),
        name="fourier",
    )(*args)


def _split_dft_mats(n, scale):
    h = n // 2
    assert n % 4 == 0
    jr = jnp.arange(h // 2, dtype=jnp.int32)
    m = jnp.arange(h, dtype=jnp.int32)
    sgn = jnp.where(m % 2 == 0, 1.0, -1.0).astype(F32)

    def mats(r):
        ang = ((r[:, None] * m[None, :]) % n).astype(F32) * (2.0 * jnp.pi / n)
        halves = jnp.stack([jnp.ones_like(sgn), sgn])[:, None, :]
        full = lambda top: (top[None] * halves).reshape(h, h).astype(BF16)
        return full(jnp.cos(ang) * scale), full(-jnp.sin(ang) * scale)

    ce, se = mats(2 * jr)
    co, so = mats(2 * jr + 1)
    return ce, se, co, so


def _merge_kernel(grp_ref, *refs, n_x, nct):
    del grp_ref
    x_refs = refs[:n_x]
    (og_ref, uf_ref, ga_ref, gb_ref, mod_ref, g2_ref, wro_ref, wf_ref, wo_ref, wr_ref, br_ref,
     xo_ref, h2_ref, lp_ref, tw_ref, cnt_ref) = refs[n_x:]
    tm = og_ref.shape[0]

    a = jnp.dot(og_ref[...], wro_ref[...], preferred_element_type=F32)
    b = jnp.dot(uf_ref[...], wf_ref[...], preferred_element_type=F32)
    merged = (jax.nn.sigmoid(ga_ref[...].astype(F32)) * a
              + jax.nn.sigmoid(gb_ref[...].astype(F32)) * b)
    y = jnp.dot(merged.astype(BF16), wo_ref[...], preferred_element_type=F32)
    m = mod_ref[0]
    x = _stream_tile(x_refs, pl.program_id(0), nct) + m[2:3] * y
    xo_ref[...] = x
    h2 = _rms_mod(x, g2_ref[...], m[3:4], m[4:5])
    h_hi = h2.astype(BF16)
    h2_ref[...] = h_hi
    h_lo = (h2 - h_hi.astype(F32)).astype(BF16)
    p_hi = jnp.dot(h_hi, wr_ref[...], preferred_element_type=F32)
    p_lo = jnp.dot(h_lo, wr_ref[...], preferred_element_type=F32)
    logits = p_hi + pltpu.roll(p_hi, E_PAD - N_EXPERTS, 1) + p_lo + br_ref[...]
    for s in range(tm // MOE_TILE):
        lp, tw, n = _route_tile(logits[s * MOE_TILE:(s + 1) * MOE_TILE])
        lp_ref[pl.ds(s * MOE_TILE, MOE_TILE), :] = lp
        tw_ref[pl.ds(s * MOE_TILE, MOE_TILE), :] = tw
        cnt_ref[s] = n


def _route_tile(logits):
    tt = logits.shape[0]
    lane = lax.broadcasted_iota(jnp.int32, (tt, E_PAD), 1)
    rest = logits
    vals, sels = [], []
    for _ in range(TOP_K):
        mx = jnp.max(rest, axis=-1, keepdims=True)
        idx = jnp.min(jnp.where(rest == mx, lane, E_PAD), axis=-1, keepdims=True)
        sel = lane == idx
        vals.append(mx)
        sels.append(sel)
        rest = jnp.where(sel, -jnp.inf, rest)
    exps = [jnp.exp(v - vals[0]) for v in vals]
    den = exps[0] + exps[1] + exps[2] + exps[3]

    picked = jnp.zeros((tt, E_PAD), F32)
    for sel in sels:
        picked = picked + sel.astype(F32)
    n = jnp.sum(picked, axis=0, keepdims=True).astype(jnp.int32)
    seg = ((n + (SEG_ALIGN - 1)) // SEG_ALIGN) * SEG_ALIGN
    er = lax.broadcasted_iota(jnp.int32, (E_PAD, E_PAD), 0)
    ec = lax.broadcasted_iota(jnp.int32, (E_PAD, E_PAD), 1)
    seg_rows = jnp.broadcast_to(seg.astype(F32), (SEG_ALIGN, E_PAD)).astype(BF16)
    off = jnp.dot(seg_rows, (er < ec).astype(BF16), preferred_element_type=F32)[0:1]
    ri = lax.broadcasted_iota(jnp.int32, (tt, tt), 0)
    ci = lax.broadcasted_iota(jnp.int32, (tt, tt), 1)
    before = jnp.dot((ri > ci).astype(BF16), picked.astype(BF16), preferred_element_type=F32) + off

    lp = jnp.zeros((tt, E_PAD), jnp.int32)
    tw = jnp.zeros((tt, E_PAD), F32)
    for k in range(TOP_K):
        row = jnp.sum(jnp.where(sels[k], before, 0.0), axis=-1, keepdims=True).astype(jnp.int32)
        lp = jnp.where(lane == k, row, lp)
        tw = jnp.where(lane == k, exps[k] / den, tw)
    return lp, tw, n


def _merge(xs, og, uf, u, tile_grp, mod, g2, wro, wf, wo, wr, br, tm):
    t, d = og.shape[0], wo.shape[1]
    assert tm % MOE_TILE == 0
    sub = tm // MOE_TILE
    ga_b, gb_b = COL_GA // d, COL_GA // d + 1
    full = lambda shape: pl.BlockSpec(shape, lambda i, grp: (0,) * len(shape))
    row = lambda w: pl.BlockSpec((tm, w), lambda i, grp: (i, 0))
    grid_spec = pltpu.PrefetchScalarGridSpec(
        num_scalar_prefetch=1,
        grid=(t // tm,),
        in_specs=_stream_specs(xs, tm, 1) + [
            row(RET_V_W), row(FOURIER_W),
            pl.BlockSpec((tm, d), lambda i, grp: (i, ga_b)),
            pl.BlockSpec((tm, d), lambda i, grp: (i, gb_b)),
            pl.BlockSpec((1, MOD_ROWS, d), lambda i, grp: (grp[i], 0, 0)),
            full((1, d)), full(wro.shape), full(wf.shape), full(wo.shape),
            full(wr.shape), full(br.shape)],
        out_specs=[row(d), row(d), row(E_PAD), row(E_PAD),
                   pl.BlockSpec((sub, 1, E_PAD), lambda i, grp: (i, 0, 0))],
    )
    return pl.pallas_call(
        functools.partial(_merge_kernel, n_x=len(xs), nct=xs[0].shape[0] // tm),
        out_shape=[jax.ShapeDtypeStruct((t, d), F32), jax.ShapeDtypeStruct((t, d), BF16),
                   jax.ShapeDtypeStruct((t, E_PAD), jnp.int32), jax.ShapeDtypeStruct((t, E_PAD), F32),
                   jax.ShapeDtypeStruct((t // MOE_TILE, 1, E_PAD), jnp.int32)],
        grid_spec=grid_spec,
        compiler_params=_cparams(("parallel",)),
        name="merge_route",
    )(tile_grp, *xs, og, uf, u, u, mod, g2.reshape(1, d), wro, wf, wo, wr, br)


def _router_operands(w_router, b_router):
    hi = w_router.astype(BF16)
    lo = (w_router - hi.astype(F32)).astype(BF16)
    wr = jnp.concatenate([hi, lo], axis=-1)
    wr = jnp.pad(wr, ((0, 0), (0, 0), (0, E_PAD - 2 * N_EXPERTS)))
    br = jnp.pad(b_router, ((0, 0), (0, E_PAD - N_EXPERTS)), constant_values=-1e30)
    return wr, br[:, None, :]


def _segment_copies(i, off_ref, seg_ref, gs_ref, local, remote, sem, to_remote):
    for e in range(N_EXPERTS):
        j = i * N_EXPERTS + e
        n = pl.multiple_of(seg_ref[j], SEG_ALIGN)

        @pl.when(n > 0)
        def _():
            lo = local.at[pl.ds(pl.multiple_of(off_ref[j], SEG_ALIGN), n)]
            re = remote.at[pl.ds(pl.multiple_of(gs_ref[j], SEG_ALIGN), n)]
            src, dst = (lo, re) if to_remote else (re, lo)
            pltpu.make_async_copy(src, dst, sem).start(priority=e % 2)


def _segment_wait(i, off_ref, seg_ref, local, remote, sem):
    j = i * N_EXPERTS + N_EXPERTS - 1
    tot = pl.multiple_of(off_ref[j] + seg_ref[j], SEG_ALIGN)
    pltpu.make_async_copy(local.at[pl.ds(0, tot)], remote.at[pl.ds(0, tot)], sem).wait()


def _pack_bf16_pairs(x):
    w = x.shape[1] // 2
    bits = lax.bitcast_convert_type(x, U32)
    return bits[:, :w] | (bits[:, w:] >> 16)


def _unpack_bf16_pairs(p):
    hi = lax.bitcast_convert_type(p & jnp.uint32(0xFFFF0000), F32)
    lo = lax.bitcast_convert_type(p << 16, F32)
    return jnp.concatenate([hi, lo], axis=1).astype(BF16)


def _local_row_iota(tt):
    return lax.broadcasted_iota(jnp.int32, (tt, SEG_ROWS), 1)


def _dispatch_kernel(off_ref, seg_ref, gs_ref, pend_ref, h_ref, lp_ref, xp_ref, xbuf_ref, zbuf_ref,
                     sem, zsem):
    i = pl.program_id(0)
    nt = pl.num_programs(0)
    slot = i % 2
    bm = zbuf_ref.shape[0]

    @pl.when(i == 0)
    def _():
        zbuf_ref[...] = jnp.zeros_like(zbuf_ref)

        def tail(e):
            start = jnp.where(e == 0, 0, pend_ref[jnp.maximum(e - 1, 0)])
            end = pend_ref[e]
            return end > start, xp_ref.at[pl.ds(pl.multiple_of(jnp.maximum(end - bm, 0), SEG_ALIGN), bm)]

        for e in range(N_EXPERTS):
            has, dst = tail(e)

            @pl.when(has)
            def _():
                pltpu.make_async_copy(zbuf_ref, dst, zsem).start()
        for e in range(N_EXPERTS):
            has, dst = tail(e)

            @pl.when(has)
            def _():
                pltpu.make_async_copy(zbuf_ref, dst, zsem).wait()

    lp = lp_ref[...]
    rid = _local_row_iota(lp.shape[0])
    hit = rid == lp[:, 0:1]
    for k in range(1, TOP_K):
        hit = hit | (rid == lp[:, k:k + 1])
    xbuf_ref[slot] = _pack_bf16_pairs(lax.dot_general(hit.astype(BF16), h_ref[...], (((0,), (0,)), ((), ())),
                                                      preferred_element_type=F32))
    _segment_copies(i, off_ref, seg_ref, gs_ref, xbuf_ref.at[slot], xp_ref, sem.at[slot], True)

    @pl.when(i > 0)
    def _():
        _segment_wait(i - 1, off_ref, seg_ref, xbuf_ref.at[1 - slot], xp_ref, sem.at[1 - slot])

    @pl.when(i == nt - 1)
    def _():
        _segment_wait(i, off_ref, seg_ref, xbuf_ref.at[slot], xp_ref, sem.at[slot])


def _dispatch(h2, lp, lay):
    t, d = h2.shape
    tt = MOE_TILE
    grid_spec = pltpu.PrefetchScalarGridSpec(
        num_scalar_prefetch=4,
        grid=(t // tt,),
        in_specs=[pl.BlockSpec((tt, d), lambda i, *s: (i, 0)),
                  pl.BlockSpec((tt, E_PAD), lambda i, *s: (i, 0))],
        out_specs=pl.BlockSpec(memory_space=pl.ANY),
        scratch_shapes=[pltpu.VMEM((2, SEG_ROWS, d // 2), U32), pltpu.VMEM((MOE_BLOCK, d // 2), U32),
                        pltpu.SemaphoreType.DMA((2,)), pltpu.SemaphoreType.DMA],
    )
    return pl.pallas_call(
        _dispatch_kernel,
        out_shape=jax.ShapeDtypeStruct((lay["p_total"], d // 2), U32),
        grid_spec=grid_spec,
        compiler_params=_cparams(("arbitrary",)),
        name="moe_dispatch",
    )(lay["off"], lay["seg"], lay["gseg"], lay["p_end"], h2, lp)


def _expert_kernel(be_ref, nu_ref, nxt_ref, x_ref, w1_ref, b1_ref, w2_ref, b2_ref, y_ref,
                   w1s_ref, w2s_ref, w1b_ref, w2b_ref, sem, *, layer):
    i = pl.program_id(0)
    de = w2b_ref.shape[0]
    used = i < nu_ref[0]
    e = be_ref[i]

    def fetch(ex):
        return (pltpu.make_async_copy(w1_ref.at[layer, ex], w1s_ref, sem.at[0]),
                pltpu.make_async_copy(w2_ref.at[layer, ex], w2s_ref, sem.at[1]))

    @pl.when(i == 0)
    def _():
        for cp in fetch(e):
            cp.start()

    @pl.when(used & ((i == 0) | (e != be_ref[jnp.maximum(i - 1, 0)])))
    def _():
        for cp in fetch(e):
            cp.wait()
        w1b_ref[...] = w1s_ref[...].astype(BF16)
        w2b_ref[...] = w2s_ref[...].astype(BF16)
        nxt = nxt_ref[e]

        @pl.when(nxt >= 0)
        def _():
            for cp in fetch(nxt):
                cp.start()

    @pl.when(used)
    def _():
        hh = jnp.dot(_unpack_bf16_pairs(x_ref[...]), w1b_ref[...], preferred_element_type=F32) + b1_ref[0, 0]
        gate = jnp.minimum(hh[:, :de], SWIGLU_LIMIT)
        lin = jnp.clip(hh[:, de:], -SWIGLU_LIMIT, SWIGLU_LIMIT)
        act = (lin + 1.0) * (gate * jax.nn.sigmoid(SWIGLU_ALPHA * gate))
        y = jnp.dot(act.astype(BF16), w2b_ref[...], preferred_element_type=F32) + b2_ref[0, 0]
        y_ref[...] = _pack_bf16_pairs(y.astype(BF16).astype(F32))

    @pl.when(jnp.logical_not(used))
    def _():
        y_ref[...] = jnp.zeros_like(y_ref)


def _experts(xp, lay, layer, w1, b1, w2, b2):
    p_total, dh = xp.shape
    depth, ne, d, de2 = w1.shape
    assert d == 2 * dh
    de = w2.shape[2]
    bm = MOE_BLOCK
    bmap = lambda i, be, nu, nxt: (layer, be[i], 0, 0)
    grid_spec = pltpu.PrefetchScalarGridSpec(
        num_scalar_prefetch=3,
        grid=(p_total // bm,),
        in_specs=[pl.BlockSpec((bm, dh), lambda i, be, nu, nxt: (jnp.minimum(i, nu[0] - 1), 0)),
                  pl.BlockSpec(memory_space=pl.ANY),
                  pl.BlockSpec((1, 1, 1, de2), bmap),
                  pl.BlockSpec(memory_space=pl.ANY),
                  pl.BlockSpec((1, 1, 1, d), bmap)],
        out_specs=pl.BlockSpec((bm, dh), lambda i, be, nu, nxt: (i, 0)),
        scratch_shapes=[pltpu.VMEM((d, de2), F32), pltpu.VMEM((de, d), F32),
                        pltpu.VMEM((d, de2), BF16), pltpu.VMEM((de, d), BF16),
                        pltpu.SemaphoreType.DMA((2,))],
    )
    return pl.pallas_call(
        functools.partial(_expert_kernel, layer=layer),
        out_shape=jax.ShapeDtypeStruct((p_total, dh), U32),
        grid_spec=grid_spec,
        compiler_params=_cparams(("arbitrary",)),
        name="moe_experts",
    )(lay["blk_e"], lay["n_used"], lay["nxt"], xp, w1, b1.reshape(depth, ne, 1, de2), w2,
      b2.reshape(depth, ne, 1, d))


def _combine_kernel(grp_ref, off_ref, seg_ref, gs_ref, lp_ref, x_ref, tw_ref, mod_ref, fg_ref, yb_ref,
                    *refs, split_out, nct):
    del grp_ref
    o_refs = refs[:2] if split_out else refs[:1]
    ybuf_ref, sem = refs[len(o_refs):]
    i = pl.program_id(0)
    nt = pl.num_programs(0)
    slot = i % 2

    def gather(tile, s):
        _segment_copies(tile, off_ref, seg_ref, gs_ref, ybuf_ref.at[s], yb_ref, sem.at[s], False)

    @pl.when(i == 0)
    def _():
        ybuf_ref[...] = jnp.zeros_like(ybuf_ref)
        gather(0, 0)

    @pl.when(i + 1 < nt)
    def _():
        gather(i + 1, 1 - slot)

    _segment_wait(i, off_ref, seg_ref, ybuf_ref.at[slot], yb_ref, sem.at[slot])
    lp = lp_ref[...]
    tw = tw_ref[...]
    rid = _local_row_iota(lp.shape[0])
    wsel = jnp.where(rid == lp[:, 0:1], tw[:, 0:1], 0.0)
    for k in range(1, TOP_K):
        wsel = jnp.where(rid == lp[:, k:k + 1], tw[:, k:k + 1], wsel)
    acc = jnp.dot(wsel.astype(BF16), _unpack_bf16_pairs(ybuf_ref[slot]), preferred_element_type=F32)
    x = x_ref[...] + mod_ref[0][5:6] * acc
    if not split_out:
        o_refs[0][...] = x
    else:
        y = x * lax.rsqrt(jnp.mean(x * x, axis=-1, keepdims=True) + NORM_EPS) * fg_ref[...]

        @pl.when(i < nct)
        def _():
            o_refs[0][...] = y

        @pl.when(i >= nct)
        def _():
            o_refs[1][...] = y


def _combine(x, yb, lp, tw, lay, tile_grp, mod, final_g, t_ctx, split_out):
    t, d = x.shape
    tt = MOE_TILE
    nct = t_ctx // tt
    if split_out:
        out_shape = [jax.ShapeDtypeStruct((t_ctx, d), F32), jax.ShapeDtypeStruct((t - t_ctx, d), F32)]
        out_specs = [pl.BlockSpec((tt, d), lambda i, *s: (jnp.minimum(i, nct - 1), 0)),
                     pl.BlockSpec((tt, d), lambda i, *s: (jnp.maximum(i - nct, 0), 0))]
    else:
        out_shape = [jax.ShapeDtypeStruct((t, d), F32)]
        out_specs = [pl.BlockSpec((tt, d), lambda i, *s: (i, 0))]
    grid_spec = pltpu.PrefetchScalarGridSpec(
        num_scalar_prefetch=4,
        grid=(t // tt,),
        in_specs=[pl.BlockSpec((tt, E_PAD), lambda i, *s: (i, 0)),
                  pl.BlockSpec((tt, d), lambda i, *s: (i, 0)),
                  pl.BlockSpec((tt, E_PAD), lambda i, *s: (i, 0)),
                  pl.BlockSpec((1, MOD_ROWS, d), lambda i, grp, *s: (grp[i], 0, 0)),
                  pl.BlockSpec((1, d), lambda i, *s: (0, 0)),
                  pl.BlockSpec(memory_space=pl.ANY)],
        out_specs=out_specs,
        scratch_shapes=[pltpu.VMEM((2, SEG_ROWS, d // 2), U32), pltpu.SemaphoreType.DMA((2,))],
    )
    return pl.pallas_call(
        functools.partial(_combine_kernel, split_out=split_out, nct=nct),
        out_shape=out_shape,
        grid_spec=grid_spec,
        compiler_params=_cparams(("arbitrary",)),
        name="moe_combine",
    )(tile_grp, lay["off"], lay["seg"], lay["gseg"], lp, x, tw, mod, final_g.reshape(1, d), yb)


def _segment_layout(cnt, t):
    bm = MOE_BLOCK
    nt = t // MOE_TILE
    n = cnt[:, 0, :N_EXPERTS]
    seg = ((n + (SEG_ALIGN - 1)) // SEG_ALIGN) * SEG_ALIGN
    off = jnp.cumsum(seg, axis=1) - seg
    tot = jnp.sum(seg, axis=0)
    padded = ((tot + bm - 1) // bm) * bm
    p_end = jnp.cumsum(padded)
    gseg = (p_end - padded)[None, :] + jnp.cumsum(seg, axis=0) - seg
    max_rows = t * TOP_K + nt * N_EXPERTS * (SEG_ALIGN - 1)
    n_blk = -(-max_rows // bm) + N_EXPERTS
    blk_start = jnp.arange(n_blk, dtype=jnp.int32) * bm
    blk_e = jnp.minimum(jnp.sum(p_end[None, :] <= blk_start[:, None], axis=1), N_EXPERTS - 1)
    ex = jnp.arange(N_EXPERTS, dtype=jnp.int32)
    later = (ex[None, :] > ex[:, None]) & (tot[None, :] > 0)
    nxt = jnp.where(jnp.any(later, axis=1), jnp.argmax(later, axis=1), -1)
    i32 = lambda a: a.reshape(-1).astype(jnp.int32)
    return dict(seg=i32(seg), off=i32(off), gseg=i32(gseg), p_end=i32(p_end), blk_e=i32(blk_e),
                n_used=i32(p_end[-1:] // bm), nxt=i32(nxt), p_total=n_blk * bm)


def _tile_groups(t, t_ctx, dec_seq, tm):
    starts = jnp.arange(t // tm, dtype=jnp.int32) * tm
    return jnp.where(starts < t_ctx, 0, 1 + (starts - t_ctx) // dec_seq).astype(jnp.int32)


def kernel(x_prompt, x_sample, state_ret, c, c_ctx, w_ada, b_ada, norm1_g, norm2_g, w_in, ret_decay_logit, w_ret_o, w_four, w_out, w_router, b_router, w1, b1, w2, b2, final_g):
    batch, seq, d = x_prompt.shape
    dec_batch, dec_seq, _ = x_sample.shape
    depth = w_in.shape[0]
    t_ctx, t_lat = batch * seq, dec_batch * dec_seq
    t = t_ctx + t_lat
    tm_in = min(1024, t_ctx, dec_seq)
    tm = min(512, t_ctx, dec_seq)
    assert t_ctx % tm_in == 0 and dec_seq % tm_in == 0 and t_ctx % dec_seq == 0

    n_cond = 16
    cond = jnp.zeros((n_cond, d), F32).at[0].set(c_ctx).at[1:1 + dec_batch].set(c)
    mod = _adaln_mod(cond, w_ada, b_ada).reshape(depth, n_cond, 6, d)
    mod = jnp.pad(mod, ((0, 0), (0, 0), (0, MOD_ROWS - 6), (0, 0)))
    grp_in = _tile_groups(t, t_ctx, dec_seq, tm_in)
    grp = _tile_groups(t, t_ctx, dec_seq, tm)
    grp_moe = _tile_groups(t, t_ctx, dec_seq, MOE_TILE)

    rot = _rotary_tables(dec_seq)
    gw = FOURIER_GROUP_W
    jc = jnp.arange(gw, dtype=jnp.int32)
    ang_c = ((jc[:, None] * jc[None, :]) % gw).astype(F32) * (2.0 * jnp.pi / gw)
    cs = jnp.concatenate([jnp.cos(ang_c), jnp.sin(ang_c)], axis=1).astype(BF16)
    dm_ctx = _split_dft_mats(seq, (seq * gw) ** -0.5)
    dm_lat = _split_dft_mats(dec_seq, (dec_seq * gw) ** -0.5)
    log_g = jax.nn.log_sigmoid(ret_decay_logit.astype(F32))
    wr, br = _router_operands(w_router, b_router)

    xs = [x_prompt.reshape(t_ctx, d), x_sample.reshape(t_lat, d)]
    new_state = None
    for l in range(depth):
        last = l == depth - 1
        tall = len(xs) == 1 and t_ctx % (2 * tm_in) == 0 and dec_seq % (2 * tm_in) == 0
        if tall:
            u = _in_proj(xs, _tile_groups(t, t_ctx, dec_seq, 2 * tm_in), mod[l], norm1_g[l],
                         w_in[l].astype(BF16), 2 * tm_in, 1536)
        else:
            u = _in_proj(xs, grp_in, mod[l], norm1_g[l], w_in[l].astype(BF16), tm_in, 3072)
        tables = _decay_tables(log_g[l], min(RET_CHUNK, seq, dec_seq))
        og, new_state = _retention(u, None, n=seq, nseq=batch, row0=0, tables=tables, rot=None, s0=None,
                                   layer=l, emit_state=True, heads=N_RET_HEADS, depth=depth,
                                   st_prev=new_state)
        og = _retention(u, og, n=dec_seq, nseq=dec_batch, row0=t_ctx, tables=tables, rot=rot,
                        s0=state_ret, layer=l, emit_state=False, heads=1)
        uf = _fourier(u, None, n=seq, nseq=batch, row0=0, cs=cs, mats=dm_ctx, tr=seq, tc=FOURIER_W)
        uf = _fourier(u, uf, n=dec_seq, nseq=dec_batch, row0=t_ctx, cs=cs, mats=dm_lat,
                      tr=min(512, dec_seq), tc=FOURIER_W)
        x, h2, lp, tw, cnt = _merge(
            xs, og, uf, u, grp, mod[l], norm2_g[l], w_ret_o[l].astype(BF16), w_four[l].astype(BF16),
            w_out[l].astype(BF16), wr[l], br[l], tm)
        lay = _segment_layout(cnt, t)
        xp = _dispatch(h2, lp, lay)
        yb = _experts(xp, lay, l, w1, b1, w2, b2)
        xs = _combine(x, yb, lp, tw, lay, grp_moe, mod[l], final_g, t_ctx, split_out=last)

    return (xs[0].reshape(batch, seq, d), xs[1].reshape(dec_batch, dec_seq, d), new_state)
```
